```python
import jax, jax.numpy as jnp
from jax import lax
import numpy as np

D_MODEL = 1024
BATCH = 8
SEQ = 8192
DEPTH = 4

N_META = 16
ATT_BLOCK = 128
ATT_HEADS = 16
ATT_HEAD_DIM = 64
DN_HEADS = 8
DN_HEAD_DIM = 128
DN_CONV = 4
DN_CHUNK = 64
FFN_DIM = 2816
FFN_CONV = 3
N_ATTN_LAYERS = (DEPTH + 1) // 2
N_DN_LAYERS = DEPTH // 2
ATT_HD = ATT_HEADS * ATT_HEAD_DIM
DN_HD = DN_HEADS * DN_HEAD_DIM
ATT_IN = 4 * ATT_HD + ATT_HEADS
DN_IN = 4 * DN_HD + 2 * DN_HEADS
EPS = 1e-6
NEG = -1e30

kernel_name = "fox_gdn_hybrid_trunk"


def rmsnorm(x, g):
    xf = x.astype(jnp.float32)
    y = xf * lax.rsqrt(jnp.mean(xf * xf, axis=-1, keepdims=True) + EPS)
    return (y * g.astype(jnp.float32)).astype(x.dtype)


def l2norm(x):
    xf = x.astype(jnp.float32)
    return xf * lax.rsqrt(jnp.sum(xf * xf, axis=-1, keepdims=True) + EPS)


def causal_dwconv(x, w):
    K, C = w.shape
    return lax.conv_general_dilated(
        x, w[:, None, :].astype(x.dtype), window_strides=(1,), padding=[(K - 1, 0)],
        dimension_numbers=('NWC', 'WIO', 'NWC'), feature_group_count=C)


def forgetting_attention(h, w_in, b_forget, q_gain, k_gain, w_out):
    B, L, _ = h.shape
    pad = (-L) % ATT_BLOCK
    hp = jnp.pad(h, ((0, 0), (pad, 0), (0, 0)))
    Lp = L + pad
    proj = hp @ w_in
    q, k, v, og, fl = jnp.split(proj, [ATT_HD, 2 * ATT_HD, 3 * ATT_HD, 4 * ATT_HD], axis=-1)
    q = rmsnorm(q.reshape(B, Lp, ATT_HEADS, ATT_HEAD_DIM), q_gain).transpose(0, 2, 1, 3)
    k = rmsnorm(k.reshape(B, Lp, ATT_HEADS, ATT_HEAD_DIM), k_gain).transpose(0, 2, 1, 3)
    v = v.reshape(B, Lp, ATT_HEADS, ATT_HEAD_DIM).transpose(0, 2, 1, 3)
    logf = jax.nn.log_sigmoid(fl.astype(jnp.float32) + b_forget.astype(jnp.float32))
    c = jnp.cumsum(logf, axis=1).transpose(0, 2, 1)
    pos = jnp.arange(Lp)
    n_blk = Lp // ATT_BLOCK
    qb = q.reshape(B, ATT_HEADS, n_blk, ATT_BLOCK, ATT_HEAD_DIM).transpose(2, 0, 1, 3, 4)
    cb = c.reshape(B, ATT_HEADS, n_blk, ATT_BLOCK).transpose(2, 0, 1, 3)
    pb = pos.reshape(n_blk, ATT_BLOCK)
    scale = ATT_HEAD_DIM ** -0.5

    def one_block(args):
        q_i, c_i, p_i = args
        s = jnp.einsum('bhqd,bhkd->bhqk', q_i, k).astype(jnp.float32) * scale
        s = s + c_i[..., :, None] - c[:, :, None, :]
        mask = (pos[None, :] <= p_i[:, None]) & (pos[None, :] >= pad)
        p = jax.nn.softmax(jnp.where(mask, s, NEG), axis=-1)
        return jnp.einsum('bhqk,bhkd->bhqd', p.astype(v.dtype), v)

    o = lax.map(one_block, (qb, cb, pb))
    o = o.transpose(1, 0, 3, 2, 4).reshape(B, Lp, ATT_HD)
    o = o * jax.nn.sigmoid(og)
    return (o @ w_out)[:, pad:]


def gated_deltanet(h, w_in, conv_w, a_log, dt_bias, o_gain, w_out):
    B, L, _ = h.shape
    H, Dh, C = DN_HEADS, DN_HEAD_DIM, DN_CHUNK
    pad = (-L) % C
    hp = jnp.pad(h, ((0, 0), (pad, 0), (0, 0)))
    Lp = L + pad
    N = Lp // C
    proj = hp @ w_in
    qkv, og, b_logit, a_logit = jnp.split(proj, [3 * DN_HD, 4 * DN_HD, 4 * DN_HD + H], axis=-1)
    qkv = jax.nn.silu(causal_dwconv(qkv, conv_w))
    q, k, v = jnp.split(qkv, [DN_HD, 2 * DN_HD], axis=-1)
    q = l2norm(q.reshape(B, Lp, H, Dh)) * (Dh ** -0.5)
    k = l2norm(k.reshape(B, Lp, H, Dh))
    v = v.reshape(B, Lp, H, Dh).astype(jnp.float32)
    beta = jax.nn.sigmoid(b_logit.astype(jnp.float32))
    g = -jnp.exp(a_log.astype(jnp.float32)) * jax.nn.softplus(
        a_logit.astype(jnp.float32) + dt_bias.astype(jnp.float32))

    def chunk(t):
        return t.reshape(B, N, C, H, Dh).transpose(0, 3, 1, 2, 4)
    qc, kc, vc = chunk(q), chunk(k), chunk(v)
    bc = beta.reshape(B, N, C, H).transpose(0, 3, 1, 2)
    gc = jnp.cumsum(g.reshape(B, N, C, H).transpose(0, 3, 1, 2), axis=-1)
    idx = jnp.arange(C)
    strict = idx[:, None] > idx[None, :]
    incl = idx[:, None] >= idx[None, :]
    dec = jnp.exp(jnp.where(incl, gc[..., :, None] - gc[..., None, :], -jnp.inf))
    kk = jnp.einsum('bhnid,bhnjd->bhnij', kc, kc)
    A = jnp.where(strict, kk * dec * bc[..., :, None], 0.0)
    lhs = A + jnp.eye(C, dtype=jnp.float32)
    rhs = jnp.concatenate([kc * (bc * jnp.exp(gc))[..., None], vc * bc[..., None]], axis=-1)
    sol = lax.linalg.triangular_solve(lhs, rhs, left_side=True, lower=True, unit_diagonal=True)
    W, U0 = sol[..., :Dh], sol[..., Dh:]
    qk = jnp.einsum('bhnid,bhnjd->bhnij', qc, kc) * dec
    q_dec = qc * jnp.exp(gc)[..., None]
    k_dec = kc * jnp.exp(gc[..., -1:] - gc)[..., None]
    g_last = jnp.exp(gc[..., -1])

    def to_n(t):
        return jnp.moveaxis(t, 2, 0)

    def step(S, xs):
        W_n, U0_n, qk_n, qd_n, kd_n, gl_n = xs
        U = U0_n - jnp.einsum('bhcd,bhvd->bhcv', W_n, S)
        O = jnp.einsum('bhcd,bhvd->bhcv', qd_n, S) + jnp.einsum('bhij,bhjv->bhiv', qk_n, U)
        S = gl_n[..., None, None] * S + jnp.einsum('bhcv,bhcd->bhvd', U, kd_n)
        return S, O

    S0 = jnp.zeros((B, H, Dh, Dh), jnp.float32)
    _, O = lax.scan(step, S0, (to_n(W), to_n(U0), to_n(qk), to_n(q_dec), to_n(k_dec), to_n(g_last)))
    O = O.transpose(1, 0, 3, 2, 4).reshape(B, Lp, H, Dh)
    O = rmsnorm(O, o_gain) * jax.nn.silu(og.reshape(B, Lp, H, Dh).astype(jnp.float32))
    return (O.reshape(B, Lp, DN_HD).astype(h.dtype) @ w_out)[:, pad:]


def conv_ffn(h, w_up, conv_w, w_down):
    u = causal_dwconv(h @ w_up, conv_w)
    gate, up = jnp.split(u, [FFN_DIM], axis=-1)
    return (jax.nn.gelu(gate, approximate=True) * up) @ w_down


def _fwd_setup_inputs(seed: int = 0) -> dict:
    key = jax.random.key(seed)
    ks = jax.random.split(key, 24)
    D = D_MODEL
    nrm = lambda k, shape, s: jax.random.normal(k, shape, jnp.float32) * s
    gain = lambda k, shape: 1.0 + 0.1 * jax.random.normal(k, shape, jnp.float32)
    dt = jnp.exp(jax.random.uniform(ks[12], (N_DN_LAYERS, DN_HEADS), jnp.float32,
                                    np.log(1e-3), np.log(1e-1)))
    return {
        "x": nrm(ks[0], (BATCH, SEQ, D), 1.0),
        "meta_tokens": nrm(ks[1], (N_META, D), 1.0),
        "norm_mix_pre": gain(ks[2], (DEPTH, D)),
        "norm_mix_post": gain(ks[3], (DEPTH, D)),
        "norm_ffn_pre": gain(ks[4], (DEPTH, D)),
        "norm_ffn_post": gain(ks[5], (DEPTH, D)),
        "attn_w_in": nrm(ks[6], (N_ATTN_LAYERS, D, ATT_IN), D ** -0.5),
        "attn_b_forget": jax.random.uniform(ks[7], (N_ATTN_LAYERS, ATT_HEADS), jnp.float32, 2.0, 6.0),
        "attn_q_norm": gain(ks[8], (N_ATTN_LAYERS, ATT_HEAD_DIM)),
        "attn_k_norm": gain(ks[9], (N_ATTN_LAYERS, ATT_HEAD_DIM)),
        "attn_w_out": nrm(ks[10], (N_ATTN_LAYERS, ATT_HD, D), ATT_HD ** -0.5),
        "dn_w_in": nrm(ks[11], (N_DN_LAYERS, D, DN_IN), D ** -0.5),
        "dn_conv": nrm(ks[13], (N_DN_LAYERS, DN_CONV, 3 * DN_HD), DN_CONV ** -0.5),
        "dn_a_log": jnp.log(jax.random.uniform(ks[14], (N_DN_LAYERS, DN_HEADS), jnp.float32, 1.0, 16.0)),
        "dn_dt_bias": dt + jnp.log(-jnp.expm1(-dt)),
        "dn_o_norm": gain(ks[15], (N_DN_LAYERS, DN_HEAD_DIM)),
        "dn_w_out": nrm(ks[16], (N_DN_LAYERS, DN_HD, D), DN_HD ** -0.5),
        "ffn_w_up": nrm(ks[17], (DEPTH, D, 2 * FFN_DIM), D ** -0.5),
        "ffn_conv": nrm(ks[18], (DEPTH, FFN_CONV, 2 * FFN_DIM), FFN_CONV ** -0.5),
        "ffn_w_down": nrm(ks[19], (DEPTH, FFN_DIM, D), FFN_DIM ** -0.5),
    }


def _fwd_reference(x, meta_tokens, norm_mix_pre, norm_mix_post, norm_ffn_pre, norm_ffn_post,
              attn_w_in, attn_b_forget, attn_q_norm, attn_k_norm, attn_w_out,
              dn_w_in, dn_conv, dn_a_log, dn_dt_bias, dn_o_norm, dn_w_out,
              ffn_w_up, ffn_conv, ffn_w_down):
    B = x.shape[0]
    meta = jnp.broadcast_to(meta_tokens[None].astype(x.dtype), (B, N_META, x.shape[-1]))
    h = jnp.concatenate([meta, x], axis=1)
    for i in range(DEPTH):
        j = i // 2
        a = rmsnorm(h, norm_mix_pre[i])
        if i % 2 == 0:
            m = forgetting_attention(a, attn_w_in[j], attn_b_forget[j], attn_q_norm[j],
                                     attn_k_norm[j], attn_w_out[j])
        else:
            m = gated_deltanet(a, dn_w_in[j], dn_conv[j], dn_a_log[j], dn_dt_bias[j],
                               dn_o_norm[j], dn_w_out[j])
        h = h + rmsnorm(m, norm_mix_post[i])
        f = conv_ffn(rmsnorm(h, norm_ffn_pre[i]), ffn_w_up[i], ffn_conv[i], ffn_w_down[i])
        h = h + rmsnorm(f, norm_ffn_post[i])
    return h[:, N_META:]


import jax as _jax
import jax.numpy as _jnp

TWIN_FORMAT = 'train_step'
FWD_PARAMS = ['x', 'meta_tokens', 'norm_mix_pre', 'norm_mix_post', 'norm_ffn_pre', 'norm_ffn_post', 'attn_w_in', 'attn_b_forget', 'attn_q_norm', 'attn_k_norm', 'attn_w_out', 'dn_w_in', 'dn_conv', 'dn_a_log', 'dn_dt_bias', 'dn_o_norm', 'dn_w_out', 'ffn_w_up', 'ffn_conv', 'ffn_w_down']
TWIN_WEIGHTS = ['meta_tokens', 'norm_mix_pre', 'norm_mix_post', 'norm_ffn_pre', 'norm_ffn_post', 'attn_w_in', 'attn_b_forget', 'attn_q_norm', 'attn_k_norm', 'attn_w_out', 'dn_w_in', 'dn_conv', 'dn_a_log', 'dn_dt_bias', 'dn_o_norm', 'dn_w_out', 'ffn_w_up', 'ffn_conv', 'ffn_w_down']
TWIN_DIFF_INPUT = 'x'
TWIN_INPUTS = ['x', 'meta_tokens', 'norm_mix_pre', 'norm_mix_post', 'norm_ffn_pre', 'norm_ffn_post', 'attn_w_in', 'attn_b_forget', 'attn_q_norm', 'attn_k_norm', 'attn_w_out', 'dn_w_in', 'dn_conv', 'dn_a_log', 'dn_dt_bias', 'dn_o_norm', 'dn_w_out', 'ffn_w_up', 'ffn_conv', 'ffn_w_down', 'loss_target', 'm_meta_tokens', 'm_norm_mix_pre', 'm_norm_mix_post', 'm_norm_ffn_pre', 'm_norm_ffn_post', 'm_attn_w_in', 'm_attn_b_forget', 'm_attn_q_norm', 'm_attn_k_norm', 'm_attn_w_out', 'm_dn_w_in', 'm_dn_conv', 'm_dn_a_log', 'm_dn_dt_bias', 'm_dn_o_norm', 'm_dn_w_out', 'm_ffn_w_up', 'm_ffn_conv', 'm_ffn_w_down', 'v_meta_tokens', 'v_norm_mix_pre', 'v_norm_mix_post', 'v_norm_ffn_pre', 'v_norm_ffn_post', 'v_attn_w_in', 'v_attn_b_forget', 'v_attn_q_norm', 'v_attn_k_norm', 'v_attn_w_out', 'v_dn_w_in', 'v_dn_conv', 'v_dn_a_log', 'v_dn_dt_bias', 'v_dn_o_norm', 'v_dn_w_out', 'v_ffn_w_up', 'v_ffn_conv', 'v_ffn_w_down']
TWIN_OUTPUTS = ['loss', 'grad_x', 'grad_meta_tokens', 'grad_norm_mix_pre', 'grad_norm_mix_post', 'grad_norm_ffn_pre', 'grad_norm_ffn_post', 'grad_attn_w_in', 'grad_attn_b_forget', 'grad_attn_q_norm', 'grad_attn_k_norm', 'grad_attn_w_out', 'grad_dn_w_in', 'grad_dn_conv', 'grad_dn_a_log', 'grad_dn_dt_bias', 'grad_dn_o_norm', 'grad_dn_w_out', 'grad_ffn_w_up', 'grad_ffn_conv', 'grad_ffn_w_down', 'delta_meta_tokens', 'delta_norm_mix_pre', 'delta_norm_mix_post', 'delta_norm_ffn_pre', 'delta_norm_ffn_post', 'delta_attn_w_in', 'delta_attn_b_forget', 'delta_attn_q_norm', 'delta_attn_k_norm', 'delta_attn_w_out', 'delta_dn_w_in', 'delta_dn_conv', 'delta_dn_a_log', 'delta_dn_dt_bias', 'delta_dn_o_norm', 'delta_dn_w_out', 'delta_ffn_w_up', 'delta_ffn_conv', 'delta_ffn_w_down', 'new_m_meta_tokens', 'new_m_norm_mix_pre', 'new_m_norm_mix_post', 'new_m_norm_ffn_pre', 'new_m_norm_ffn_post', 'new_m_attn_w_in', 'new_m_attn_b_forget', 'new_m_attn_q_norm', 'new_m_attn_k_norm', 'new_m_attn_w_out', 'new_m_dn_w_in', 'new_m_dn_conv', 'new_m_dn_a_log', 'new_m_dn_dt_bias', 'new_m_dn_o_norm', 'new_m_dn_w_out', 'new_m_ffn_w_up', 'new_m_ffn_conv', 'new_m_ffn_w_down', 'new_v_meta_tokens', 'new_v_norm_mix_pre', 'new_v_norm_mix_post', 'new_v_norm_ffn_pre', 'new_v_norm_ffn_post', 'new_v_attn_w_in', 'new_v_attn_b_forget', 'new_v_attn_q_norm', 'new_v_attn_k_norm', 'new_v_attn_w_out', 'new_v_dn_w_in', 'new_v_dn_conv', 'new_v_dn_a_log', 'new_v_dn_dt_bias', 'new_v_dn_o_norm', 'new_v_dn_w_out', 'new_v_ffn_w_up', 'new_v_ffn_conv', 'new_v_ffn_w_down']
TWIN_LEAF_KINDS = {'loss': 'loss', 'grad_x': 'grad_x', 'grad_meta_tokens': 'grad_w', 'grad_norm_mix_pre': 'grad_w', 'grad_norm_mix_post': 'grad_w', 'grad_norm_ffn_pre': 'grad_w', 'grad_norm_ffn_post': 'grad_w', 'grad_attn_w_in': 'grad_w', 'grad_attn_b_forget': 'grad_w', 'grad_attn_q_norm': 'grad_w', 'grad_attn_k_norm': 'grad_w', 'grad_attn_w_out': 'grad_w', 'grad_dn_w_in': 'grad_w', 'grad_dn_conv': 'grad_w', 'grad_dn_a_log': 'grad_w', 'grad_dn_dt_bias': 'grad_w', 'grad_dn_o_norm': 'grad_w', 'grad_dn_w_out': 'grad_w', 'grad_ffn_w_up': 'grad_w', 'grad_ffn_conv': 'grad_w', 'grad_ffn_w_down': 'grad_w', 'delta_meta_tokens': 'delta_w', 'delta_norm_mix_pre': 'delta_w', 'delta_norm_mix_post': 'delta_w', 'delta_norm_ffn_pre': 'delta_w', 'delta_norm_ffn_post': 'delta_w', 'delta_attn_w_in': 'delta_w', 'delta_attn_b_forget': 'delta_w', 'delta_attn_q_norm': 'delta_w', 'delta_attn_k_norm': 'delta_w', 'delta_attn_w_out': 'delta_w', 'delta_dn_w_in': 'delta_w', 'delta_dn_conv': 'delta_w', 'delta_dn_a_log': 'delta_w', 'delta_dn_dt_bias': 'delta_w', 'delta_dn_o_norm': 'delta_w', 'delta_dn_w_out': 'delta_w', 'delta_ffn_w_up': 'delta_w', 'delta_ffn_conv': 'delta_w', 'delta_ffn_w_down': 'delta_w', 'new_m_meta_tokens': 'new_m', 'new_m_norm_mix_pre': 'new_m', 'new_m_norm_mix_post': 'new_m', 'new_m_norm_ffn_pre': 'new_m', 'new_m_norm_ffn_post': 'new_m', 'new_m_attn_w_in': 'new_m', 'new_m_attn_b_forget': 'new_m', 'new_m_attn_q_norm': 'new_m', 'new_m_attn_k_norm': 'new_m', 'new_m_attn_w_out': 'new_m', 'new_m_dn_w_in': 'new_m', 'new_m_dn_conv': 'new_m', 'new_m_dn_a_log': 'new_m', 'new_m_dn_dt_bias': 'new_m', 'new_m_dn_o_norm': 'new_m', 'new_m_dn_w_out': 'new_m', 'new_m_ffn_w_up': 'new_m', 'new_m_ffn_conv': 'new_m', 'new_m_ffn_w_down': 'new_m', 'new_v_meta_tokens': 'new_v', 'new_v_norm_mix_pre': 'new_v', 'new_v_norm_mix_post': 'new_v', 'new_v_norm_ffn_pre': 'new_v', 'new_v_norm_ffn_post': 'new_v', 'new_v_attn_w_in': 'new_v', 'new_v_attn_b_forget': 'new_v', 'new_v_attn_q_norm': 'new_v', 'new_v_attn_k_norm': 'new_v', 'new_v_attn_w_out': 'new_v', 'new_v_dn_w_in': 'new_v', 'new_v_dn_conv': 'new_v', 'new_v_dn_a_log': 'new_v', 'new_v_dn_dt_bias': 'new_v', 'new_v_dn_o_norm': 'new_v', 'new_v_dn_w_out': 'new_v', 'new_v_ffn_w_up': 'new_v', 'new_v_ffn_conv': 'new_v', 'new_v_ffn_w_down': 'new_v'}


def _forward(args):
    return _fwd_reference(*[args[k] for k in FWD_PARAMS])


def _output_shape():
    def fwd():
        inp = _fwd_setup_inputs(0)
        return _fwd_reference(*[inp[k] for k in FWD_PARAMS])
    out = _jax.eval_shape(fwd)
    return out.shape, out.dtype

N_MICROBATCH = 1
ADAM_LR = 0.001
ADAM_B1 = 0.9
ADAM_B2 = 0.999
ADAM_EPS = 1e-08
ADAM_WD = 0.01
ADAM_STEP = 10
PER_EXAMPLE_BATCH_AXIS = {'x': 0, 'loss_target': 0}
SHARED_INPUTS = []
_WEIGHT_DTYPES = {'meta_tokens': _jnp.float32, 'norm_mix_pre': _jnp.float32, 'norm_mix_post': _jnp.float32, 'norm_ffn_pre': _jnp.float32, 'norm_ffn_post': _jnp.float32, 'attn_w_in': _jnp.float32, 'attn_b_forget': _jnp.float32, 'attn_q_norm': _jnp.float32, 'attn_k_norm': _jnp.float32, 'attn_w_out': _jnp.float32, 'dn_w_in': _jnp.float32, 'dn_conv': _jnp.float32, 'dn_a_log': _jnp.float32, 'dn_dt_bias': _jnp.float32, 'dn_o_norm': _jnp.float32, 'dn_w_out': _jnp.float32, 'ffn_w_up': _jnp.float32, 'ffn_conv': _jnp.float32, 'ffn_w_down': _jnp.float32}
MOMENT_SCALE = {'meta_tokens': 4.960316e-01, 'norm_mix_pre': 3.972150e+00, 'norm_mix_post': 6.347333e+01, 'norm_ffn_pre': 2.749322e+00, 'norm_ffn_post': 6.357099e+01, 'attn_w_in': 2.504953e+00, 'attn_b_forget': 1.900060e+01, 'attn_q_norm': 9.729741e+00, 'attn_k_norm': 9.941538e+00, 'attn_w_out': 4.012022e+00, 'dn_w_in': 1.477354e+00, 'dn_conv': 2.071095e+00, 'dn_a_log': 1.055077e+01, 'dn_dt_bias': 1.019636e+01, 'dn_o_norm': 1.378905e+01, 'dn_w_out': 5.459851e+00, 'ffn_w_up': 1.123036e+00, 'ffn_conv': 1.186402e+00, 'ffn_w_down': 2.112524e+00}


def _to_microbatches(a, axis):
    t = _jnp.moveaxis(a, axis, 0)
    t = t.reshape((N_MICROBATCH, t.shape[0] // N_MICROBATCH) + t.shape[1:])
    return _jnp.moveaxis(t, 1, axis + 1)


def setup_inputs(seed: int = 0) -> dict:
    inp = _fwd_setup_inputs(seed)
    key = _jax.random.fold_in(_jax.random.key(seed), 7919)
    shape, _ = _output_shape()
    out = dict(inp)
    out["loss_target"] = _jax.random.normal(_jax.random.fold_in(key, 0), shape, _jnp.float32)
    for i, name in enumerate(TWIN_WEIGHTS):
        w = inp[name].astype(_jnp.float32)
        if MOMENT_SCALE is None:
            s = _jnp.sqrt(_jnp.mean(_jnp.square(w)) + 1e-30)
        else:
            s = MOMENT_SCALE[name]
        km, kv = _jax.random.split(_jax.random.fold_in(key, i + 1))
        out[name] = w
        out["m_" + name] = s * _jax.random.normal(km, w.shape, _jnp.float32)
        out["v_" + name] = (s * s) * _jax.random.uniform(kv, w.shape, _jnp.float32, 0.5, 1.5)
    if N_MICROBATCH > 1:
        for name, axis in PER_EXAMPLE_BATCH_AXIS.items():
            out[name] = _to_microbatches(out[name], axis)
    return {'x': out['x'], 'meta_tokens': out['meta_tokens'], 'norm_mix_pre': out['norm_mix_pre'], 'norm_mix_post': out['norm_mix_post'], 'norm_ffn_pre': out['norm_ffn_pre'], 'norm_ffn_post': out['norm_ffn_post'], 'attn_w_in': out['attn_w_in'], 'attn_b_forget': out['attn_b_forget'], 'attn_q_norm': out['attn_q_norm'], 'attn_k_norm': out['attn_k_norm'], 'attn_w_out': out['attn_w_out'], 'dn_w_in': out['dn_w_in'], 'dn_conv': out['dn_conv'], 'dn_a_log': out['dn_a_log'], 'dn_dt_bias': out['dn_dt_bias'], 'dn_o_norm': out['dn_o_norm'], 'dn_w_out': out['dn_w_out'], 'ffn_w_up': out['ffn_w_up'], 'ffn_conv': out['ffn_conv'], 'ffn_w_down': out['ffn_w_down'], 'loss_target': out['loss_target'], 'm_meta_tokens': out['m_meta_tokens'], 'm_norm_mix_pre': out['m_norm_mix_pre'], 'm_norm_mix_post': out['m_norm_mix_post'], 'm_norm_ffn_pre': out['m_norm_ffn_pre'], 'm_norm_ffn_post': out['m_norm_ffn_post'], 'm_attn_w_in': out['m_attn_w_in'], 'm_attn_b_forget': out['m_attn_b_forget'], 'm_attn_q_norm': out['m_attn_q_norm'], 'm_attn_k_norm': out['m_attn_k_norm'], 'm_attn_w_out': out['m_attn_w_out'], 'm_dn_w_in': out['m_dn_w_in'], 'm_dn_conv': out['m_dn_conv'], 'm_dn_a_log': out['m_dn_a_log'], 'm_dn_dt_bias': out['m_dn_dt_bias'], 'm_dn_o_norm': out['m_dn_o_norm'], 'm_dn_w_out': out['m_dn_w_out'], 'm_ffn_w_up': out['m_ffn_w_up'], 'm_ffn_conv': out['m_ffn_conv'], 'm_ffn_w_down': out['m_ffn_w_down'], 'v_meta_tokens': out['v_meta_tokens'], 'v_norm_mix_pre': out['v_norm_mix_pre'], 'v_norm_mix_post': out['v_norm_mix_post'], 'v_norm_ffn_pre': out['v_norm_ffn_pre'], 'v_norm_ffn_post': out['v_norm_ffn_post'], 'v_attn_w_in': out['v_attn_w_in'], 'v_attn_b_forget': out['v_attn_b_forget'], 'v_attn_q_norm': out['v_attn_q_norm'], 'v_attn_k_norm': out['v_attn_k_norm'], 'v_attn_w_out': out['v_attn_w_out'], 'v_dn_w_in': out['v_dn_w_in'], 'v_dn_conv': out['v_dn_conv'], 'v_dn_a_log': out['v_dn_a_log'], 'v_dn_dt_bias': out['v_dn_dt_bias'], 'v_dn_o_norm': out['v_dn_o_norm'], 'v_dn_w_out': out['v_dn_w_out'], 'v_ffn_w_up': out['v_ffn_w_up'], 'v_ffn_conv': out['v_ffn_conv'], 'v_ffn_w_down': out['v_ffn_w_down']}


def _loss(weights, diff, rest, loss_target):
    with _jax.named_scope("forward"):
        args = {**rest, TWIN_DIFF_INPUT: diff, **{k: w.astype(_WEIGHT_DTYPES[k]) for k, w in weights.items()}}
        y = _forward(args)
    with _jax.named_scope("loss_head"):
        err = _jnp.square(y.astype(_jnp.float32) - loss_target)
        return 0.5 * _jnp.sum(_jnp.mean(err, axis=-1)) if err.ndim else 0.5 * err


def _adamw(w, g, m, v):
    m = ADAM_B1 * m + (1.0 - ADAM_B1) * g
    v = ADAM_B2 * v + (1.0 - ADAM_B2) * _jnp.square(g)
    m_hat = m / (1.0 - ADAM_B1 ** ADAM_STEP)
    v_hat = v / (1.0 - ADAM_B2 ** ADAM_STEP)
    delta = -ADAM_LR * (m_hat / (_jnp.sqrt(v_hat) + ADAM_EPS) + ADAM_WD * w)
    return delta, m, v


def reference(x, meta_tokens, norm_mix_pre, norm_mix_post, norm_ffn_pre, norm_ffn_post, attn_w_in, attn_b_forget, attn_q_norm, attn_k_norm, attn_w_out, dn_w_in, dn_conv, dn_a_log, dn_dt_bias, dn_o_norm, dn_w_out, ffn_w_up, ffn_conv, ffn_w_down, loss_target, m_meta_tokens, m_norm_mix_pre, m_norm_mix_post, m_norm_ffn_pre, m_norm_ffn_post, m_attn_w_in, m_attn_b_forget, m_attn_q_norm, m_attn_k_norm, m_attn_w_out, m_dn_w_in, m_dn_conv, m_dn_a_log, m_dn_dt_bias, m_dn_o_norm, m_dn_w_out, m_ffn_w_up, m_ffn_conv, m_ffn_w_down, v_meta_tokens, v_norm_mix_pre, v_norm_mix_post, v_norm_ffn_pre, v_norm_ffn_post, v_attn_w_in, v_attn_b_forget, v_attn_q_norm, v_attn_k_norm, v_attn_w_out, v_dn_w_in, v_dn_conv, v_dn_a_log, v_dn_dt_bias, v_dn_o_norm, v_dn_w_out, v_ffn_w_up, v_ffn_conv, v_ffn_w_down):
    given = dict(x=x, meta_tokens=meta_tokens, norm_mix_pre=norm_mix_pre, norm_mix_post=norm_mix_post, norm_ffn_pre=norm_ffn_pre, norm_ffn_post=norm_ffn_post, attn_w_in=attn_w_in, attn_b_forget=attn_b_forget, attn_q_norm=attn_q_norm, attn_k_norm=attn_k_norm, attn_w_out=attn_w_out, dn_w_in=dn_w_in, dn_conv=dn_conv, dn_a_log=dn_a_log, dn_dt_bias=dn_dt_bias, dn_o_norm=dn_o_norm, dn_w_out=dn_w_out, ffn_w_up=ffn_w_up, ffn_conv=ffn_conv, ffn_w_down=ffn_w_down, loss_target=loss_target, m_meta_tokens=m_meta_tokens, m_norm_mix_pre=m_norm_mix_pre, m_norm_mix_post=m_norm_mix_post, m_norm_ffn_pre=m_norm_ffn_pre, m_norm_ffn_post=m_norm_ffn_post, m_attn_w_in=m_attn_w_in, m_attn_b_forget=m_attn_b_forget, m_attn_q_norm=m_attn_q_norm, m_attn_k_norm=m_attn_k_norm, m_attn_w_out=m_attn_w_out, m_dn_w_in=m_dn_w_in, m_dn_conv=m_dn_conv, m_dn_a_log=m_dn_a_log, m_dn_dt_bias=m_dn_dt_bias, m_dn_o_norm=m_dn_o_norm, m_dn_w_out=m_dn_w_out, m_ffn_w_up=m_ffn_w_up, m_ffn_conv=m_ffn_conv, m_ffn_w_down=m_ffn_w_down, v_meta_tokens=v_meta_tokens, v_norm_mix_pre=v_norm_mix_pre, v_norm_mix_post=v_norm_mix_post, v_norm_ffn_pre=v_norm_ffn_pre, v_norm_ffn_post=v_norm_ffn_post, v_attn_w_in=v_attn_w_in, v_attn_b_forget=v_attn_b_forget, v_attn_q_norm=v_attn_q_norm, v_attn_k_norm=v_attn_k_norm, v_attn_w_out=v_attn_w_out, v_dn_w_in=v_dn_w_in, v_dn_conv=v_dn_conv, v_dn_a_log=v_dn_a_log, v_dn_dt_bias=v_dn_dt_bias, v_dn_o_norm=v_dn_o_norm, v_dn_w_out=v_dn_w_out, v_ffn_w_up=v_ffn_w_up, v_ffn_conv=v_ffn_conv, v_ffn_w_down=v_ffn_w_down)
    weights = {n: given[n] for n in TWIN_WEIGHTS}
    shared = {n: given[n] for n in SHARED_INPUTS}
    per_example = {n: given[n] for n in ['x']}
    grad_fn = _jax.value_and_grad(_loss, argnums=(0, 1))

    def one_microbatch(ex, loss_target):
        ex = dict(ex)
        diff = ex.pop(TWIN_DIFF_INPUT)
        return grad_fn(weights, diff, {**shared, **ex}, loss_target)

    if N_MICROBATCH == 1:
        loss, (grad_w, grad_x) = one_microbatch(per_example, given["loss_target"])
    else:
        def body(carry, xs):
            loss_sum, grad_sum = carry
            l_k, (gw_k, gx_k) = one_microbatch(xs[0], xs[1])
            with _jax.named_scope("update"):
                return (loss_sum + l_k, _jax.tree.map(_jnp.add, grad_sum, gw_k)), gx_k

        init = (_jnp.zeros((), _jnp.float32), _jax.tree.map(_jnp.zeros_like, weights))
        (loss, grad_w), grad_x = _jax.lax.scan(body, init, (per_example, given["loss_target"]))
    with _jax.named_scope("update"):
        delta_w, new_m, new_v = {}, {}, {}
        for n in TWIN_WEIGHTS:
            delta_w[n], new_m[n], new_v[n] = _adamw(weights[n], grad_w[n], given["m_" + n], given["v_" + n])
    return (loss, grad_x, *[grad_w[n] for n in TWIN_WEIGHTS], *[delta_w[n] for n in TWIN_WEIGHTS],
            *[new_m[n] for n in TWIN_WEIGHTS], *[new_v[n] for n in TWIN_WEIGHTS])
```

```python
import functools

import jax
import jax.numpy as jnp
from jax import lax
from jax.experimental import pallas as pl
from jax.experimental.pallas import tpu as pltpu

F32 = jnp.float32
BF16 = jnp.bfloat16
LANE = 128
SUB = 8
N_DEV = 8
N_META = 16
ATT_HEADS, ATT_HEAD_DIM = 16, 64
DN_HEADS, DN_HEAD_DIM, DN_CHUNK, DN_CONV = 8, 128, 64, 4
FFN_CONV = 3
EPS = 1e-6
NEG = -1e30
ADAM_LR, ADAM_B1, ADAM_B2, ADAM_EPS, ADAM_WD, ADAM_STEP = 0.001, 0.9, 0.999, 1e-08, 0.01, 10
HI = lax.Precision.HIGHEST
VMEM_LIMIT = 56 * 1024 * 1024


def _tile(n, target, align=LANE):
    if n <= target:
        return n
    best = None
    for t in range(align, target + 1, align):
        if n % t == 0:
            best = t
    assert best is not None, (n, target, align)
    return best


def _iota(shape, dim):
    return lax.broadcasted_iota(jnp.int32, shape, dim)


def _colsum8(x):
    r, c = x.shape
    return x.reshape(r // SUB, SUB, c).sum(axis=0)


def _cparams(*sem):
    return pltpu.CompilerParams(dimension_semantics=sem, vmem_limit_bytes=VMEM_LIMIT)


def _sigmoid(x):
    return 1.0 / (1.0 + jnp.exp(-x))


def _softplus(x):
    return jnp.maximum(x, 0.0) + jnp.log(1.0 + jnp.exp(-jnp.abs(x)))


def _accum(ref, part, first):
    @pl.when(first)
    def _():
        ref[...] = part

    @pl.when(jnp.logical_not(first))
    def _():
        ref[...] += part


def matmul(a, b, *, trans_a=False, trans_b=False, out_dtype=F32, tm=640, tn=512, tk=1024, name="matmul"):
    if trans_a:
        kdim, m = a.shape
    else:
        m, kdim = a.shape
    if trans_b:
        n, kb = b.shape
    else:
        kb, n = b.shape
    assert kb == kdim, (a.shape, b.shape, trans_a, trans_b)
    tm, tn, tk = _tile(m, tm), _tile(n, tn), _tile(kdim, tk)
    nk = kdim // tk
    dims = (((0 if trans_a else 1,), (1 if trans_b else 0,)), ((), ()))
    cdt = BF16

    def body(a_ref, b_ref, o_ref, acc_ref):
        k = pl.program_id(2)
        part = lax.dot_general(a_ref[...].astype(cdt), b_ref[...].astype(cdt), dims, preferred_element_type=F32)
        _accum(acc_ref, part, k == 0)

        @pl.when(k == nk - 1)
        def _():
            o_ref[...] = acc_ref[...].astype(o_ref.dtype)

    a_spec = pl.BlockSpec((tk, tm), lambda i, j, k: (k, i)) if trans_a else pl.BlockSpec((tm, tk), lambda i, j, k: (i, k))
    b_spec = pl.BlockSpec((tn, tk), lambda i, j, k: (j, k)) if trans_b else pl.BlockSpec((tk, tn), lambda i, j, k: (k, j))
    return pl.pallas_call(
        body,
        name=name,
        grid=(m // tm, n // tn, nk),
        in_specs=[a_spec, b_spec],
        out_specs=pl.BlockSpec((tm, tn), lambda i, j, k: (i, j)),
        out_shape=jax.ShapeDtypeStruct((m, n), out_dtype),
        scratch_shapes=[pltpu.VMEM((tm, tn), F32)],
        compiler_params=_cparams("parallel", "parallel", "arbitrary"),
    )(a, b)


def rms_fwd(x, g, *, res=None, out_dtype, name):
    n, d = x.shape
    tr = _tile(n, 640, SUB)

    def body(*refs):
        x_ref, g_ref = refs[0], refs[1]
        o_ref = refs[-1]
        xv = x_ref[...]
        y = xv * lax.rsqrt(jnp.mean(xv * xv, axis=-1, keepdims=True) + EPS) * g_ref[...]
        if res is not None:
            y = y + refs[2][...]
        o_ref[...] = y.astype(o_ref.dtype)

    row = pl.BlockSpec((tr, d), lambda i: (i, 0))
    ins = [x, g] + ([res] if res is not None else [])
    return pl.pallas_call(
        body, name=name, grid=(n // tr,),
        in_specs=[row, pl.BlockSpec((1, d), lambda i: (0, 0))] + ([row] if res is not None else []),
        out_specs=row, out_shape=jax.ShapeDtypeStruct((n, d), out_dtype),
        compiler_params=_cparams("parallel"),
    )(*ins)


def rms_bwd(x, g, dy, *, res=None, pad, name):
    n, d = x.shape
    tr = _tile(n, 640, SUB)

    def body(*refs):
        x_ref, g_ref, dy_ref = refs[:3]
        dx_ref, dg_ref = refs[-2:]
        i = pl.program_id(0)
        xv = x_ref[...]
        r = lax.rsqrt(jnp.mean(xv * xv, axis=-1, keepdims=True) + EPS)
        xh = xv * r
        dyv = dy_ref[...].astype(F32)
        gdy = dyv * g_ref[...]
        dx = r * (gdy - xh * jnp.mean(xh * gdy, axis=-1, keepdims=True))
        if res is not None:
            dx = dx + refs[3][...]
        rows = i * tr + _iota((tr, 1), 0)
        dx_ref[...] = jnp.where(rows >= pad, dx, 0.0)
        _accum(dg_ref, _colsum8(dyv * xh), i == 0)

    row = pl.BlockSpec((tr, d), lambda i: (i, 0))
    ins = [x, g, dy] + ([res] if res is not None else [])
    return pl.pallas_call(
        body, name=name, grid=(n // tr,),
        in_specs=[row, pl.BlockSpec((1, d), lambda i: (0, 0)), row] + ([row] if res is not None else []),
        out_specs=[row, pl.BlockSpec((SUB, d), lambda i: (0, 0))],
        out_shape=[jax.ShapeDtypeStruct((n, d), F32), jax.ShapeDtypeStruct((SUB, d), F32)],
        compiler_params=_cparams("arbitrary"),
    )(*ins)


def _shift_down(cur, prev8, s):
    if s == 0:
        return cur
    out = pltpu.roll(cur, s, 0)
    row = _iota(cur.shape, 0)
    for r in range(s):
        out = jnp.where(row == r, prev8[SUB - s + r:SUB - s + r + 1, :], out)
    return out


def _shift_up(cur, next8, s):
    if s == 0:
        return cur
    tr = cur.shape[0]
    out = pltpu.roll(cur, tr - s, 0)
    row = _iota(cur.shape, 0)
    for r in range(s):
        out = jnp.where(row == tr - s + r, next8[r:r + 1, :], out)
    return out


def _conv_rows(cur, prev8, w):
    kw = w.shape[0]
    acc = w[kw - 1:kw, :] * cur
    for k in range(kw - 1):
        acc = acc + w[k:k + 1, :] * _shift_down(cur, prev8, kw - 1 - k)
    return acc


def conv_transpose(dy, w, *, out_dtype, name):
    n, c = dy.shape
    kw = w.shape[0]
    tr = _tile(n, 640, SUB)
    tc = _tile(c, 512)
    nb8 = n // SUB

    def body(dy_ref, nx_ref, w_ref, o_ref):
        i = pl.program_id(0)
        cur = dy_ref[...].astype(F32)
        nxt = jnp.where(i == pl.num_programs(0) - 1, 0.0, nx_ref[...].astype(F32))
        wv = w_ref[...]
        acc = wv[kw - 1:kw, :] * cur
        for k in range(kw - 1):
            acc = acc + wv[k:k + 1, :] * _shift_up(cur, nxt, kw - 1 - k)
        o_ref[...] = acc.astype(o_ref.dtype)

    return pl.pallas_call(
        body, name=name, grid=(n // tr, c // tc),
        in_specs=[pl.BlockSpec((tr, tc), lambda i, j: (i, j)),
                  pl.BlockSpec((SUB, tc), lambda i, j: (jnp.minimum((i + 1) * (tr // SUB), nb8 - 1), j)),
                  pl.BlockSpec((kw, tc), lambda i, j: (0, j))],
        out_specs=pl.BlockSpec((tr, tc), lambda i, j: (i, j)),
        out_shape=jax.ShapeDtypeStruct((n, c), out_dtype),
        compiler_params=_cparams("parallel", "parallel"),
    )(dy, dy, w)


_GELU_C = 0.7978845608028654
_GELU_A = 0.044715


def _gelu(x):
    return 0.5 * x * (1.0 + jnp.tanh(_GELU_C * (x + _GELU_A * x * x * x)))


def _gelu_grad(x):
    th = jnp.tanh(_GELU_C * (x + _GELU_A * x * x * x))
    return 0.5 * (1.0 + th) + 0.5 * x * (1.0 - th * th) * _GELU_C * (1.0 + 3.0 * _GELU_A * x * x)


def conv_glu_fwd(u, cw, *, name):
    n, f2 = u.shape
    f = f2 // 2
    tr = _tile(n, 640, SUB)
    tc = _tile(f, 512)
    nf = f // tc
    r8 = tr // SUB

    def body(ug_ref, uu_ref, pg_ref, pu_ref, wg_ref, wu_ref, o_ref):
        first = pl.program_id(0) == 0
        pg = jnp.where(first, 0.0, pg_ref[...])
        pu = jnp.where(first, 0.0, pu_ref[...])
        gate = _conv_rows(ug_ref[...], pg, wg_ref[...])
        up = _conv_rows(uu_ref[...], pu, wu_ref[...])
        o_ref[...] = (_gelu(gate) * up).astype(o_ref.dtype)

    prev = lambda off: pl.BlockSpec((SUB, tc), lambda i, j: (jnp.maximum(i * r8 - 1, 0), j + off))
    return pl.pallas_call(
        body, name=name, grid=(n // tr, nf),
        in_specs=[pl.BlockSpec((tr, tc), lambda i, j: (i, j)), pl.BlockSpec((tr, tc), lambda i, j: (i, j + nf)),
                  prev(0), prev(nf),
                  pl.BlockSpec((FFN_CONV, tc), lambda i, j: (0, j)), pl.BlockSpec((FFN_CONV, tc), lambda i, j: (0, j + nf))],
        out_specs=pl.BlockSpec((tr, tc), lambda i, j: (i, j)),
        out_shape=jax.ShapeDtypeStruct((n, f), BF16),
        compiler_params=_cparams("parallel", "parallel"),
    )(u, u, u, u, cw, cw)


def conv_glu_bwd(u, cw, dact, *, name):
    n, f2 = u.shape
    f = f2 // 2
    tr = _tile(n, 640, SUB)
    tc = _tile(f, 512)
    nf = f // tc
    r8 = tr // SUB

    def body(us_ref, up_ref, ps_ref, pp_ref, ws_ref, wp_ref, da_ref, o_ref, dw_ref):
        j, i = pl.program_id(0), pl.program_id(1)
        first = i == 0
        ps = jnp.where(first, 0.0, ps_ref[...])
        pp = jnp.where(first, 0.0, pp_ref[...])
        cur = us_ref[...]
        mine = _conv_rows(cur, ps, ws_ref[...])
        other = _conv_rows(up_ref[...], pp, wp_ref[...])
        da = da_ref[...].astype(F32)
        is_gate = j < nf
        d_mine = jnp.where(is_gate, da * other * _gelu_grad(mine), da * _gelu(other))
        o_ref[...] = d_mine
        for k in range(FFN_CONV):
            part = _colsum8(d_mine * _shift_down(cur, ps, FFN_CONV - 1 - k))

            @pl.when(first)
            def _():
                dw_ref[k] = part

            @pl.when(jnp.logical_not(first))
            def _():
                dw_ref[k] += part

    part_of = lambda j: (j + nf) % (2 * nf)
    return pl.pallas_call(
        body, name=name, grid=(2 * nf, n // tr),
        in_specs=[pl.BlockSpec((tr, tc), lambda j, i: (i, j)), pl.BlockSpec((tr, tc), lambda j, i: (i, part_of(j))),
                  pl.BlockSpec((SUB, tc), lambda j, i: (jnp.maximum(i * r8 - 1, 0), j)),
                  pl.BlockSpec((SUB, tc), lambda j, i: (jnp.maximum(i * r8 - 1, 0), part_of(j))),
                  pl.BlockSpec((FFN_CONV, tc), lambda j, i: (0, j)), pl.BlockSpec((FFN_CONV, tc), lambda j, i: (0, part_of(j))),
                  pl.BlockSpec((tr, tc), lambda j, i: (i, j % nf))],
        out_specs=[pl.BlockSpec((tr, tc), lambda j, i: (i, j)), pl.BlockSpec((FFN_CONV, SUB, tc), lambda j, i: (0, 0, j))],
        out_shape=[jax.ShapeDtypeStruct((n, f2), F32), jax.ShapeDtypeStruct((FFN_CONV, SUB, f2), F32)],
        compiler_params=_cparams("parallel", "arbitrary"),
    )(u, u, u, u, cw, cw, dact)


def cumsum_rows(x, *, reverse, name):
    n, c = x.shape
    tr = LANE
    nb = n // tr

    def body(x_ref, o_ref, carry_ref):
        i = pl.program_id(0)

        @pl.when(i == 0)
        def _():
            carry_ref[...] = jnp.zeros_like(carry_ref)

        r, cc = _iota((tr, tr), 0), _iota((tr, tr), 1)
        tri = jnp.where((cc >= r) if reverse else (cc <= r), 1.0, 0.0).astype(F32)
        out = jnp.dot(tri, x_ref[...], precision=HI, preferred_element_type=F32) + carry_ref[...]
        o_ref[...] = out
        carry_ref[...] = out[0:1, :] if reverse else out[tr - 1:tr, :]

    idx = (lambda i: (nb - 1 - i, 0)) if reverse else (lambda i: (i, 0))
    return pl.pallas_call(
        body, name=name, grid=(nb,),
        in_specs=[pl.BlockSpec((tr, c), idx)], out_specs=pl.BlockSpec((tr, c), idx),
        out_shape=jax.ShapeDtypeStruct((n, c), F32), scratch_shapes=[pltpu.VMEM((1, c), F32)],
        compiler_params=_cparams("arbitrary"),
    )(x)


def _group_sum64(x):
    r, c = x.shape
    a, b = _iota((LANE, LANE), 0), _iota((LANE, LANE), 1)
    bd = jnp.where((a // 64) == (b // 64), 1.0, 0.0).astype(F32)
    parts = [jnp.dot(x[:, k * LANE:(k + 1) * LANE], bd, precision=HI, preferred_element_type=F32) for k in range(c // LANE)]
    return parts[0] if len(parts) == 1 else jnp.concatenate(parts, axis=1)


def attn_prep(proj, qg, kg, bf, *, hd, name):
    n = proj.shape[0]
    tr = _tile(n, 640, SUB)
    scale = ATT_HEAD_DIM ** -0.5
    nh = hd // LANE

    def body(q_ref, k_ref, v_ref, f_ref, qg_ref, kg_ref, bf_ref, qo_ref, ko_ref, vo_ref, lf_ref):
        def norm(x, g):
            ms = _group_sum64(x * x) * (1.0 / ATT_HEAD_DIM)
            return x * lax.rsqrt(ms + EPS) * g

        qo_ref[...] = (norm(q_ref[...], qg_ref[...]) * scale).astype(qo_ref.dtype)
        ko_ref[...] = norm(k_ref[...], kg_ref[...]).astype(ko_ref.dtype)
        vo_ref[...] = v_ref[...].astype(vo_ref.dtype)
        lf_ref[...] = -_softplus(-(f_ref[...] + bf_ref[...]))

    col = lambda c: pl.BlockSpec((tr, hd), lambda i: (i, c))
    vec = lambda w: pl.BlockSpec((1, w), lambda i: (0, 0))
    return pl.pallas_call(
        body, name=name, grid=(n // tr,),
        in_specs=[col(0), col(1), col(2), pl.BlockSpec((tr, LANE), lambda i: (i, 4 * nh)), vec(hd), vec(hd), vec(LANE)],
        out_specs=[col(0), col(0), col(0), pl.BlockSpec((tr, LANE), lambda i: (i, 0))],
        out_shape=[jax.ShapeDtypeStruct((n, hd), BF16)] * 3 + [jax.ShapeDtypeStruct((n, LANE), F32)],
        compiler_params=_cparams("parallel"),
    )(proj, proj, proj, proj, qg, kg, bf)


def _half_mask(shape):
    return _iota(shape, 1) < ATT_HEAD_DIM


def flash_fwd(qs, kn, v, proj, ct, *, pad, t, name):
    n, hd = qs.shape
    npair = hd // LANE
    nb = n // t
    gate0 = 3 * npair

    def body(q_ref, k_ref, v_ref, g_ref, c_ref, o_ref, og_ref, lse_ref, m_ref, l_ref, acc_ref):
        i, j = pl.program_id(1), pl.program_id(2)

        @pl.when(j == 0)
        def _():
            m_ref[...] = jnp.full_like(m_ref, NEG)
            l_ref[...] = jnp.zeros_like(l_ref)
            acc_ref[...] = jnp.zeros_like(acc_ref)

        @pl.when(j <= i)
        def _():
            q, k, vv = q_ref[...], k_ref[...], v_ref[...]
            half0 = _half_mask(q.shape)
            rowpos = i * t + _iota((t, t), 0)
            colpos = j * t + _iota((t, t), 1)
            mask = (colpos <= rowpos) & (colpos >= pad)
            for hh in range(2):
                qm = jnp.where(half0 if hh == 0 else jnp.logical_not(half0), q, jnp.zeros_like(q))
                s = lax.dot_general(qm, k, (((1,), (1,)), ((), ())), preferred_element_type=F32)
                s = jnp.where(mask, s - c_ref[0, hh:hh + 1, :], NEG)
                m_prev = m_ref[hh]
                m_new = jnp.maximum(m_prev, jnp.max(s, axis=1, keepdims=True))
                alpha = jnp.exp(m_prev - m_new)
                p = jnp.where(mask, jnp.exp(s - m_new[:, 0:1]), 0.0)
                l_ref[hh] = alpha * l_ref[hh] + jnp.sum(p, axis=1, keepdims=True)
                acc_ref[hh] = alpha * acc_ref[hh] + jnp.dot(p.astype(vv.dtype), vv, preferred_element_type=F32)
                m_ref[hh] = m_new

        @pl.when(j == i)
        def _():
            half0 = _half_mask((t, LANE))
            l = jnp.where(half0, l_ref[0], l_ref[1])
            acc = jnp.where(half0, acc_ref[0], acc_ref[1])
            m = jnp.where(half0, m_ref[0], m_ref[1])
            live = l > 0.0
            o = jnp.where(live, acc / jnp.where(live, l, 1.0), 0.0)
            o_ref[...] = o
            og_ref[...] = (o * _sigmoid(g_ref[...])).astype(og_ref.dtype)
            lse_ref[...] = jnp.where(live, m + jnp.log(jnp.where(live, l, 1.0)), 0.0)

    qspec = pl.BlockSpec((t, LANE), lambda p, i, j: (i, p))
    kspec = pl.BlockSpec((t, LANE), lambda p, i, j: (jnp.minimum(j, i), p))
    return pl.pallas_call(
        body, name=name, grid=(npair, nb, nb),
        in_specs=[qspec, kspec, kspec, pl.BlockSpec((t, LANE), lambda p, i, j: (i, gate0 + p)),
                  pl.BlockSpec((1, 2, t), lambda p, i, j: (p, 0, jnp.minimum(j, i)))],
        out_specs=[qspec, qspec, qspec],
        out_shape=[jax.ShapeDtypeStruct((n, hd), F32), jax.ShapeDtypeStruct((n, hd), BF16), jax.ShapeDtypeStruct((n, hd), F32)],
        scratch_shapes=[pltpu.VMEM((2, t, LANE), F32)] * 3,
        compiler_params=_cparams("parallel", "parallel", "arbitrary"),
    )(qs, kn, v, proj, ct)


def attn_bwd_prep(dgated, o, proj, *, hd, name):
    n = o.shape[0]
    tr = _tile(n, 640, SUB)
    gate0 = 3

    def body(dg_ref, o_ref, g_ref, do_ref, dl_ref, dgate_ref):
        dg, ov = dg_ref[...], o_ref[...]
        sg = _sigmoid(g_ref[...])
        do = dg * sg
        do_ref[...] = do.astype(do_ref.dtype)
        dl_ref[...] = _group_sum64(do * ov)
        dgate_ref[...] = dg * ov * sg * (1.0 - sg)

    row = pl.BlockSpec((tr, hd), lambda i: (i, 0))
    return pl.pallas_call(
        body, name=name, grid=(n // tr,),
        in_specs=[row, row, pl.BlockSpec((tr, hd), lambda i: (i, gate0))],
        out_specs=[row, row, row],
        out_shape=[jax.ShapeDtypeStruct((n, hd), BF16), jax.ShapeDtypeStruct((n, hd), F32), jax.ShapeDtypeStruct((n, hd), F32)],
        compiler_params=_cparams("parallel"),
    )(dgated, o, proj)


def flash_bwd(qs, kn, v, do, lse, delta, ct, *, pad, t, name):
    n, hd = qs.shape
    npair = hd // LANE
    nb = n // t

    def body(q_ref, k_ref, v_ref, do_ref, lse_ref, dl_ref, c_ref, dq_ref, dk_ref, dv_ref, dc_ref, dcr_ref, dk_acc, dv_acc, dc_acc):
        j, i = pl.program_id(1), pl.program_id(2)

        @pl.when((j == 0) & (i == 0))
        def _():
            dq_ref[...] = jnp.zeros_like(dq_ref)
            dcr_ref[...] = jnp.zeros_like(dcr_ref)

        @pl.when(i == j)
        def _():
            dk_acc[...] = jnp.zeros_like(dk_acc)
            dv_acc[...] = jnp.zeros_like(dv_acc)
            dc_acc[...] = jnp.zeros_like(dc_acc)

        @pl.when(i >= j)
        def _():
            q, k, vv, dov = q_ref[...], k_ref[...], v_ref[...], do_ref[...]
            half0 = _half_mask(q.shape)
            rowpos = i * t + _iota((t, t), 0)
            colpos = j * t + _iota((t, t), 1)
            mask = (colpos <= rowpos) & (colpos >= pad)
            nt = (((1,), (1,)), ((), ()))
            tn = (((0,), (0,)), ((), ()))
            dq_h, dk_h, dv_h, rs_h = [], [], [], []
            for hh in range(2):
                sel = half0 if hh == 0 else jnp.logical_not(half0)
                qm = jnp.where(sel, q, jnp.zeros_like(q))
                dom = jnp.where(sel, dov, jnp.zeros_like(dov))
                s = lax.dot_general(qm, k, nt, preferred_element_type=F32) - c_ref[0, hh:hh + 1, :]
                lse_h = lse_ref[:, hh * 64:hh * 64 + 1]
                p = jnp.where(mask, jnp.exp(jnp.where(mask, s - lse_h, NEG)), 0.0)
                dp = lax.dot_general(dom, vv, nt, preferred_element_type=F32)
                ds = p * (dp - dl_ref[:, hh * 64:hh * 64 + 1])
                dsb, pb = ds.astype(k.dtype), p.astype(k.dtype)
                dq_h.append(jnp.dot(dsb, k, preferred_element_type=F32))
                dk_h.append(lax.dot_general(dsb, q, tn, preferred_element_type=F32))
                dv_h.append(lax.dot_general(pb, dov, tn, preferred_element_type=F32))
                dc_acc[hh:hh + 1, :] += -jnp.sum(ds, axis=0, keepdims=True)
                rs_h.append(jnp.sum(ds, axis=1, keepdims=True))
            rows = pl.ds(pl.multiple_of(i * t, t), t)
            dq_ref[rows, :] += jnp.where(half0, dq_h[0], dq_h[1])
            dcr_ref[rows, :] += jnp.where(half0, rs_h[0], rs_h[1])
            dk_acc[...] += jnp.where(half0, dk_h[0], dk_h[1])
            dv_acc[...] += jnp.where(half0, dv_h[0], dv_h[1])

        @pl.when(i == nb - 1)
        def _():
            dk_ref[...] = dk_acc[...]
            dv_ref[...] = dv_acc[...]
            dc_ref[0] = dc_acc[...]

    qspec = pl.BlockSpec((t, LANE), lambda p, j, i: (jnp.maximum(i, j), p))
    kspec = pl.BlockSpec((t, LANE), lambda p, j, i: (j, p))
    cspec = pl.BlockSpec((1, 2, t), lambda p, j, i: (p, 0, j))
    return pl.pallas_call(
        body, name=name, grid=(npair, nb, nb),
        in_specs=[qspec, kspec, kspec, qspec, qspec, qspec, cspec],
        out_specs=[pl.BlockSpec((n, LANE), lambda p, j, i: (0, p)), kspec, kspec, cspec, pl.BlockSpec((n, LANE), lambda p, j, i: (0, p))],
        out_shape=[jax.ShapeDtypeStruct((n, hd), F32)] * 3 + [jax.ShapeDtypeStruct((npair, 2, n), F32), jax.ShapeDtypeStruct((n, hd), F32)],
        scratch_shapes=[pltpu.VMEM((t, LANE), F32), pltpu.VMEM((t, LANE), F32), pltpu.VMEM((2, t), F32)],
        compiler_params=_cparams("parallel", "arbitrary", "arbitrary"),
    )(qs, kn, v, do, lse, delta, ct)


def attn_in_bwd(dqs, dkn, proj, qg, kg, bf, dlogf, *, hd, pad, name):
    n = proj.shape[0]
    tr = _tile(n, 640, SUB)
    scale = ATT_HEAD_DIM ** -0.5
    nh = hd // LANE

    def body(dq_ref, dk_ref, q_ref, k_ref, f_ref, qg_ref, kg_ref, bf_ref, dl_ref, oq_ref, ok_ref, of_ref, gq_ref, gk_ref, gb_ref):
        i = pl.program_id(0)

        def back(x, g, dy):
            r = lax.rsqrt(_group_sum64(x * x) * (1.0 / ATT_HEAD_DIM) + EPS)
            xh = x * r
            gdy = dy * g
            dx = r * (gdy - xh * _group_sum64(xh * gdy) * (1.0 / ATT_HEAD_DIM))
            return dx, _colsum8(dy * xh)

        dxq, gq = back(q_ref[...], qg_ref[...], dq_ref[...] * scale)
        dxk, gk = back(k_ref[...], kg_ref[...], dk_ref[...])
        oq_ref[...] = dxq.astype(oq_ref.dtype)
        ok_ref[...] = dxk.astype(ok_ref.dtype)
        rows = i * tr + _iota((tr, 1), 0)
        dfl = jnp.where(rows >= pad, dl_ref[...] * _sigmoid(-(f_ref[...] + bf_ref[...])), 0.0)
        of_ref[...] = dfl.astype(of_ref.dtype)
        _accum(gq_ref, gq, i == 0)
        _accum(gk_ref, gk, i == 0)
        _accum(gb_ref, _colsum8(dfl), i == 0)

    row = pl.BlockSpec((tr, hd), lambda i: (i, 0))
    col = lambda c: pl.BlockSpec((tr, hd), lambda i: (i, c))
    nar = pl.BlockSpec((tr, LANE), lambda i: (i, 0))
    vec = lambda w: pl.BlockSpec((1, w), lambda i: (0, 0))
    acc = lambda w: pl.BlockSpec((SUB, w), lambda i: (0, 0))
    return pl.pallas_call(
        body, name=name, grid=(n // tr,),
        in_specs=[row, row, col(0), col(1), pl.BlockSpec((tr, LANE), lambda i: (i, 4 * nh)), vec(hd), vec(hd), vec(LANE), nar],
        out_specs=[row, row, nar, acc(hd), acc(hd), acc(LANE)],
        out_shape=[jax.ShapeDtypeStruct((n, hd), BF16)] * 2 + [jax.ShapeDtypeStruct((n, LANE), BF16),
                   jax.ShapeDtypeStruct((SUB, hd), F32), jax.ShapeDtypeStruct((SUB, hd), F32), jax.ShapeDtypeStruct((SUB, LANE), F32)],
        compiler_params=_cparams("arbitrary"),
    )(dqs, dkn, proj, proj, proj, qg, kg, bf, dlogf)


def _silu(x):
    return x * _sigmoid(x)


def _silu_grad(x):
    s = _sigmoid(x)
    return s * (1.0 + x * (1.0 - s))


def gdn_prep(proj, cw, *, hd, name):
    n = proj.shape[0]
    tr = _tile(n, 640, SUB)
    nh = hd // LANE
    r8 = tr // SUB
    qscale = DN_HEAD_DIM ** -0.5

    def body(x_ref, p_ref, w_ref, o_ref):
        i, c = pl.program_id(0), pl.program_id(1)
        prev = jnp.where(i == 0, 0.0, p_ref[...])
        s = _silu(_conv_rows(x_ref[...], prev, w_ref[...]))
        r = lax.rsqrt(jnp.sum(s * s, axis=-1, keepdims=True) + EPS)
        mult = jnp.where(c < nh, r * qscale, jnp.where(c < 2 * nh, r, 1.0))
        o_ref[...] = s * mult

    return pl.pallas_call(
        body, name=name, grid=(n // tr, 3 * nh),
        in_specs=[pl.BlockSpec((tr, LANE), lambda i, c: (i, c)),
                  pl.BlockSpec((SUB, LANE), lambda i, c: (jnp.maximum(i * r8 - 1, 0), c)),
                  pl.BlockSpec((DN_CONV, LANE), lambda i, c: (0, c))],
        out_specs=pl.BlockSpec((tr, LANE), lambda i, c: (i, c)),
        out_shape=jax.ShapeDtypeStruct((n, 3 * hd), F32),
        compiler_params=_cparams("parallel", "parallel"),
    )(proj, proj, cw)


def _chunk_tri(reverse):
    r, c = _iota((LANE, LANE), 0), _iota((LANE, LANE), 1)
    same = (r // DN_CHUNK) == (c // DN_CHUNK)
    return jnp.where(same & ((c >= r) if reverse else (c <= r)), 1.0, 0.0).astype(F32)


def gdn_gates(proj, alog, dtb, *, hd, name):
    n = proj.shape[0]
    gcol = 4 * (hd // LANE)

    def body(x_ref, a_ref, d_ref, o_ref):
        x = x_ref[...]
        lane = _iota(x.shape, 1)
        g = -jnp.exp(a_ref[...]) * _softplus(x + d_ref[...])
        gc = jnp.dot(_chunk_tri(False), jnp.where((lane >= DN_HEADS) & (lane < 2 * DN_HEADS), g, 0.0), precision=HI,
                     preferred_element_type=F32)
        o_ref[...] = jnp.where(lane < DN_HEADS, _sigmoid(x), gc)

    vec = pl.BlockSpec((1, LANE), lambda i: (0, 0))
    return pl.pallas_call(
        body, name=name, grid=(n // LANE,),
        in_specs=[pl.BlockSpec((LANE, LANE), lambda i: (i, gcol)), vec, vec],
        out_specs=pl.BlockSpec((LANE, LANE), lambda i: (i, 0)),
        out_shape=jax.ShapeDtypeStruct((n, LANE), F32),
        compiler_params=_cparams("parallel"),
    )(proj, alog, dtb)


def _mm(a, b, ca=1, cb=0):
    return lax.dot_general(a, b, (((ca,), (cb,)), ((), ())), precision=HI, preferred_element_type=F32)


def _gdn_common(q, k, v, beta, gc_c, gc_r):
    r, c = _iota((LANE, LANE), 0), _iota((LANE, LANE), 1)
    same = (r // DN_CHUNK) == (c // DN_CHUNK)
    incl, strict = same & (r >= c), same & (r > c)
    d = jnp.exp(jnp.where(incl, gc_c - gc_r, NEG))
    kk = _mm(k, k, 1, 1)
    ahat = jnp.where(strict, kk * d, 0.0)
    a = ahat * beta
    eye = jnp.where(r == c, 1.0, 0.0).astype(F32)
    t = eye - a
    pw = _mm(a, a)
    for step in range(5):
        t = t + _mm(t, pw)
        if step < 4:
            pw = _mm(pw, pw)
    row = _iota((LANE, 1), 0)
    gl0 = jnp.sum(jnp.where(row == DN_CHUNK - 1, gc_c, 0.0), axis=0, keepdims=True)
    gl1 = jnp.sum(jnp.where(row == LANE - 1, gc_c, 0.0), axis=0, keepdims=True)
    gam = jnp.exp(gc_c)
    lam = jnp.exp(jnp.where(row < DN_CHUNK, gl0, gl1) - gc_c)
    kb, vb = k * (beta * gam), v * beta
    qk = _mm(q, k, 1, 1)
    return dict(incl=incl, strict=strict, d=d, kk=kk, ahat=ahat, t=t, gam=gam, lam=lam, kb=kb, vb=vb, w=_mm(t, kb), u0=_mm(t, vb),
                qk=qk, pm=jnp.where(incl, qk * d, 0.0), qg=q * gam, kl=k * lam, g0=jnp.exp(gl0), g1=jnp.exp(gl1))


def _gdn_states(cm, s0):
    c = DN_CHUNK
    u_a = cm["u0"][:c] - _mm(cm["w"][:c], s0, 1, 1)
    s1 = cm["g0"] * s0 + _mm(u_a, cm["kl"][:c], 0, 0)
    u_b = cm["u0"][c:] - _mm(cm["w"][c:], s1, 1, 1)
    s2 = cm["g1"] * s1 + _mm(u_b, cm["kl"][c:], 0, 0)
    return u_a, s1, u_b, s2


def gdn_chunk_fwd(qkv, bg, bgt, proj, ogain, *, hd, name):
    n = qkv.shape[0]
    nb = n // LANE
    nh = hd // LANE
    c = DN_CHUNK

    def body(q_ref, k_ref, v_ref, bg_ref, bgt_ref, g_ref, gain_ref, o_ref, og_ref, hist_ref, s_ref):
        @pl.when(pl.program_id(0) == 0)
        def _():
            s_ref[...] = jnp.zeros_like(s_ref)

        hist_ref[0] = s_ref[...]
        for h in range(nh):
            cols = slice(h * LANE, (h + 1) * LANE)
            cm = _gdn_common(q_ref[:, cols], k_ref[:, cols], v_ref[:, cols], bg_ref[:, h:h + 1],
                             bg_ref[:, nh + h:nh + h + 1], bgt_ref[nh + h:nh + h + 1, :])
            s0 = s_ref[h]
            u_a, s1, u_b, s2 = _gdn_states(cm, s0)
            u_all = jnp.concatenate([u_a, u_b], axis=0)
            o = jnp.concatenate([_mm(cm["qg"][:c], s0, 1, 1), _mm(cm["qg"][c:], s1, 1, 1)], axis=0) + _mm(cm["pm"], u_all)
            s_ref[h] = s2
            o_ref[:, cols] = o
            rn = lax.rsqrt(jnp.mean(o * o, axis=-1, keepdims=True) + EPS)
            og_ref[:, cols] = (o * rn * gain_ref[...] * _silu(g_ref[:, cols])).astype(og_ref.dtype)

    col = lambda cc: pl.BlockSpec((LANE, hd), lambda b: (b, cc))
    return pl.pallas_call(
        body, name=name, grid=(nb,),
        in_specs=[col(0), col(1), col(2), pl.BlockSpec((LANE, LANE), lambda b: (b, 0)),
                  pl.BlockSpec((2 * nh, LANE), lambda b: (0, b)), pl.BlockSpec((LANE, hd), lambda b: (b, 3)),
                  pl.BlockSpec((1, LANE), lambda b: (0, 0))],
        out_specs=[col(0), col(0), pl.BlockSpec((1, nh, LANE, LANE), lambda b: (b, 0, 0, 0))],
        out_shape=[jax.ShapeDtypeStruct((n, hd), F32), jax.ShapeDtypeStruct((n, hd), BF16),
                   jax.ShapeDtypeStruct((nb, nh, LANE, LANE), F32)],
        scratch_shapes=[pltpu.VMEM((nh, LANE, LANE), F32)],
        compiler_params=_cparams("arbitrary"),
    )(qkv, qkv, qkv, bg, bgt, proj, ogain)


def gdn_chunk_bwd(qkv, bg, bgt, proj, ogain, o_raw, dog, hist, *, hd, name):
    n = qkv.shape[0]
    nb = n // LANE
    nh = hd // LANE
    c = DN_CHUNK

    def body(q_ref, k_ref, v_ref, bg_ref, bgt_ref, g_ref, gain_ref, o_ref, dog_ref, hist_ref,
             dq_ref, dk_ref, dv_ref, dgate_ref, dbg_ref, dgt_ref, dgain_ref, ds_ref):
        first = pl.program_id(0) == 0

        @pl.when(first)
        def _():
            ds_ref[...] = jnp.zeros_like(ds_ref)

        lane = _iota((LANE, LANE), 1)
        row = _iota((LANE, 1), 0)
        dbg = jnp.zeros((LANE, LANE), F32)
        dgain = jnp.zeros((SUB, LANE), F32)
        for h in range(nh):
            cols = slice(h * LANE, (h + 1) * LANE)
            q, k, v = q_ref[:, cols], k_ref[:, cols], v_ref[:, cols]
            beta = bg_ref[:, h:h + 1]
            cm = _gdn_common(q, k, v, beta, bg_ref[:, nh + h:nh + h + 1], bgt_ref[nh + h:nh + h + 1, :])
            s0 = hist_ref[0, h]
            u_a, s1, u_b, _ = _gdn_states(cm, s0)
            u_all = jnp.concatenate([u_a, u_b], axis=0)
            o, gate, d_out, gain = o_ref[:, cols], g_ref[:, cols], dog_ref[:, cols], gain_ref[...]
            rn = lax.rsqrt(jnp.mean(o * o, axis=-1, keepdims=True) + EPS)
            xh = o * rn
            d_on = d_out * _silu(gate)
            dgate_ref[:, cols] = d_out * xh * gain * _silu_grad(gate)
            dgain = dgain + _colsum8(d_on * xh)
            gdy = d_on * gain
            d_o = rn * (gdy - xh * jnp.mean(xh * gdy, axis=-1, keepdims=True))
            pt_do = _mm(cm["pm"], d_o, 0, 0)
            ds_in = ds_ref[h]
            du_b = _mm(cm["kl"][c:], ds_in, 1, 1) + pt_do[c:]
            dkl_b = _mm(u_b, ds_in)
            dqg_b = _mm(d_o[c:], s1)
            dg1 = jnp.sum(jnp.sum(ds_in * s1, axis=1, keepdims=True), axis=0, keepdims=True)
            dw_b = -_mm(du_b, s1)
            ds_mid = cm["g1"] * ds_in + _mm(d_o[c:], cm["qg"][c:], 0, 0) - _mm(du_b, cm["w"][c:], 0, 0)
            du_a = _mm(cm["kl"][:c], ds_mid, 1, 1) + pt_do[:c]
            dkl_a = _mm(u_a, ds_mid)
            dqg_a = _mm(d_o[:c], s0)
            dg0 = jnp.sum(jnp.sum(ds_mid * s0, axis=1, keepdims=True), axis=0, keepdims=True)
            dw_a = -_mm(du_a, s0)
            ds_ref[h] = cm["g0"] * ds_mid + _mm(d_o[:c], cm["qg"][:c], 0, 0) - _mm(du_a, cm["w"][:c], 0, 0)
            du = jnp.concatenate([du_a, du_b], axis=0)
            dkl = jnp.concatenate([dkl_a, dkl_b], axis=0)
            dqg = jnp.concatenate([dqg_a, dqg_b], axis=0)
            dw = jnp.concatenate([dw_a, dw_b], axis=0)
            t, d, gam, lam = cm["t"], cm["d"], cm["gam"], cm["lam"]
            dp = jnp.where(cm["incl"], _mm(d_o, u_all, 1, 1), 0.0)
            dt = _mm(dw, cm["kb"], 1, 1) + _mm(du, cm["vb"], 1, 1)
            dkb = _mm(t, dw, 0, 0)
            dvb = _mm(t, du, 0, 0)
            da = jnp.where(cm["strict"], -_mm(_mm(t, dt, 0, 0), t, 1, 1), 0.0)
            kb_k = jnp.sum(dkb * k, axis=1, keepdims=True)
            dbeta = jnp.sum(da * cm["ahat"], axis=1, keepdims=True) + gam * kb_k + jnp.sum(dvb * v, axis=1, keepdims=True)
            dahat = da * beta
            dkk = dahat * d
            dqk = dp * d
            e = (dahat * cm["kk"] + dp * cm["qk"]) * d
            dk_ref[:, cols] = (_mm(dkk, k) + _mm(dkk, k, 0, 0) + _mm(dqk, q, 0, 0) + dkb * (beta * gam) + dkl * lam)
            dq_ref[:, cols] = _mm(dqk, k) + dqg * gam
            dv_ref[:, cols] = dvb * beta
            dgam = beta * kb_k + jnp.sum(dqg * q, axis=1, keepdims=True)
            dlam_lam = jnp.sum(dkl * k, axis=1, keepdims=True) * lam
            dgl0 = jnp.sum(jnp.where(row < c, dlam_lam, 0.0), axis=0, keepdims=True) + dg0 * cm["g0"]
            dgl1 = jnp.sum(jnp.where(row >= c, dlam_lam, 0.0), axis=0, keepdims=True) + dg1 * cm["g1"]
            dgc = (jnp.sum(e, axis=1, keepdims=True) + dgam * gam - dlam_lam
                   + jnp.where(row == c - 1, dgl0, 0.0) + jnp.where(row == LANE - 1, dgl1, 0.0))
            dgt_ref[h:h + 1, :] = -jnp.sum(e, axis=0, keepdims=True)
            dbg = dbg + jnp.where(lane == h, dbeta, 0.0) + jnp.where(lane == nh + h, dgc, 0.0)
        dbg_ref[...] = dbg
        _accum(dgain_ref, dgain, first)

    rev = lambda b: nb - 1 - b
    col = lambda cc: pl.BlockSpec((LANE, hd), lambda b: (rev(b), cc))
    return pl.pallas_call(
        body, name=name, grid=(nb,),
        in_specs=[col(0), col(1), col(2), pl.BlockSpec((LANE, LANE), lambda b: (rev(b), 0)),
                  pl.BlockSpec((2 * nh, LANE), lambda b: (0, rev(b))), pl.BlockSpec((LANE, hd), lambda b: (rev(b), 3)),
                  pl.BlockSpec((1, LANE), lambda b: (0, 0)), col(0), col(0),
                  pl.BlockSpec((1, nh, LANE, LANE), lambda b: (rev(b), 0, 0, 0))],
        out_specs=[col(0), col(0), col(0), col(0), pl.BlockSpec((LANE, LANE), lambda b: (rev(b), 0)),
                   pl.BlockSpec((nh, LANE), lambda b: (0, rev(b))), pl.BlockSpec((SUB, LANE), lambda b: (0, 0))],
        out_shape=[jax.ShapeDtypeStruct((n, hd), F32)] * 4 + [jax.ShapeDtypeStruct((n, LANE), F32),
                   jax.ShapeDtypeStruct((nh, n), F32), jax.ShapeDtypeStruct((SUB, LANE), F32)],
        scratch_shapes=[pltpu.VMEM((nh, LANE, LANE), F32)],
        compiler_params=_cparams("arbitrary"),
    )(qkv, qkv, qkv, bg, bgt, proj, ogain, o_raw, dog, hist)


def gdn_gates_bwd(proj, alog, dtb, dbg, *, hd, pad, name):
    n = proj.shape[0]
    gcol = 4 * (hd // LANE)

    def body(x_ref, a_ref, d_ref, dbg_ref, o_ref, da_ref, dd_ref):
        i = pl.program_id(0)
        x = x_ref[...]
        lane = _iota(x.shape, 1)
        rows = i * LANE + _iota((LANE, 1), 0)
        isg = (lane >= DN_HEADS) & (lane < 2 * DN_HEADS)
        dbgv = jnp.where(rows >= pad, dbg_ref[...], 0.0)
        dg = jnp.dot(_chunk_tri(True), jnp.where(isg, dbgv, 0.0), precision=HI, preferred_element_type=F32)
        ea = jnp.exp(a_ref[...])
        z = x + d_ref[...]
        dg = jnp.where(rows >= pad, dg, 0.0)
        dz = jnp.where(isg, dg * (-ea) * _sigmoid(z), 0.0)
        sb = _sigmoid(x)
        o_ref[...] = jnp.where(lane < DN_HEADS, dbgv * sb * (1.0 - sb), dz).astype(o_ref.dtype)
        _accum(da_ref, _colsum8(jnp.where(isg, dg * (-ea) * _softplus(z), 0.0)), i == 0)
        _accum(dd_ref, _colsum8(dz), i == 0)

    vec = pl.BlockSpec((1, LANE), lambda i: (0, 0))
    blk = pl.BlockSpec((LANE, LANE), lambda i: (i, 0))
    acc = pl.BlockSpec((SUB, LANE), lambda i: (0, 0))
    return pl.pallas_call(
        body, name=name, grid=(n // LANE,),
        in_specs=[pl.BlockSpec((LANE, LANE), lambda i: (i, gcol)), vec, vec, blk],
        out_specs=[blk, acc, acc],
        out_shape=[jax.ShapeDtypeStruct((n, LANE), BF16), jax.ShapeDtypeStruct((SUB, LANE), F32), jax.ShapeDtypeStruct((SUB, LANE), F32)],
        compiler_params=_cparams("arbitrary"),
    )(proj, alog, dtb, dbg)


def gdn_prep_bwd(proj, cw, dqkv, *, hd, name):
    n = proj.shape[0]
    tr = _tile(n, 640, SUB)
    nh = hd // LANE
    r8 = tr // SUB
    qscale = DN_HEAD_DIM ** -0.5

    def body(x_ref, p_ref, w_ref, dq_ref, dk_ref, dv_ref, o_ref, dw_ref):
        c, i = pl.program_id(0), pl.program_id(1)
        first = i == 0
        prev = jnp.where(first, 0.0, p_ref[...])
        cur = x_ref[...]
        cv = _conv_rows(cur, prev, w_ref[...])
        s = _silu(cv)
        r = lax.rsqrt(jnp.sum(s * s, axis=-1, keepdims=True) + EPS)
        y = s * r
        dy = jnp.where(c < nh, dq_ref[...] * qscale, dk_ref[...])
        ds_norm = r * (dy - y * jnp.sum(dy * y, axis=-1, keepdims=True))
        dcv = jnp.where(c < 2 * nh, ds_norm, dv_ref[...]) * _silu_grad(cv)
        o_ref[...] = dcv
        for k in range(DN_CONV):
            part = _colsum8(dcv * _shift_down(cur, prev, DN_CONV - 1 - k))

            @pl.when(first)
            def _():
                dw_ref[k] = part

            @pl.when(jnp.logical_not(first))
            def _():
                dw_ref[k] += part

    blk = lambda f: pl.BlockSpec((tr, LANE), f)
    return pl.pallas_call(
        body, name=name, grid=(3 * nh, n // tr),
        in_specs=[blk(lambda c, i: (i, c)), pl.BlockSpec((SUB, LANE), lambda c, i: (jnp.maximum(i * r8 - 1, 0), c)),
                  pl.BlockSpec((DN_CONV, LANE), lambda c, i: (0, c)),
                  blk(lambda c, i: (i, jnp.minimum(c, nh - 1))), blk(lambda c, i: (i, jnp.clip(c - nh, 0, nh - 1))),
                  blk(lambda c, i: (i, jnp.clip(c - 2 * nh, 0, nh - 1)))],
        out_specs=[blk(lambda c, i: (i, c)), pl.BlockSpec((DN_CONV, SUB, LANE), lambda c, i: (0, 0, c))],
        out_shape=[jax.ShapeDtypeStruct((n, 3 * hd), F32), jax.ShapeDtypeStruct((DN_CONV, SUB, 3 * hd), F32)],
        compiler_params=_cparams("parallel", "arbitrary"),
    )(proj, proj, cw, *dqkv)


def loss_head(h, target, *, x0, name):
    n, d = h.shape
    tr = LANE
    nb0 = x0 // tr

    def body(h_ref, t_ref, dh_ref, sq_ref):
        i = pl.program_id(0)
        live = i >= nb0
        err = jnp.where(live, h_ref[...] - t_ref[...], 0.0)
        dh_ref[...] = err * (1.0 / d)
        _accum(sq_ref, _colsum8(err * err), i == 0)

    row = pl.BlockSpec((tr, d), lambda i: (i, 0))
    return pl.pallas_call(
        body, name=name, grid=(n // tr,),
        in_specs=[row, pl.BlockSpec((tr, d), lambda i: (jnp.maximum(i - nb0, 0), 0))],
        out_specs=[row, pl.BlockSpec((SUB, d), lambda i: (0, 0))],
        out_shape=[jax.ShapeDtypeStruct((n, d), F32), jax.ShapeDtypeStruct((SUB, d), F32)],
        compiler_params=_cparams("arbitrary"),
    )(h, target)


def adamw(w, g, m, v, *, name):
    r, c = w.shape
    tr = _tile(r, 512, SUB) if r % SUB == 0 else r
    c1 = 1.0 / (1.0 - ADAM_B1 ** ADAM_STEP)
    c2 = 1.0 / (1.0 - ADAM_B2 ** ADAM_STEP)

    def body(w_ref, g_ref, m_ref, v_ref, d_ref, mo_ref, vo_ref):
        gv = g_ref[...]
        mn = ADAM_B1 * m_ref[...] + (1.0 - ADAM_B1) * gv
        vn = ADAM_B2 * v_ref[...] + (1.0 - ADAM_B2) * (gv * gv)
        d_ref[...] = -ADAM_LR * ((mn * c1) / (jnp.sqrt(vn * c2) + ADAM_EPS) + ADAM_WD * w_ref[...])
        mo_ref[...] = mn
        vo_ref[...] = vn

    blk = pl.BlockSpec((tr, c), lambda i: (i, 0))
    return pl.pallas_call(
        body, name=name, grid=(r // tr,), in_specs=[blk] * 4, out_specs=[blk] * 3,
        out_shape=[jax.ShapeDtypeStruct((r, c), F32)] * 3, compiler_params=_cparams("parallel"),
    )(w, g, m, v)


def _row(v, width=None):
    v = v.astype(F32).reshape(1, -1)
    if width is not None and v.shape[1] < width:
        v = jnp.pad(v, ((0, 0), (0, width - v.shape[1])))
    return v


def _fold8(p):
    return jnp.sum(p, axis=-2)


def local_step(x, target, w):
    seq, d = x.shape
    pad = (-(N_META + seq)) % LANE
    x0 = pad + N_META
    n = x0 + seq
    depth = w["g_pre"].shape[0]
    hd_a = ATT_HEADS * ATT_HEAD_DIM
    hd_d = DN_HEADS * DN_HEAD_DIM
    t_att = _tile(n, 640)
    h = jnp.concatenate([jnp.zeros((pad, d), F32), w["meta"].astype(F32), x], axis=0)
    saved = []
    for i in range(depth):
        j = i // 2
        s = dict(h=h)
        s["a"] = rms_fwd(h, _row(w["g_pre"][i]), out_dtype=BF16, name="rms_pre")
        if i % 2 == 0:
            s["proj"] = proj = matmul(s["a"], w["attn_w_in"][j], name="mm_attn_in")
            qg, kg = _row(jnp.tile(w["attn_qg"][j], ATT_HEADS)), _row(jnp.tile(w["attn_kg"][j], ATT_HEADS))
            bf = _row(w["attn_b"][j], LANE)
            s["qs"], s["kn"], s["v"], logf = attn_prep(proj, qg, kg, bf, hd=hd_a, name="attn_prep")
            c = cumsum_rows(logf, reverse=False, name="cumsum_fwd")
            s["ct"] = c[:, :ATT_HEADS].T.reshape(ATT_HEADS // 2, 2, n)
            s["o"], s["og"], s["lse"] = flash_fwd(s["qs"], s["kn"], s["v"], proj, s["ct"], pad=pad, t=t_att, name="flash_fwd")
            s["m"] = matmul(s["og"], w["attn_w_out"][j], name="mm_attn_out")
        else:
            s["proj"] = proj = matmul(s["a"], w["dn_w_in"][j], name="mm_dn_in")
            s["qkv"] = gdn_prep(proj, w["dn_conv"][j], hd=hd_d, name="gdn_prep")
            alog = jnp.pad(_row(w["dn_alog"][j]), ((0, 0), (DN_HEADS, LANE - 2 * DN_HEADS)))
            dtb = jnp.pad(_row(w["dn_dtb"][j]), ((0, 0), (DN_HEADS, LANE - 2 * DN_HEADS)))
            s["bg"] = gdn_gates(proj, alog, dtb, hd=hd_d, name="gdn_gates")
            s["bgt"] = s["bg"][:, :2 * DN_HEADS].T
            s["o"], s["og"], s["hist"] = gdn_chunk_fwd(s["qkv"], s["bg"], s["bgt"], proj, _row(w["dn_og"][j]), hd=hd_d, name="gdn_fwd")
            s["m"] = matmul(s["og"], w["dn_w_out"][j], name="mm_dn_out")
        s["h_mid"] = rms_fwd(s["m"], _row(w["g_post"][i]), res=h, out_dtype=F32, name="rms_post")
        s["b"] = rms_fwd(s["h_mid"], _row(w["g_fpre"][i]), out_dtype=BF16, name="rms_fpre")
        s["u"] = matmul(s["b"], w["ffn_w_up"][i], name="mm_ffn_up")
        s["act"] = conv_glu_fwd(s["u"], w["ffn_conv"][i], name="ffn_glu")
        s["f"] = matmul(s["act"], w["ffn_w_down"][i], tk=1408, name="mm_ffn_down")
        h = rms_fwd(s["f"], _row(w["g_fpost"][i]), res=s["h_mid"], out_dtype=F32, name="rms_fpost")
        saved.append(s)

    dh, sq = loss_head(h, target, x0=x0, name="loss_head")
    loss = 0.5 * jnp.sum(sq) / d

    g = {k: [None] * depth for k in ("g_pre", "g_post", "g_fpre", "g_fpost", "ffn_w_up", "ffn_conv", "ffn_w_down")}
    for k in ("attn_w_in", "attn_b", "attn_qg", "attn_kg", "attn_w_out", "dn_w_in", "dn_conv", "dn_alog", "dn_dtb", "dn_og", "dn_w_out"):
        g[k] = [None] * (depth // 2)
    for i in reversed(range(depth)):
        j = i // 2
        s = saved[i]
        proj = s["proj"]
        df, p8 = rms_bwd(s["f"], _row(w["g_fpost"][i]), dh, pad=pad, name="rms_fpost_bwd")
        g["g_fpost"][i] = _fold8(p8)
        dact = matmul(df, w["ffn_w_down"][i], trans_b=True, out_dtype=BF16, name="mm_ffn_down_dx")
        g["ffn_w_down"][i] = matmul(s["act"], df, trans_a=True, name="mm_ffn_down_dw")
        duc, p8 = conv_glu_bwd(s["u"], w["ffn_conv"][i], dact, name="ffn_glu_bwd")
        g["ffn_conv"][i] = _fold8(p8)
        du = conv_transpose(duc, w["ffn_conv"][i], out_dtype=BF16, name="ffn_conv_t")
        db = matmul(du, w["ffn_w_up"][i], trans_b=True, name="mm_ffn_up_dx")
        g["ffn_w_up"][i] = matmul(s["b"], du, trans_a=True, name="mm_ffn_up_dw")
        dh_mid, p8 = rms_bwd(s["h_mid"], _row(w["g_fpre"][i]), db, res=dh, pad=pad, name="rms_fpre_bwd")
        g["g_fpre"][i] = _fold8(p8)
        dm, p8 = rms_bwd(s["m"], _row(w["g_post"][i]), dh_mid, pad=pad, name="rms_post_bwd")
        g["g_post"][i] = _fold8(p8)
        if i % 2 == 0:
            g["attn_w_out"][j] = matmul(s["og"], dm, trans_a=True, name="mm_attn_out_dw")
            dgated = matmul(dm, w["attn_w_out"][j], trans_b=True, name="mm_attn_out_dx")
            do, delta, dgate = attn_bwd_prep(dgated, s["o"], proj, hd=hd_a, name="attn_bwd_prep")
            dqs, dkn, dv, dct, dcr = flash_bwd(s["qs"], s["kn"], s["v"], do, s["lse"], delta, s["ct"], pad=pad, t=t_att, name="flash_bwd")
            dc = jnp.pad(dct.reshape(ATT_HEADS, n).T + dcr[:, ::ATT_HEAD_DIM], ((0, 0), (0, LANE - ATT_HEADS)))
            dlogf = cumsum_rows(dc, reverse=True, name="cumsum_bwd")
            qg, kg = _row(jnp.tile(w["attn_qg"][j], ATT_HEADS)), _row(jnp.tile(w["attn_kg"][j], ATT_HEADS))
            bf = _row(w["attn_b"][j], LANE)
            dq_raw, dk_raw, dfl, gq8, gk8, gb8 = attn_in_bwd(dqs, dkn, proj, qg, kg, bf, dlogf, hd=hd_a, pad=pad, name="attn_in_bwd")
            g["attn_qg"][j] = _fold8(gq8).reshape(ATT_HEADS, ATT_HEAD_DIM).sum(axis=0)
            g["attn_kg"][j] = _fold8(gk8).reshape(ATT_HEADS, ATT_HEAD_DIM).sum(axis=0)
            g["attn_b"][j] = _fold8(gb8)[:ATT_HEADS]
            dproj = jnp.concatenate([dq_raw, dk_raw, dv.astype(BF16), dgate.astype(BF16), dfl], axis=1)
            w_in, key, n_in = w["attn_w_in"][j], "attn_w_in", 4 * hd_a + ATT_HEADS
        else:
            g["dn_w_out"][j] = matmul(s["og"], dm, trans_a=True, name="mm_dn_out_dw")
            dgated = matmul(dm, w["dn_w_out"][j], trans_b=True, name="mm_dn_out_dx")
            alog = jnp.pad(_row(w["dn_alog"][j]), ((0, 0), (DN_HEADS, LANE - 2 * DN_HEADS)))
            dtb = jnp.pad(_row(w["dn_dtb"][j]), ((0, 0), (DN_HEADS, LANE - 2 * DN_HEADS)))
            dq, dk, dv, dgate, dbg, dgt, gain8 = gdn_chunk_bwd(s["qkv"], s["bg"], s["bgt"], proj, _row(w["dn_og"][j]), s["o"], dgated,
                                                              s["hist"], hd=hd_d, name="gdn_bwd")
            g["dn_og"][j] = _fold8(gain8)
            dbg = dbg + jnp.pad(dgt.T, ((0, 0), (DN_HEADS, LANE - 2 * DN_HEADS)))
            dgl, da8, dd8 = gdn_gates_bwd(proj, alog, dtb, dbg, hd=hd_d, pad=pad, name="gdn_gates_bwd")
            g["dn_alog"][j] = _fold8(da8)[DN_HEADS:2 * DN_HEADS]
            g["dn_dtb"][j] = _fold8(dd8)[DN_HEADS:2 * DN_HEADS]
            dcv, p8 = gdn_prep_bwd(proj, w["dn_conv"][j], (dq, dk, dv), hd=hd_d, name="gdn_prep_bwd")
            g["dn_conv"][j] = _fold8(p8)
            dqkv = conv_transpose(dcv, w["dn_conv"][j], out_dtype=BF16, name="gdn_conv_t")
            dproj = jnp.concatenate([dqkv, dgate.astype(BF16), dgl], axis=1)
            w_in, key, n_in = w["dn_w_in"][j], "dn_w_in", 4 * hd_d + 2 * DN_HEADS
        da = matmul(dproj, w_in, trans_b=True, name="mm_in_dx")
        g[key][j] = matmul(s["a"], dproj, trans_a=True, name="mm_in_dw")[:, :n_in]
        dh, p8 = rms_bwd(s["h"], _row(w["g_pre"][i]), da, res=dh_mid, pad=pad, name="rms_pre_bwd")
        g["g_pre"][i] = _fold8(p8)

    grads = {k: jnp.stack(v) for k, v in g.items()}
    grads["meta"] = dh[pad:x0]
    return loss, dh[x0:], grads


_ANY = pl.BlockSpec(memory_space=pl.ANY)


def _mesh_place():
    x, y, c = lax.axis_index("x"), lax.axis_index("y"), lax.axis_index("c")
    return x, y, c, 4 * x + 2 * y + c


def _peer(x, y, c, k):
    px, py, pc = (1 - x if k & 4 else x), (1 - y if k & 2 else y), (1 - c if k & 1 else c)
    return (px, py, pc), 4 * px + 2 * py + pc


def exchange(bufs, modes, *, name):
    nbuf = len(bufs)

    def body(*refs):
        ins, outs = refs[:nbuf], refs[nbuf:2 * nbuf]
        send_sems, recv_sems, loc_sems = refs[2 * nbuf:]
        x, y, c, me = _mesh_place()

        def src(b, idx):
            return ins[b] if modes[b] == "gather" else ins[b].at[idx]

        local = [pltpu.make_async_copy(src(b, me), outs[b].at[me], loc_sems.at[b]) for b in range(nbuf)]
        for cp in local:
            cp.start()
        sends, recvs = [], []
        for k in range(1, N_DEV):
            peer, pidx = _peer(x, y, c, k)
            for b in range(nbuf):
                cp = pltpu.make_async_remote_copy(src_ref=src(b, pidx), dst_ref=outs[b].at[me], send_sem=send_sems.at[b, k - 1],
                                                  recv_sem=recv_sems.at[b, k - 1], device_id=peer, device_id_type=pl.DeviceIdType.MESH)
                cp.start()
                sends.append(cp)
                recvs.append(pltpu.make_async_remote_copy(src_ref=src(b, pidx), dst_ref=outs[b].at[pidx], send_sem=send_sems.at[b, k - 1],
                                                          recv_sem=recv_sems.at[b, k - 1], device_id=peer,
                                                          device_id_type=pl.DeviceIdType.MESH))
        for cp in recvs:
            cp.wait_recv()
        for cp in sends:
            cp.wait_send()
        for cp in local:
            cp.wait()

    outs = [jax.ShapeDtypeStruct((N_DEV,) + (b.shape if m == "gather" else b.shape[1:]), b.dtype) for b, m in zip(bufs, modes)]
    return pl.pallas_call(
        body, name=name, in_specs=[_ANY] * nbuf, out_specs=[_ANY] * nbuf, out_shape=outs,
        scratch_shapes=[pltpu.SemaphoreType.DMA((nbuf, N_DEV - 1)), pltpu.SemaphoreType.DMA((nbuf, N_DEV - 1)),
                        pltpu.SemaphoreType.DMA((nbuf,))],
        compiler_params=pltpu.CompilerParams(has_side_effects=True),
    )(*bufs)


def slot_sum(x, *, name):
    _, r, c = x.shape
    tr = _tile(r, 1024, 16)

    def body(x_ref, o_ref):
        acc = x_ref[0].astype(F32)
        for d in range(1, N_DEV):
            acc = acc + x_ref[d].astype(F32)
        o_ref[...] = acc

    return pl.pallas_call(
        body, name=name, grid=(r // tr,), in_specs=[pl.BlockSpec((N_DEV, tr, c), lambda i: (0, i, 0))],
        out_specs=pl.BlockSpec((tr, c), lambda i: (i, 0)), out_shape=jax.ShapeDtypeStruct((r, c), F32),
        compiler_params=_cparams("parallel"),
    )(x)


def _pack(parts, dtype, lead=()):
    nl = len(lead)
    flat = jnp.concatenate([p.astype(dtype).reshape(lead + (-1,)) for p in parts], axis=nl)
    tot = flat.shape[nl]
    rows = -(-tot // (16 * LANE)) * 16
    flat = jnp.pad(flat, [(0, 0)] * nl + [(0, rows * LANE - tot)])
    return flat.reshape(lead + (rows, LANE))


def _unpack(buf, shapes, lead=()):
    nl = len(lead)
    flat = buf.reshape(lead + (-1,))
    out, off = [], 0
    for shp in shapes:
        size = 1
        for s in shp:
            size *= s
        out.append(lax.slice_in_dim(flat, off, off + size, axis=nl).reshape(lead + tuple(shp)))
        off += size
    return out


def _whole(g8, axis):
    t = jnp.moveaxis(g8, 0, axis)
    shp = t.shape
    return t.reshape(shp[:axis] + (shp[axis] * shp[axis + 1],) + shp[axis + 2:])


def _slots(full, axis):
    shp = full.shape
    t = full.reshape(shp[:axis] + (N_DEV, shp[axis] // N_DEV) + shp[axis + 1:])
    return jnp.moveaxis(t, axis, 0)


_PARAMS = (("meta_tokens", 1, False), ("norm_mix_pre", None, False), ("norm_mix_post", None, False), ("norm_ffn_pre", None, False),
           ("norm_ffn_post", None, False), ("attn_w_in", 2, True), ("attn_b_forget", None, False), ("attn_q_norm", None, False),
           ("attn_k_norm", None, False), ("attn_w_out", 1, True), ("dn_w_in", 2, True), ("dn_conv", 2, False), ("dn_a_log", None, False),
           ("dn_dt_bias", None, False), ("dn_o_norm", None, False), ("dn_w_out", 1, True), ("ffn_w_up", 2, True), ("ffn_conv", 2, False),
           ("ffn_w_down", 1, True))
_LOCAL_KEY = dict(meta_tokens="meta", norm_mix_pre="g_pre", norm_mix_post="g_post", norm_ffn_pre="g_fpre", norm_ffn_post="g_fpost",
                  attn_w_in="attn_w_in", attn_b_forget="attn_b", attn_q_norm="attn_qg", attn_k_norm="attn_kg", attn_w_out="attn_w_out",
                  dn_w_in="dn_w_in", dn_conv="dn_conv", dn_a_log="dn_alog", dn_dt_bias="dn_dtb", dn_o_norm="dn_og", dn_w_out="dn_w_out",
                  ffn_w_up="ffn_w_up", ffn_conv="ffn_conv", ffn_w_down="ffn_w_down")


def kernel(x, meta_tokens, norm_mix_pre, norm_mix_post, norm_ffn_pre, norm_ffn_post, attn_w_in, attn_b_forget, attn_q_norm, attn_k_norm, attn_w_out, dn_w_in, dn_conv, dn_a_log, dn_dt_bias, dn_o_norm, dn_w_out, ffn_w_up, ffn_conv, ffn_w_down, loss_target, m_meta_tokens, m_norm_mix_pre, m_norm_mix_post, m_norm_ffn_pre, m_norm_ffn_post, m_attn_w_in, m_attn_b_forget, m_attn_q_norm, m_attn_k_norm, m_attn_w_out, m_dn_w_in, m_dn_conv, m_dn_a_log, m_dn_dt_bias, m_dn_o_norm, m_dn_w_out, m_ffn_w_up, m_ffn_conv, m_ffn_w_down, v_meta_tokens, v_norm_mix_pre, v_norm_mix_post, v_norm_ffn_pre, v_norm_ffn_post, v_attn_w_in, v_attn_b_forget, v_attn_q_norm, v_attn_k_norm, v_attn_w_out, v_dn_w_in, v_dn_conv, v_dn_a_log, v_dn_dt_bias, v_dn_o_norm, v_dn_w_out, v_ffn_w_up, v_ffn_conv, v_ffn_w_down):
    given = dict(locals())
    names = [p[0] for p in _PARAMS]
    big = [p for p in _PARAMS if p[2]]
    cut = [p for p in _PARAMS if p[1] is not None and not p[2]]
    rep = [p for p in _PARAMS if p[1] is None]

    g_big, g_cut = exchange([_pack([given[n] for n, _, _ in big], BF16), _pack([given[n] for n, _, _ in cut], F32)],
                            ["gather", "gather"], name="gather_weights")
    w = {}
    for (n, axis, _), g8 in zip(big, _unpack(g_big, [given[n].shape for n, _, _ in big], lead=(N_DEV,))):
        w[_LOCAL_KEY[n]] = _whole(g8, axis)
    for (n, axis, _), g8 in zip(cut, _unpack(g_cut, [given[n].shape for n, _, _ in cut], lead=(N_DEV,))):
        w[_LOCAL_KEY[n]] = _whole(g8, axis)
    for n, _, _ in rep:
        w[_LOCAL_KEY[n]] = given[n]
    for key in ("attn_w_in", "dn_w_in"):
        cols = w[key].shape[-1]
        w[key] = jnp.pad(w[key], ((0, 0), (0, 0), (0, LANE - cols % LANE)))

    loss, grad_x, g = local_step(x[0], loss_target[0], w)

    s_big = _pack([_slots(g[_LOCAL_KEY[n]], axis) for n, axis, _ in big], BF16, lead=(N_DEV,))
    s_cut = _pack([_slots(g[_LOCAL_KEY[n]], axis) for n, axis, _ in cut], F32, lead=(N_DEV,))
    s_rep = _pack([g[_LOCAL_KEY[n]] for n, _, _ in rep] + [loss.reshape(1)], F32)
    r_big, r_cut, r_rep = exchange([s_big, s_cut, s_rep], ["scatter", "scatter", "gather"], name="reduce_grads")
    grads = {}
    for (n, _, _), gv in zip(big, _unpack(slot_sum(r_big, name="sum_big"), [given[n].shape for n, _, _ in big])):
        grads[n] = gv
    for (n, _, _), gv in zip(cut, _unpack(slot_sum(r_cut, name="sum_cut"), [given[n].shape for n, _, _ in cut])):
        grads[n] = gv
    rep_sum = _unpack(slot_sum(r_rep, name="sum_rep"), [given[n].shape for n, _, _ in rep] + [(1,)])
    for (n, _, _), gv in zip(rep, rep_sum):
        grads[n] = gv
    loss_all = rep_sum[-1].reshape(())

    deltas, new_m, new_v = {}, {}, {}
    for n in names:
        shp = given[n].shape
        two_d = (-1, shp[-1])
        d, mn, vn = adamw(given[n].reshape(two_d), grads[n].reshape(two_d), given["m_" + n].reshape(two_d), given["v_" + n].reshape(two_d),
                          name="adamw_" + n)
        deltas[n], new_m[n], new_v[n] = d.reshape(shp), mn.reshape(shp), vn.reshape(shp)
    return (loss_all, grad_x[None], *[grads[n] for n in names], *[deltas[n] for n in names], *[new_m[n] for n in names],
            *[new_v[n] for n in names])
```

```python
import functools

import jax
import jax.numpy as jnp
from jax import lax
from jax.experimental import pallas as pl
from jax.experimental.pallas import tpu as pltpu

F32 = jnp.float32
BF16 = jnp.bfloat16
LANE = 128
SUB = 8
N_DEV = 8
N_META = 16
ATT_HEADS, ATT_HEAD_DIM = 16, 64
DN_HEADS, DN_HEAD_DIM, DN_CHUNK, DN_CONV = 8, 128, 64, 4
FFN_CONV = 3
EPS = 1e-6
NEG = -1e30
ADAM_LR, ADAM_B1, ADAM_B2, ADAM_EPS, ADAM_WD, ADAM_STEP = 0.001, 0.9, 0.999, 1e-08, 0.01, 10
HI = lax.Precision.HIGHEST
VMEM_LIMIT = 56 * 1024 * 1024


def _tile(n, target, align=LANE):
    if n <= target:
        return n
    best = None
    for t in range(align, target + 1, align):
        if n % t == 0:
            best = t
    assert best is not None, (n, target, align)
    return best


def _iota(shape, dim):
    return lax.broadcasted_iota(jnp.int32, shape, dim)


def _colsum8(x):
    r, c = x.shape
    return x.reshape(r // SUB, SUB, c).sum(axis=0)


def _cparams(*sem):
    return pltpu.CompilerParams(dimension_semantics=sem, vmem_limit_bytes=VMEM_LIMIT)


def _sigmoid(x):
    return 1.0 / (1.0 + jnp.exp(-x))


def _softplus(x):
    return jnp.maximum(x, 0.0) + jnp.log(1.0 + jnp.exp(-jnp.abs(x)))


def _accum(ref, part, first):
    @pl.when(first)
    def _():
        ref[...] = part

    @pl.when(jnp.logical_not(first))
    def _():
        ref[...] += part


def matmul(a, b, *, trans_a=False, trans_b=False, out_dtype=F32, tm=640, tn=512, tk=1024, name="matmul"):
    if trans_a:
        kdim, m = a.shape
    else:
        m, kdim = a.shape
    if trans_b:
        n, kb = b.shape
    else:
        kb, n = b.shape
    assert kb == kdim, (a.shape, b.shape, trans_a, trans_b)
    tm, tn, tk = _tile(m, tm), _tile(n, tn), _tile(kdim, tk)
    nk = kdim // tk
    dims = (((0 if trans_a else 1,), (1 if trans_b else 0,)), ((), ()))
    cdt = BF16

    def body(a_ref, b_ref, o_ref, acc_ref):
        k = pl.program_id(2)
        part = lax.dot_general(a_ref[...].astype(cdt), b_ref[...].astype(cdt), dims, preferred_element_type=F32)
        _accum(acc_ref, part, k == 0)

        @pl.when(k == nk - 1)
        def _():
            o_ref[...] = acc_ref[...].astype(o_ref.dtype)

    a_spec = pl.BlockSpec((tk, tm), lambda i, j, k: (k, i)) if trans_a else pl.BlockSpec((tm, tk), lambda i, j, k: (i, k))
    b_spec = pl.BlockSpec((tn, tk), lambda i, j, k: (j, k)) if trans_b else pl.BlockSpec((tk, tn), lambda i, j, k: (k, j))
    return pl.pallas_call(
        body,
        name=name,
        grid=(m // tm, n // tn, nk),
        in_specs=[a_spec, b_spec],
        out_specs=pl.BlockSpec((tm, tn), lambda i, j, k: (i, j)),
        out_shape=jax.ShapeDtypeStruct((m, n), out_dtype),
        scratch_shapes=[pltpu.VMEM((tm, tn), F32)],
        compiler_params=_cparams("parallel", "parallel", "arbitrary"),
    )(a, b)


def rms_fwd(x, g, *, res=None, out_dtype, name):
    n, d = x.shape
    tr = _tile(n, 640, SUB)

    def body(*refs):
        x_ref, g_ref = refs[0], refs[1]
        o_ref = refs[-1]
        xv = x_ref[...]
        y = xv * lax.rsqrt(jnp.mean(xv * xv, axis=-1, keepdims=True) + EPS) * g_ref[...]
        if res is not None:
            y = y + refs[2][...]
        o_ref[...] = y.astype(o_ref.dtype)

    row = pl.BlockSpec((tr, d), lambda i: (i, 0))
    ins = [x, g] + ([res] if res is not None else [])
    return pl.pallas_call(
        body, name=name, grid=(n // tr,),
        in_specs=[row, pl.BlockSpec((1, d), lambda i: (0, 0))] + ([row] if res is not None else []),
        out_specs=row, out_shape=jax.ShapeDtypeStruct((n, d), out_dtype),
        compiler_params=_cparams("parallel"),
    )(*ins)


def rms_bwd(x, g, dy, *, res=None, pad, name):
    n, d = x.shape
    tr = _tile(n, 640, SUB)

    def body(*refs):
        x_ref, g_ref, dy_ref = refs[:3]
        dx_ref, dg_ref = refs[-2:]
        i = pl.program_id(0)
        xv = x_ref[...]
        r = lax.rsqrt(jnp.mean(xv * xv, axis=-1, keepdims=True) + EPS)
        xh = xv * r
        dyv = dy_ref[...].astype(F32)
        gdy = dyv * g_ref[...]
        dx = r * (gdy - xh * jnp.mean(xh * gdy, axis=-1, keepdims=True))
        if res is not None:
            dx = dx + refs[3][...]
        rows = i * tr + _iota((tr, 1), 0)
        dx_ref[...] = jnp.where(rows >= pad, dx, 0.0)
        _accum(dg_ref, _colsum8(dyv * xh), i == 0)

    row = pl.BlockSpec((tr, d), lambda i: (i, 0))
    ins = [x, g, dy] + ([res] if res is not None else [])
    return pl.pallas_call(
        body, name=name, grid=(n // tr,),
        in_specs=[row, pl.BlockSpec((1, d), lambda i: (0, 0)), row] + ([row] if res is not None else []),
        out_specs=[row, pl.BlockSpec((SUB, d), lambda i: (0, 0))],
        out_shape=[jax.ShapeDtypeStruct((n, d), F32), jax.ShapeDtypeStruct((SUB, d), F32)],
        compiler_params=_cparams("arbitrary"),
    )(*ins)


def _shift_down(cur, prev8, s):
    if s == 0:
        return cur
    out = pltpu.roll(cur, s, 0)
    row = _iota(cur.shape, 0)
    for r in range(s):
        out = jnp.where(row == r, prev8[SUB - s + r:SUB - s + r + 1, :], out)
    return out


def _shift_up(cur, next8, s):
    if s == 0:
        return cur
    tr = cur.shape[0]
    out = pltpu.roll(cur, tr - s, 0)
    row = _iota(cur.shape, 0)
    for r in range(s):
        out = jnp.where(row == tr - s + r, next8[r:r + 1, :], out)
    return out


def _conv_rows(cur, prev8, w):
    kw = w.shape[0]
    acc = w[kw - 1:kw, :] * cur
    for k in range(kw - 1):
        acc = acc + w[k:k + 1, :] * _shift_down(cur, prev8, kw - 1 - k)
    return acc


def conv_transpose(dy, w, *, out_dtype, name):
    n, c = dy.shape
    kw = w.shape[0]
    tr = _tile(n, 640, SUB)
    tc = _tile(c, 512)
    nb8 = n // SUB

    def body(dy_ref, nx_ref, w_ref, o_ref):
        i = pl.program_id(0)
        cur = dy_ref[...].astype(F32)
        nxt = jnp.where(i == pl.num_programs(0) - 1, 0.0, nx_ref[...].astype(F32))
        wv = w_ref[...]
        acc = wv[kw - 1:kw, :] * cur
        for k in range(kw - 1):
            acc = acc + wv[k:k + 1, :] * _shift_up(cur, nxt, kw - 1 - k)
        o_ref[...] = acc.astype(o_ref.dtype)

    return pl.pallas_call(
        body, name=name, grid=(n // tr, c // tc),
        in_specs=[pl.BlockSpec((tr, tc), lambda i, j: (i, j)),
                  pl.BlockSpec((SUB, tc), lambda i, j: (jnp.minimum((i + 1) * (tr // SUB), nb8 - 1), j)),
                  pl.BlockSpec((kw, tc), lambda i, j: (0, j))],
        out_specs=pl.BlockSpec((tr, tc), lambda i, j: (i, j)),
        out_shape=jax.ShapeDtypeStruct((n, c), out_dtype),
        compiler_params=_cparams("parallel", "parallel"),
    )(dy, dy, w)


_GELU_C = 0.7978845608028654
_GELU_A = 0.044715


def _gelu(x):
    return 0.5 * x * (1.0 + jnp.tanh(_GELU_C * (x + _GELU_A * x * x * x)))


def _gelu_grad(x):
    th = jnp.tanh(_GELU_C * (x + _GELU_A * x * x * x))
    return 0.5 * (1.0 + th) + 0.5 * x * (1.0 - th * th) * _GELU_C * (1.0 + 3.0 * _GELU_A * x * x)


def conv_glu_fwd(u, cw, *, name):
    n, f2 = u.shape
    f = f2 // 2
    tr = _tile(n, 640, SUB)
    tc = _tile(f, 512)
    nf = f // tc
    r8 = tr // SUB

    def body(ug_ref, uu_ref, pg_ref, pu_ref, wg_ref, wu_ref, o_ref):
        first = pl.program_id(0) == 0
        pg = jnp.where(first, 0.0, pg_ref[...])
        pu = jnp.where(first, 0.0, pu_ref[...])
        gate = _conv_rows(ug_ref[...], pg, wg_ref[...])
        up = _conv_rows(uu_ref[...], pu, wu_ref[...])
        o_ref[...] = (_gelu(gate) * up).astype(o_ref.dtype)

    prev = lambda off: pl.BlockSpec((SUB, tc), lambda i, j: (jnp.maximum(i * r8 - 1, 0), j + off))
    return pl.pallas_call(
        body, name=name, grid=(n // tr, nf),
        in_specs=[pl.BlockSpec((tr, tc), lambda i, j: (i, j)), pl.BlockSpec((tr, tc), lambda i, j: (i, j + nf)),
                  prev(0), prev(nf),
                  pl.BlockSpec((FFN_CONV, tc), lambda i, j: (0, j)), pl.BlockSpec((FFN_CONV, tc), lambda i, j: (0, j + nf))],
        out_specs=pl.BlockSpec((tr, tc), lambda i, j: (i, j)),
        out_shape=jax.ShapeDtypeStruct((n, f), BF16),
        compiler_params=_cparams("parallel", "parallel"),
    )(u, u, u, u, cw, cw)


def conv_glu_bwd(u, cw, dact, *, name):
    n, f2 = u.shape
    f = f2 // 2
    tr = _tile(n, 640, SUB)
    tc = _tile(f, 512)
    nf = f // tc
    r8 = tr // SUB

    def body(us_ref, up_ref, ps_ref, pp_ref, ws_ref, wp_ref, da_ref, o_ref, dw_ref):
        j, i = pl.program_id(0), pl.program_id(1)
        first = i == 0
        ps = jnp.where(first, 0.0, ps_ref[...])
        pp = jnp.where(first, 0.0, pp_ref[...])
        cur = us_ref[...]
        mine = _conv_rows(cur, ps, ws_ref[...])
        other = _conv_rows(up_ref[...], pp, wp_ref[...])
        da = da_ref[...].astype(F32)
        is_gate = j < nf
        d_mine = jnp.where(is_gate, da * other * _gelu_grad(mine), da * _gelu(other))
        o_ref[...] = d_mine
        for k in range(FFN_CONV):
            part = _colsum8(d_mine * _shift_down(cur, ps, FFN_CONV - 1 - k))

            @pl.when(first)
            def _():
                dw_ref[k] = part

            @pl.when(jnp.logical_not(first))
            def _():
                dw_ref[k] += part

    part_of = lambda j: (j + nf) % (2 * nf)
    return pl.pallas_call(
        body, name=name, grid=(2 * nf, n // tr),
        in_specs=[pl.BlockSpec((tr, tc), lambda j, i: (i, j)), pl.BlockSpec((tr, tc), lambda j, i: (i, part_of(j))),
                  pl.BlockSpec((SUB, tc), lambda j, i: (jnp.maximum(i * r8 - 1, 0), j)),
                  pl.BlockSpec((SUB, tc), lambda j, i: (jnp.maximum(i * r8 - 1, 0), part_of(j))),
                  pl.BlockSpec((FFN_CONV, tc), lambda j, i: (0, j)), pl.BlockSpec((FFN_CONV, tc), lambda j, i: (0, part_of(j))),
                  pl.BlockSpec((tr, tc), lambda j, i: (i, j % nf))],
        out_specs=[pl.BlockSpec((tr, tc), lambda j, i: (i, j)), pl.BlockSpec((FFN_CONV, SUB, tc), lambda j, i: (0, 0, j))],
        out_shape=[jax.ShapeDtypeStruct((n, f2), F32), jax.ShapeDtypeStruct((FFN_CONV, SUB, f2), F32)],
        compiler_params=_cparams("parallel", "arbitrary"),
    )(u, u, u, u, cw, cw, dact)


def cumsum_rows(x, *, reverse, name):
    n, c = x.shape
    tr = LANE
    nb = n // tr

    def body(x_ref, o_ref, carry_ref):
        i = pl.program_id(0)

        @pl.when(i == 0)
        def _():
            carry_ref[...] = jnp.zeros_like(carry_ref)

        r, cc = _iota((tr, tr), 0), _iota((tr, tr), 1)
        tri = jnp.where((cc >= r) if reverse else (cc <= r), 1.0, 0.0).astype(F32)
        out = jnp.dot(tri, x_ref[...], precision=HI, preferred_element_type=F32) + carry_ref[...]
        o_ref[...] = out
        carry_ref[...] = out[0:1, :] if reverse else out[tr - 1:tr, :]

    idx = (lambda i: (nb - 1 - i, 0)) if reverse else (lambda i: (i, 0))
    return pl.pallas_call(
        body, name=name, grid=(nb,),
        in_specs=[pl.BlockSpec((tr, c), idx)], out_specs=pl.BlockSpec((tr, c), idx),
        out_shape=jax.ShapeDtypeStruct((n, c), F32), scratch_shapes=[pltpu.VMEM((1, c), F32)],
        compiler_params=_cparams("arbitrary"),
    )(x)


def _group_sum64(x):
    r, c = x.shape
    a, b = _iota((LANE, LANE), 0), _iota((LANE, LANE), 1)
    bd = jnp.where((a // 64) == (b // 64), 1.0, 0.0).astype(F32)
    parts = [jnp.dot(x[:, k * LANE:(k + 1) * LANE], bd, precision=HI, preferred_element_type=F32) for k in range(c // LANE)]
    return parts[0] if len(parts) == 1 else jnp.concatenate(parts, axis=1)


def attn_prep(proj, qg, kg, bf, *, hd, name):
    n = proj.shape[0]
    tr = _tile(n, 640, SUB)
    scale = ATT_HEAD_DIM ** -0.5
    nh = hd // LANE

    def body(q_ref, k_ref, v_ref, f_ref, qg_ref, kg_ref, bf_ref, qo_ref, ko_ref, vo_ref, lf_ref):
        def norm(x, g):
            ms = _group_sum64(x * x) * (1.0 / ATT_HEAD_DIM)
            return x * lax.rsqrt(ms + EPS) * g

        qo_ref[...] = (norm(q_ref[...], qg_ref[...]) * scale).astype(qo_ref.dtype)
        ko_ref[...] = norm(k_ref[...], kg_ref[...]).astype(ko_ref.dtype)
        vo_ref[...] = v_ref[...].astype(vo_ref.dtype)
        lf_ref[...] = -_softplus(-(f_ref[...] + bf_ref[...]))

    col = lambda c: pl.BlockSpec((tr, hd), lambda i: (i, c))
    vec = lambda w: pl.BlockSpec((1, w), lambda i: (0, 0))
    return pl.pallas_call(
        body, name=name, grid=(n // tr,),
        in_specs=[col(0), col(1), col(2), pl.BlockSpec((tr, LANE), lambda i: (i, 4 * nh)), vec(hd), vec(hd), vec(LANE)],
        out_specs=[col(0), col(0), col(0), pl.BlockSpec((tr, LANE), lambda i: (i, 0))],
        out_shape=[jax.ShapeDtypeStruct((n, hd), BF16)] * 3 + [jax.ShapeDtypeStruct((n, LANE), F32)],
        compiler_params=_cparams("parallel"),
    )(proj, proj, proj, proj, qg, kg, bf)


def _half_mask(shape):
    return _iota(shape, 1) < ATT_HEAD_DIM


def flash_fwd(qs, kn, v, proj, ct, *, pad, t, name):
    n, hd = qs.shape
    npair = hd // LANE
    nb = n // t
    gate0 = 3 * npair

    def body(q_ref, k_ref, v_ref, g_ref, c_ref, o_ref, og_ref, lse_ref, m_ref, acc_ref):
        i, j = pl.program_id(1), pl.program_id(2)

        @pl.when(j == 0)
        def _():
            m_ref[...] = jnp.full_like(m_ref, NEG)
            acc_ref[...] = jnp.zeros_like(acc_ref)

        def step(masked):
            q, k, vv = q_ref[...], k_ref[...], v_ref[...]
            half0 = _half_mask(q.shape)
            if masked:
                rowpos = i * t + _iota((t, t), 0)
                colpos = j * t + _iota((t, t), 1)
                mask = (colpos <= rowpos) & (colpos >= pad)
            for hh in range(2):
                sel = half0 if hh == 0 else jnp.logical_not(half0)
                qm = jnp.where(sel, q, jnp.zeros_like(q))
                v1 = jnp.where(sel, vv, jnp.ones_like(vv))
                s = lax.dot_general(qm, k, (((1,), (1,)), ((), ())), preferred_element_type=F32) - c_ref[0, hh:hh + 1, :]
                if masked:
                    s = jnp.where(mask, s, NEG)
                m_prev = m_ref[hh]
                m_new = jnp.maximum(m_prev, jnp.max(s, axis=1, keepdims=True))
                p = jnp.exp(s - m_new[:, 0:1])
                if masked:
                    p = jnp.where(mask, p, 0.0)
                acc_ref[hh] = jnp.exp(m_prev - m_new) * acc_ref[hh] + jnp.dot(p.astype(vv.dtype), v1, preferred_element_type=F32)
                m_ref[hh] = m_new

        edge = (j == i) | (j == 0)

        @pl.when(edge & (j <= i))
        def _():
            step(True)

        @pl.when(jnp.logical_not(edge) & (j < i))
        def _():
            step(False)

        @pl.when(j == i)
        def _():
            half0 = _half_mask((t, LANE))
            a0, a1 = acc_ref[0], acc_ref[1]
            l = jnp.where(half0, a0[:, ATT_HEAD_DIM:ATT_HEAD_DIM + 1], a1[:, 0:1])
            acc = jnp.where(half0, a0, a1)
            m = jnp.where(half0, m_ref[0], m_ref[1])
            live = l > 0.0
            o = jnp.where(live, acc / jnp.where(live, l, 1.0), 0.0)
            o_ref[...] = o
            og_ref[...] = (o * _sigmoid(g_ref[...])).astype(og_ref.dtype)
            lse_ref[...] = jnp.where(live, m + jnp.log(jnp.where(live, l, 1.0)), 0.0)

    qspec = pl.BlockSpec((t, LANE), lambda p, i, j: (i, p))
    kspec = pl.BlockSpec((t, LANE), lambda p, i, j: (jnp.minimum(j, i), p))
    return pl.pallas_call(
        body, name=name, grid=(npair, nb, nb),
        in_specs=[qspec, kspec, kspec, pl.BlockSpec((t, LANE), lambda p, i, j: (i, gate0 + p)),
                  pl.BlockSpec((1, 2, t), lambda p, i, j: (p, 0, jnp.minimum(j, i)))],
        out_specs=[qspec, qspec, qspec],
        out_shape=[jax.ShapeDtypeStruct((n, hd), F32), jax.ShapeDtypeStruct((n, hd), BF16), jax.ShapeDtypeStruct((n, hd), F32)],
        scratch_shapes=[pltpu.VMEM((2, t, LANE), F32)] * 2,
        compiler_params=_cparams("parallel", "parallel", "arbitrary"),
    )(qs, kn, v, proj, ct)


def attn_bwd_prep(dgated, o, proj, *, hd, name):
    n = o.shape[0]
    tr = _tile(n, 640, SUB)
    gate0 = 3

    def body(dg_ref, o_ref, g_ref, do_ref, dl_ref, dgate_ref):
        dg, ov = dg_ref[...], o_ref[...]
        sg = _sigmoid(g_ref[...])
        do = dg * sg
        do_ref[...] = do.astype(do_ref.dtype)
        dl_ref[...] = _group_sum64(do * ov)
        dgate_ref[...] = dg * ov * sg * (1.0 - sg)

    row = pl.BlockSpec((tr, hd), lambda i: (i, 0))
    return pl.pallas_call(
        body, name=name, grid=(n // tr,),
        in_specs=[row, row, pl.BlockSpec((tr, hd), lambda i: (i, gate0))],
        out_specs=[row, row, row],
        out_shape=[jax.ShapeDtypeStruct((n, hd), BF16), jax.ShapeDtypeStruct((n, hd), F32), jax.ShapeDtypeStruct((n, hd), F32)],
        compiler_params=_cparams("parallel"),
    )(dgated, o, proj)


def flash_bwd(qs, kn, v, do, lse, delta, ct, *, pad, t, name):
    n, hd = qs.shape
    npair = hd // LANE
    nb = n // t

    def body(q_ref, k_ref, v_ref, do_ref, lse_ref, dl_ref, c_ref, dq_ref, dk_ref, dv_ref, dck_ref, dcr_ref, dk_acc, dv_acc, dck_acc):
        j, i = pl.program_id(1), pl.program_id(2)

        @pl.when((j == 0) & (i == 0))
        def _():
            dq_ref[...] = jnp.zeros_like(dq_ref)
            dcr_ref[...] = jnp.zeros_like(dcr_ref)

        @pl.when(i == j)
        def _():
            dk_acc[...] = jnp.zeros_like(dk_acc)
            dv_acc[...] = jnp.zeros_like(dv_acc)
            dck_acc[...] = jnp.zeros_like(dck_acc)

        def step(masked):
            q, k, vv, dov = q_ref[...], k_ref[...], v_ref[...], do_ref[...]
            half0 = _half_mask(q.shape)
            if masked:
                rowpos = i * t + _iota((t, t), 0)
                colpos = j * t + _iota((t, t), 1)
                mask = (colpos <= rowpos) & (colpos >= pad)
            nt = (((1,), (1,)), ((), ()))
            tn = (((0,), (0,)), ((), ()))
            dq_h, dk_h, dv_h = [], [], []
            for hh in range(2):
                sel = half0 if hh == 0 else jnp.logical_not(half0)
                qm = jnp.where(sel, q, jnp.zeros_like(q))
                dom = jnp.where(sel, dov, jnp.zeros_like(dov))
                q1 = jnp.where(sel, q, jnp.ones_like(q))
                k1 = jnp.where(sel, k, jnp.ones_like(k))
                x = lax.dot_general(qm, k, nt, preferred_element_type=F32) - c_ref[0, hh:hh + 1, :] - lse_ref[:, hh * 64:hh * 64 + 1]
                if masked:
                    p = jnp.where(mask, jnp.exp(jnp.where(mask, x, NEG)), 0.0)
                else:
                    p = jnp.exp(x)
                dp = lax.dot_general(dom, vv, nt, preferred_element_type=F32)
                ds = p * (dp - dl_ref[:, hh * 64:hh * 64 + 1])
                dsb, pb = ds.astype(k.dtype), p.astype(k.dtype)
                dq_h.append(jnp.dot(dsb, k1, preferred_element_type=F32))
                dk_h.append(lax.dot_general(dsb, q1, tn, preferred_element_type=F32))
                dv_h.append(lax.dot_general(pb, dov, tn, preferred_element_type=F32))
            rows = pl.ds(pl.multiple_of(i * t, t), t)
            dq_ref[rows, :] += jnp.where(half0, dq_h[0], dq_h[1])
            dcr_ref[rows, :] += jnp.where(half0, dq_h[1], dq_h[0])
            dk_acc[...] += jnp.where(half0, dk_h[0], dk_h[1])
            dck_acc[...] += jnp.where(half0, dk_h[1], dk_h[0])
            dv_acc[...] += jnp.where(half0, dv_h[0], dv_h[1])

        edge = (i == j) | (j == 0)

        @pl.when(edge & (i >= j))
        def _():
            step(True)

        @pl.when(jnp.logical_not(edge) & (i > j))
        def _():
            step(False)

        @pl.when(i == nb - 1)
        def _():
            dk_ref[...] = dk_acc[...]
            dv_ref[...] = dv_acc[...]
            dck_ref[...] = dck_acc[...]

    qspec = pl.BlockSpec((t, LANE), lambda p, j, i: (jnp.maximum(i, j), p))
    kspec = pl.BlockSpec((t, LANE), lambda p, j, i: (j, p))
    cspec = pl.BlockSpec((1, 2, t), lambda p, j, i: (p, 0, j))
    whole = pl.BlockSpec((n, LANE), lambda p, j, i: (0, p))
    return pl.pallas_call(
        body, name=name, grid=(npair, nb, nb),
        in_specs=[qspec, kspec, kspec, qspec, qspec, qspec, cspec],
        out_specs=[whole, kspec, kspec, kspec, whole],
        out_shape=[jax.ShapeDtypeStruct((n, hd), F32)] * 5,
        scratch_shapes=[pltpu.VMEM((t, LANE), F32)] * 3,
        compiler_params=_cparams("parallel", "arbitrary", "arbitrary"),
    )(qs, kn, v, do, lse, delta, ct)


def attn_in_bwd(dqs, dkn, proj, qg, kg, bf, dlogf, *, hd, pad, name):
    n = proj.shape[0]
    tr = _tile(n, 640, SUB)
    scale = ATT_HEAD_DIM ** -0.5
    nh = hd // LANE

    def body(dq_ref, dk_ref, q_ref, k_ref, f_ref, qg_ref, kg_ref, bf_ref, dl_ref, oq_ref, ok_ref, of_ref, gq_ref, gk_ref, gb_ref):
        i = pl.program_id(0)

        def back(x, g, dy):
            r = lax.rsqrt(_group_sum64(x * x) * (1.0 / ATT_HEAD_DIM) + EPS)
            xh = x * r
            gdy = dy * g
            dx = r * (gdy - xh * _group_sum64(xh * gdy) * (1.0 / ATT_HEAD_DIM))
            return dx, _colsum8(dy * xh)

        dxq, gq = back(q_ref[...], qg_ref[...], dq_ref[...] * scale)
        dxk, gk = back(k_ref[...], kg_ref[...], dk_ref[...])
        oq_ref[...] = dxq.astype(oq_ref.dtype)
        ok_ref[...] = dxk.astype(ok_ref.dtype)
        rows = i * tr + _iota((tr, 1), 0)
        dfl = jnp.where(rows >= pad, dl_ref[...] * _sigmoid(-(f_ref[...] + bf_ref[...])), 0.0)
        of_ref[...] = dfl.astype(of_ref.dtype)
        _accum(gq_ref, gq, i == 0)
        _accum(gk_ref, gk, i == 0)
        _accum(gb_ref, _colsum8(dfl), i == 0)

    row = pl.BlockSpec((tr, hd), lambda i: (i, 0))
    col = lambda c: pl.BlockSpec((tr, hd), lambda i: (i, c))
    nar = pl.BlockSpec((tr, LANE), lambda i: (i, 0))
    vec = lambda w: pl.BlockSpec((1, w), lambda i: (0, 0))
    acc = lambda w: pl.BlockSpec((SUB, w), lambda i: (0, 0))
    return pl.pallas_call(
        body, name=name, grid=(n // tr,),
        in_specs=[row, row, col(0), col(1), pl.BlockSpec((tr, LANE), lambda i: (i, 4 * nh)), vec(hd), vec(hd), vec(LANE), nar],
        out_specs=[row, row, nar, acc(hd), acc(hd), acc(LANE)],
        out_shape=[jax.ShapeDtypeStruct((n, hd), BF16)] * 2 + [jax.ShapeDtypeStruct((n, LANE), BF16),
                   jax.ShapeDtypeStruct((SUB, hd), F32), jax.ShapeDtypeStruct((SUB, hd), F32), jax.ShapeDtypeStruct((SUB, LANE), F32)],
        compiler_params=_cparams("arbitrary"),
    )(dqs, dkn, proj, proj, proj, qg, kg, bf, dlogf)


def _silu(x):
    return x * _sigmoid(x)


def _silu_grad(x):
    s = _sigmoid(x)
    return s * (1.0 + x * (1.0 - s))


def gdn_prep(proj, cw, *, hd, name):
    n = proj.shape[0]
    tr = _tile(n, 640, SUB)
    nh = hd // LANE
    r8 = tr // SUB
    qscale = DN_HEAD_DIM ** -0.5

    def body(x_ref, p_ref, w_ref, o_ref):
        i, c = pl.program_id(0), pl.program_id(1)
        prev = jnp.where(i == 0, 0.0, p_ref[...])
        s = _silu(_conv_rows(x_ref[...], prev, w_ref[...]))
        r = lax.rsqrt(jnp.sum(s * s, axis=-1, keepdims=True) + EPS)
        mult = jnp.where(c < nh, r * qscale, jnp.where(c < 2 * nh, r, 1.0))
        o_ref[...] = s * mult

    return pl.pallas_call(
        body, name=name, grid=(n // tr, 3 * nh),
        in_specs=[pl.BlockSpec((tr, LANE), lambda i, c: (i, c)),
                  pl.BlockSpec((SUB, LANE), lambda i, c: (jnp.maximum(i * r8 - 1, 0), c)),
                  pl.BlockSpec((DN_CONV, LANE), lambda i, c: (0, c))],
        out_specs=pl.BlockSpec((tr, LANE), lambda i, c: (i, c)),
        out_shape=jax.ShapeDtypeStruct((n, 3 * hd), F32),
        compiler_params=_cparams("parallel", "parallel"),
    )(proj, proj, cw)


def _chunk_tri(reverse):
    r, c = _iota((LANE, LANE), 0), _iota((LANE, LANE), 1)
    same = (r // DN_CHUNK) == (c // DN_CHUNK)
    return jnp.where(same & ((c >= r) if reverse else (c <= r)), 1.0, 0.0).astype(F32)


def gdn_gates(proj, alog, dtb, *, hd, name):
    n = proj.shape[0]
    gcol = 4 * (hd // LANE)

    def body(x_ref, a_ref, d_ref, o_ref):
        x = x_ref[...]
        lane = _iota(x.shape, 1)
        g = -jnp.exp(a_ref[...]) * _softplus(x + d_ref[...])
        gc = jnp.dot(_chunk_tri(False), jnp.where((lane >= DN_HEADS) & (lane < 2 * DN_HEADS), g, 0.0), precision=HI,
                     preferred_element_type=F32)
        o_ref[...] = jnp.where(lane < DN_HEADS, _sigmoid(x), gc)

    vec = pl.BlockSpec((1, LANE), lambda i: (0, 0))
    return pl.pallas_call(
        body, name=name, grid=(n // LANE,),
        in_specs=[pl.BlockSpec((LANE, LANE), lambda i: (i, gcol)), vec, vec],
        out_specs=pl.BlockSpec((LANE, LANE), lambda i: (i, 0)),
        out_shape=jax.ShapeDtypeStruct((n, LANE), F32),
        compiler_params=_cparams("parallel"),
    )(proj, alog, dtb)


def _mm(a, b, ca=1, cb=0):
    return lax.dot_general(a.astype(BF16), b.astype(BF16), (((ca,), (cb,)), ((), ())), preferred_element_type=F32)


def _mmh(a, b):
    return jnp.dot(a, b, precision=HI, preferred_element_type=F32)


def _gdn_common(q, k, v, beta, gc_c, gc_r):
    r, c = _iota((LANE, LANE), 0), _iota((LANE, LANE), 1)
    same = (r // DN_CHUNK) == (c // DN_CHUNK)
    incl, strict = same & (r >= c), same & (r > c)
    d = jnp.exp(jnp.where(incl, gc_c - gc_r, NEG))
    kk = _mm(k, k, 1, 1)
    ahat = jnp.where(strict, kk * d, 0.0)
    a = ahat * beta
    eye = jnp.where(r == c, 1.0, 0.0).astype(F32)
    t = eye - a
    pw = _mmh(a, a)
    for step in range(5):
        t = t + _mmh(t, pw)
        if step < 4:
            pw = _mmh(pw, pw)
    row = _iota((LANE, 1), 0)
    gl0 = jnp.sum(jnp.where(row == DN_CHUNK - 1, gc_c, 0.0), axis=0, keepdims=True)
    gl1 = jnp.sum(jnp.where(row == LANE - 1, gc_c, 0.0), axis=0, keepdims=True)
    gam = jnp.exp(gc_c)
    lam = jnp.exp(jnp.where(row < DN_CHUNK, gl0, gl1) - gc_c)
    kb, vb = k * (beta * gam), v * beta
    qk = _mm(q, k, 1, 1)
    return dict(incl=incl, strict=strict, d=d, kk=kk, ahat=ahat, t=t, gam=gam, lam=lam, kb=kb, vb=vb, w=_mm(t, kb), u0=_mm(t, vb),
                qk=qk, pm=jnp.where(incl, qk * d, 0.0), qg=q * gam, kl=k * lam, g0=jnp.exp(gl0), g1=jnp.exp(gl1))


def _gdn_states(cm, s0):
    c = DN_CHUNK
    u_a = cm["u0"][:c] - _mm(cm["w"][:c], s0, 1, 1)
    s1 = cm["g0"] * s0 + _mm(u_a, cm["kl"][:c], 0, 0)
    u_b = cm["u0"][c:] - _mm(cm["w"][c:], s1, 1, 1)
    s2 = cm["g1"] * s1 + _mm(u_b, cm["kl"][c:], 0, 0)
    return u_a, s1, u_b, s2


def gdn_chunk_fwd(qkv, bg, bgt, proj, ogain, *, hd, name):
    n = qkv.shape[0]
    nb = n // LANE
    nh = hd // LANE
    c = DN_CHUNK

    def body(q_ref, k_ref, v_ref, bg_ref, bgt_ref, g_ref, gain_ref, o_ref, og_ref, hist_ref, s_ref):
        @pl.when(pl.program_id(0) == 0)
        def _():
            s_ref[...] = jnp.zeros_like(s_ref)

        hist_ref[0] = s_ref[...]
        for h in range(nh):
            cols = slice(h * LANE, (h + 1) * LANE)
            cm = _gdn_common(q_ref[:, cols], k_ref[:, cols], v_ref[:, cols], bg_ref[:, h:h + 1],
                             bg_ref[:, nh + h:nh + h + 1], bgt_ref[nh + h:nh + h + 1, :])
            s0 = s_ref[h]
            u_a, s1, u_b, s2 = _gdn_states(cm, s0)
            u_all = jnp.concatenate([u_a, u_b], axis=0)
            o = jnp.concatenate([_mm(cm["qg"][:c], s0, 1, 1), _mm(cm["qg"][c:], s1, 1, 1)], axis=0) + _mm(cm["pm"], u_all)
            s_ref[h] = s2
            o_ref[:, cols] = o
            rn = lax.rsqrt(jnp.mean(o * o, axis=-1, keepdims=True) + EPS)
            og_ref[:, cols] = (o * rn * gain_ref[...] * _silu(g_ref[:, cols])).astype(og_ref.dtype)

    col = lambda cc: pl.BlockSpec((LANE, hd), lambda b: (b, cc))
    return pl.pallas_call(
        body, name=name, grid=(nb,),
        in_specs=[col(0), col(1), col(2), pl.BlockSpec((LANE, LANE), lambda b: (b, 0)),
                  pl.BlockSpec((2 * nh, LANE), lambda b: (0, b)), pl.BlockSpec((LANE, hd), lambda b: (b, 3)),
                  pl.BlockSpec((1, LANE), lambda b: (0, 0))],
        out_specs=[col(0), col(0), pl.BlockSpec((1, nh, LANE, LANE), lambda b: (b, 0, 0, 0))],
        out_shape=[jax.ShapeDtypeStruct((n, hd), F32), jax.ShapeDtypeStruct((n, hd), BF16),
                   jax.ShapeDtypeStruct((nb, nh, LANE, LANE), F32)],
        scratch_shapes=[pltpu.VMEM((nh, LANE, LANE), F32)],
        compiler_params=_cparams("arbitrary"),
    )(qkv, qkv, qkv, bg, bgt, proj, ogain)


def gdn_chunk_bwd(qkv, bg, bgt, proj, ogain, o_raw, dog, hist, *, hd, name):
    n = qkv.shape[0]
    nb = n // LANE
    nh = hd // LANE
    c = DN_CHUNK

    def body(q_ref, k_ref, v_ref, bg_ref, bgt_ref, g_ref, gain_ref, o_ref, dog_ref, hist_ref,
             dq_ref, dk_ref, dv_ref, dgate_ref, dbg_ref, dgt_ref, dgain_ref, ds_ref):
        first = pl.program_id(0) == 0

        @pl.when(first)
        def _():
            ds_ref[...] = jnp.zeros_like(ds_ref)

        lane = _iota((LANE, LANE), 1)
        row = _iota((LANE, 1), 0)
        dbg = jnp.zeros((LANE, LANE), F32)
        dgain = jnp.zeros((SUB, LANE), F32)
        for h in range(nh):
            cols = slice(h * LANE, (h + 1) * LANE)
            q, k, v = q_ref[:, cols], k_ref[:, cols], v_ref[:, cols]
            beta = bg_ref[:, h:h + 1]
            cm = _gdn_common(q, k, v, beta, bg_ref[:, nh + h:nh + h + 1], bgt_ref[nh + h:nh + h + 1, :])
            s0 = hist_ref[0, h]
            u_a, s1, u_b, _ = _gdn_states(cm, s0)
            u_all = jnp.concatenate([u_a, u_b], axis=0)
            o, gate, d_out, gain = o_ref[:, cols], g_ref[:, cols], dog_ref[:, cols], gain_ref[...]
            rn = lax.rsqrt(jnp.mean(o * o, axis=-1, keepdims=True) + EPS)
            xh = o * rn
            d_on = d_out * _silu(gate)
            dgate_ref[:, cols] = d_out * xh * gain * _silu_grad(gate)
            dgain = dgain + _colsum8(d_on * xh)
            gdy = d_on * gain
            d_o = rn * (gdy - xh * jnp.mean(xh * gdy, axis=-1, keepdims=True))
            pt_do = _mm(cm["pm"], d_o, 0, 0)
            ds_in = ds_ref[h]
            du_b = _mm(cm["kl"][c:], ds_in, 1, 1) + pt_do[c:]
            dkl_b = _mm(u_b, ds_in)
            dqg_b = _mm(d_o[c:], s1)
            dg1 = jnp.sum(jnp.sum(ds_in * s1, axis=1, keepdims=True), axis=0, keepdims=True)
            dw_b = -_mm(du_b, s1)
            ds_mid = cm["g1"] * ds_in + _mm(d_o[c:], cm["qg"][c:], 0, 0) - _mm(du_b, cm["w"][c:], 0, 0)
            du_a = _mm(cm["kl"][:c], ds_mid, 1, 1) + pt_do[:c]
            dkl_a = _mm(u_a, ds_mid)
            dqg_a = _mm(d_o[:c], s0)
            dg0 = jnp.sum(jnp.sum(ds_mid * s0, axis=1, keepdims=True), axis=0, keepdims=True)
            dw_a = -_mm(du_a, s0)
            ds_ref[h] = cm["g0"] * ds_mid + _mm(d_o[:c], cm["qg"][:c], 0, 0) - _mm(du_a, cm["w"][:c], 0, 0)
            du = jnp.concatenate([du_a, du_b], axis=0)
            dkl = jnp.concatenate([dkl_a, dkl_b], axis=0)
            dqg = jnp.concatenate([dqg_a, dqg_b], axis=0)
            dw = jnp.concatenate([dw_a, dw_b], axis=0)
            t, d, gam, lam = cm["t"], cm["d"], cm["gam"], cm["lam"]
            dp = jnp.where(cm["incl"], _mm(d_o, u_all, 1, 1), 0.0)
            dt = _mm(dw, cm["kb"], 1, 1) + _mm(du, cm["vb"], 1, 1)
            dkb = _mm(t, dw, 0, 0)
            dvb = _mm(t, du, 0, 0)
            da = jnp.where(cm["strict"], -_mm(_mm(t, dt, 0, 0), t, 1, 1), 0.0)
            kb_k = jnp.sum(dkb * k, axis=1, keepdims=True)
            dbeta = jnp.sum(da * cm["ahat"], axis=1, keepdims=True) + gam * kb_k + jnp.sum(dvb * v, axis=1, keepdims=True)
            dahat = da * beta
            dkk = dahat * d
            dqk = dp * d
            e = (dahat * cm["kk"] + dp * cm["qk"]) * d
            dk_ref[:, cols] = (_mm(dkk, k) + _mm(dkk, k, 0, 0) + _mm(dqk, q, 0, 0) + dkb * (beta * gam) + dkl * lam)
            dq_ref[:, cols] = _mm(dqk, k) + dqg * gam
            dv_ref[:, cols] = dvb * beta
            dgam = beta * kb_k + jnp.sum(dqg * q, axis=1, keepdims=True)
            dlam_lam = jnp.sum(dkl * k, axis=1, keepdims=True) * lam
            dgl0 = jnp.sum(jnp.where(row < c, dlam_lam, 0.0), axis=0, keepdims=True) + dg0 * cm["g0"]
            dgl1 = jnp.sum(jnp.where(row >= c, dlam_lam, 0.0), axis=0, keepdims=True) + dg1 * cm["g1"]
            dgc = (jnp.sum(e, axis=1, keepdims=True) + dgam * gam - dlam_lam
                   + jnp.where(row == c - 1, dgl0, 0.0) + jnp.where(row == LANE - 1, dgl1, 0.0))
            dgt_ref[h:h + 1, :] = -jnp.sum(e, axis=0, keepdims=True)
            dbg = dbg + jnp.where(lane == h, dbeta, 0.0) + jnp.where(lane == nh + h, dgc, 0.0)
        dbg_ref[...] = dbg
        _accum(dgain_ref, dgain, first)

    rev = lambda b: nb - 1 - b
    col = lambda cc: pl.BlockSpec((LANE, hd), lambda b: (rev(b), cc))
    return pl.pallas_call(
        body, name=name, grid=(nb,),
        in_specs=[col(0), col(1), col(2), pl.BlockSpec((LANE, LANE), lambda b: (rev(b), 0)),
                  pl.BlockSpec((2 * nh, LANE), lambda b: (0, rev(b))), pl.BlockSpec((LANE, hd), lambda b: (rev(b), 3)),
                  pl.BlockSpec((1, LANE), lambda b: (0, 0)), col(0), col(0),
                  pl.BlockSpec((1, nh, LANE, LANE), lambda b: (rev(b), 0, 0, 0))],
        out_specs=[col(0), col(0), col(0), col(0), pl.BlockSpec((LANE, LANE), lambda b: (rev(b), 0)),
                   pl.BlockSpec((nh, LANE), lambda b: (0, rev(b))), pl.BlockSpec((SUB, LANE), lambda b: (0, 0))],
        out_shape=[jax.ShapeDtypeStruct((n, hd), F32)] * 4 + [jax.ShapeDtypeStruct((n, LANE), F32),
                   jax.ShapeDtypeStruct((nh, n), F32), jax.ShapeDtypeStruct((SUB, LANE), F32)],
        scratch_shapes=[pltpu.VMEM((nh, LANE, LANE), F32)],
        compiler_params=_cparams("arbitrary"),
    )(qkv, qkv, qkv, bg, bgt, proj, ogain, o_raw, dog, hist)


def gdn_gates_bwd(proj, alog, dtb, dbg, *, hd, pad, name):
    n = proj.shape[0]
    gcol = 4 * (hd // LANE)

    def body(x_ref, a_ref, d_ref, dbg_ref, o_ref, da_ref, dd_ref):
        i = pl.program_id(0)
        x = x_ref[...]
        lane = _iota(x.shape, 1)
        rows = i * LANE + _iota((LANE, 1), 0)
        isg = (lane >= DN_HEADS) & (lane < 2 * DN_HEADS)
        dbgv = jnp.where(rows >= pad, dbg_ref[...], 0.0)
        dg = jnp.dot(_chunk_tri(True), jnp.where(isg, dbgv, 0.0), precision=HI, preferred_element_type=F32)
        ea = jnp.exp(a_ref[...])
        z = x + d_ref[...]
        dg = jnp.where(rows >= pad, dg, 0.0)
        dz = jnp.where(isg, dg * (-ea) * _sigmoid(z), 0.0)
        sb = _sigmoid(x)
        o_ref[...] = jnp.where(lane < DN_HEADS, dbgv * sb * (1.0 - sb), dz).astype(o_ref.dtype)
        _accum(da_ref, _colsum8(jnp.where(isg, dg * (-ea) * _softplus(z), 0.0)), i == 0)
        _accum(dd_ref, _colsum8(dz), i == 0)

    vec = pl.BlockSpec((1, LANE), lambda i: (0, 0))
    blk = pl.BlockSpec((LANE, LANE), lambda i: (i, 0))
    acc = pl.BlockSpec((SUB, LANE), lambda i: (0, 0))
    return pl.pallas_call(
        body, name=name, grid=(n // LANE,),
        in_specs=[pl.BlockSpec((LANE, LANE), lambda i: (i, gcol)), vec, vec, blk],
        out_specs=[blk, acc, acc],
        out_shape=[jax.ShapeDtypeStruct((n, LANE), BF16), jax.ShapeDtypeStruct((SUB, LANE), F32), jax.ShapeDtypeStruct((SUB, LANE), F32)],
        compiler_params=_cparams("arbitrary"),
    )(proj, alog, dtb, dbg)


def gdn_prep_bwd(proj, cw, dqkv, *, hd, name):
    n = proj.shape[0]
    tr = _tile(n, 640, SUB)
    nh = hd // LANE
    r8 = tr // SUB
    qscale = DN_HEAD_DIM ** -0.5

    def body(x_ref, p_ref, w_ref, dq_ref, dk_ref, dv_ref, o_ref, dw_ref):
        c, i = pl.program_id(0), pl.program_id(1)
        first = i == 0
        prev = jnp.where(first, 0.0, p_ref[...])
        cur = x_ref[...]
        cv = _conv_rows(cur, prev, w_ref[...])
        s = _silu(cv)
        r = lax.rsqrt(jnp.sum(s * s, axis=-1, keepdims=True) + EPS)
        y = s * r
        dy = jnp.where(c < nh, dq_ref[...] * qscale, dk_ref[...])
        ds_norm = r * (dy - y * jnp.sum(dy * y, axis=-1, keepdims=True))
        dcv = jnp.where(c < 2 * nh, ds_norm, dv_ref[...]) * _silu_grad(cv)
        o_ref[...] = dcv
        for k in range(DN_CONV):
            part = _colsum8(dcv * _shift_down(cur, prev, DN_CONV - 1 - k))

            @pl.when(first)
            def _():
                dw_ref[k] = part

            @pl.when(jnp.logical_not(first))
            def _():
                dw_ref[k] += part

    blk = lambda f: pl.BlockSpec((tr, LANE), f)
    return pl.pallas_call(
        body, name=name, grid=(3 * nh, n // tr),
        in_specs=[blk(lambda c, i: (i, c)), pl.BlockSpec((SUB, LANE), lambda c, i: (jnp.maximum(i * r8 - 1, 0), c)),
                  pl.BlockSpec((DN_CONV, LANE), lambda c, i: (0, c)),
                  blk(lambda c, i: (i, jnp.minimum(c, nh - 1))), blk(lambda c, i: (i, jnp.clip(c - nh, 0, nh - 1))),
                  blk(lambda c, i: (i, jnp.clip(c - 2 * nh, 0, nh - 1)))],
        out_specs=[blk(lambda c, i: (i, c)), pl.BlockSpec((DN_CONV, SUB, LANE), lambda c, i: (0, 0, c))],
        out_shape=[jax.ShapeDtypeStruct((n, 3 * hd), F32), jax.ShapeDtypeStruct((DN_CONV, SUB, 3 * hd), F32)],
        compiler_params=_cparams("parallel", "arbitrary"),
    )(proj, proj, cw, *dqkv)


def loss_head(h, target, *, x0, name):
    n, d = h.shape
    tr = LANE
    nb0 = x0 // tr

    def body(h_ref, t_ref, dh_ref, sq_ref):
        i = pl.program_id(0)
        live = i >= nb0
        err = jnp.where(live, h_ref[...] - t_ref[...], 0.0)
        dh_ref[...] = err * (1.0 / d)
        _accum(sq_ref, _colsum8(err * err), i == 0)

    row = pl.BlockSpec((tr, d), lambda i: (i, 0))
    return pl.pallas_call(
        body, name=name, grid=(n // tr,),
        in_specs=[row, pl.BlockSpec((tr, d), lambda i: (jnp.maximum(i - nb0, 0), 0))],
        out_specs=[row, pl.BlockSpec((SUB, d), lambda i: (0, 0))],
        out_shape=[jax.ShapeDtypeStruct((n, d), F32), jax.ShapeDtypeStruct((SUB, d), F32)],
        compiler_params=_cparams("arbitrary"),
    )(h, target)


def adamw(w, g, m, v, *, name):
    r, c = w.shape
    tr = _tile(r, 512, SUB) if r % SUB == 0 else r
    c1 = 1.0 / (1.0 - ADAM_B1 ** ADAM_STEP)
    c2 = 1.0 / (1.0 - ADAM_B2 ** ADAM_STEP)

    def body(w_ref, g_ref, m_ref, v_ref, d_ref, mo_ref, vo_ref):
        gv = g_ref[...]
        mn = ADAM_B1 * m_ref[...] + (1.0 - ADAM_B1) * gv
        vn = ADAM_B2 * v_ref[...] + (1.0 - ADAM_B2) * (gv * gv)
        d_ref[...] = -ADAM_LR * ((mn * c1) / (jnp.sqrt(vn * c2) + ADAM_EPS) + ADAM_WD * w_ref[...])
        mo_ref[...] = mn
        vo_ref[...] = vn

    blk = pl.BlockSpec((tr, c), lambda i: (i, 0))
    return pl.pallas_call(
        body, name=name, grid=(r // tr,), in_specs=[blk] * 4, out_specs=[blk] * 3,
        out_shape=[jax.ShapeDtypeStruct((r, c), F32)] * 3, compiler_params=_cparams("parallel"),
    )(w, g, m, v)


_BIG = ("attn_w_in", "attn_w_out", "dn_w_in", "dn_w_out", "ffn_w_up", "ffn_w_down")


def _row(v, width=None):
    v = v.astype(F32).reshape(1, -1)
    if width is not None and v.shape[1] < width:
        v = jnp.pad(v, ((0, 0), (0, width - v.shape[1])))
    return v


def _fold8(p):
    return jnp.sum(p, axis=-2)


def local_step(x, target, w):
    seq, d = x.shape
    pad = (-(N_META + seq)) % LANE
    x0 = pad + N_META
    n = x0 + seq
    depth = w["g_pre"].shape[0]
    hd_a = ATT_HEADS * ATT_HEAD_DIM
    hd_d = DN_HEADS * DN_HEAD_DIM
    t_att = _tile(n, 640)
    h = jnp.concatenate([jnp.zeros((pad, d), F32), w["meta"].astype(F32), x], axis=0)
    saved = []
    for i in range(depth):
        j = i // 2
        s = dict(h=h)
        s["a"] = rms_fwd(h, _row(w["g_pre"][i]), out_dtype=BF16, name="rms_pre")
        if i % 2 == 0:
            s["proj"] = proj = matmul(s["a"], w["attn_w_in"][j], name="mm_attn_in")
            qg, kg = _row(jnp.tile(w["attn_qg"][j], ATT_HEADS)), _row(jnp.tile(w["attn_kg"][j], ATT_HEADS))
            bf = _row(w["attn_b"][j], LANE)
            s["qs"], s["kn"], s["v"], logf = attn_prep(proj, qg, kg, bf, hd=hd_a, name="attn_prep")
            c = cumsum_rows(logf, reverse=False, name="cumsum_fwd")
            s["ct"] = c[:, :ATT_HEADS].T.reshape(ATT_HEADS // 2, 2, n)
            s["o"], s["og"], s["lse"] = flash_fwd(s["qs"], s["kn"], s["v"], proj, s["ct"], pad=pad, t=t_att, name="flash_fwd")
            s["m"] = matmul(s["og"], w["attn_w_out"][j], name="mm_attn_out")
        else:
            s["proj"] = proj = matmul(s["a"], w["dn_w_in"][j], name="mm_dn_in")
            s["qkv"] = gdn_prep(proj, w["dn_conv"][j], hd=hd_d, name="gdn_prep")
            alog = jnp.pad(_row(w["dn_alog"][j]), ((0, 0), (DN_HEADS, LANE - 2 * DN_HEADS)))
            dtb = jnp.pad(_row(w["dn_dtb"][j]), ((0, 0), (DN_HEADS, LANE - 2 * DN_HEADS)))
            s["bg"] = gdn_gates(proj, alog, dtb, hd=hd_d, name="gdn_gates")
            s["bgt"] = s["bg"][:, :2 * DN_HEADS].T
            s["o"], s["og"], s["hist"] = gdn_chunk_fwd(s["qkv"], s["bg"], s["bgt"], proj, _row(w["dn_og"][j]), hd=hd_d, name="gdn_fwd")
            s["m"] = matmul(s["og"], w["dn_w_out"][j], name="mm_dn_out")
        s["h_mid"] = rms_fwd(s["m"], _row(w["g_post"][i]), res=h, out_dtype=F32, name="rms_post")
        s["b"] = rms_fwd(s["h_mid"], _row(w["g_fpre"][i]), out_dtype=BF16, name="rms_fpre")
        s["u"] = matmul(s["b"], w["ffn_w_up"][i], name="mm_ffn_up")
        s["act"] = conv_glu_fwd(s["u"], w["ffn_conv"][i], name="ffn_glu")
        s["f"] = matmul(s["act"], w["ffn_w_down"][i], tk=1408, name="mm_ffn_down")
        h = rms_fwd(s["f"], _row(w["g_fpost"][i]), res=s["h_mid"], out_dtype=F32, name="rms_fpost")
        saved.append(s)

    dh, sq = loss_head(h, target, x0=x0, name="loss_head")
    loss = 0.5 * jnp.sum(sq) / d

    g = {k: [None] * depth for k in ("g_pre", "g_post", "g_fpre", "g_fpost", "ffn_w_up", "ffn_conv", "ffn_w_down")}
    for k in ("attn_w_in", "attn_b", "attn_qg", "attn_kg", "attn_w_out", "dn_w_in", "dn_conv", "dn_alog", "dn_dtb", "dn_og", "dn_w_out"):
        g[k] = [None] * (depth // 2)
    for i in reversed(range(depth)):
        j = i // 2
        s = saved[i]
        proj = s["proj"]
        df, p8 = rms_bwd(s["f"], _row(w["g_fpost"][i]), dh, pad=pad, name="rms_fpost_bwd")
        g["g_fpost"][i] = _fold8(p8)
        dact = matmul(df, w["ffn_w_down"][i], trans_b=True, out_dtype=BF16, name="mm_ffn_down_dx")
        g["ffn_w_down"][i] = matmul(s["act"], df, trans_a=True, out_dtype=BF16, name="mm_ffn_down_dw")
        duc, p8 = conv_glu_bwd(s["u"], w["ffn_conv"][i], dact, name="ffn_glu_bwd")
        g["ffn_conv"][i] = _fold8(p8)
        du = conv_transpose(duc, w["ffn_conv"][i], out_dtype=BF16, name="ffn_conv_t")
        db = matmul(du, w["ffn_w_up"][i], trans_b=True, name="mm_ffn_up_dx")
        g["ffn_w_up"][i] = matmul(s["b"], du, trans_a=True, out_dtype=BF16, name="mm_ffn_up_dw")
        dh_mid, p8 = rms_bwd(s["h_mid"], _row(w["g_fpre"][i]), db, res=dh, pad=pad, name="rms_fpre_bwd")
        g["g_fpre"][i] = _fold8(p8)
        dm, p8 = rms_bwd(s["m"], _row(w["g_post"][i]), dh_mid, pad=pad, name="rms_post_bwd")
        g["g_post"][i] = _fold8(p8)
        if i % 2 == 0:
            g["attn_w_out"][j] = matmul(s["og"], dm, trans_a=True, out_dtype=BF16, name="mm_attn_out_dw")
            dgated = matmul(dm, w["attn_w_out"][j], trans_b=True, name="mm_attn_out_dx")
            do, delta, dgate = attn_bwd_prep(dgated, s["o"], proj, hd=hd_a, name="attn_bwd_prep")
            dqs, dkn, dv, dck, dcr = flash_bwd(s["qs"], s["kn"], s["v"], do, s["lse"], delta, s["ct"], pad=pad, t=t_att, name="flash_bwd")
            dc = (dcr - dck)[:, ::ATT_HEAD_DIM].reshape(n, ATT_HEADS // 2, 2)[:, :, ::-1].reshape(n, ATT_HEADS)
            dc = jnp.pad(dc, ((0, 0), (0, LANE - ATT_HEADS)))
            dlogf = cumsum_rows(dc, reverse=True, name="cumsum_bwd")
            qg, kg = _row(jnp.tile(w["attn_qg"][j], ATT_HEADS)), _row(jnp.tile(w["attn_kg"][j], ATT_HEADS))
            bf = _row(w["attn_b"][j], LANE)
            dq_raw, dk_raw, dfl, gq8, gk8, gb8 = attn_in_bwd(dqs, dkn, proj, qg, kg, bf, dlogf, hd=hd_a, pad=pad, name="attn_in_bwd")
            g["attn_qg"][j] = _fold8(gq8).reshape(ATT_HEADS, ATT_HEAD_DIM).sum(axis=0)
            g["attn_kg"][j] = _fold8(gk8).reshape(ATT_HEADS, ATT_HEAD_DIM).sum(axis=0)
            g["attn_b"][j] = _fold8(gb8)[:ATT_HEADS]
            dproj = jnp.concatenate([dq_raw, dk_raw, dv.astype(BF16), dgate.astype(BF16), dfl], axis=1)
            w_in, key = w["attn_w_in"][j], "attn_w_in"
        else:
            g["dn_w_out"][j] = matmul(s["og"], dm, trans_a=True, out_dtype=BF16, name="mm_dn_out_dw")
            dgated = matmul(dm, w["dn_w_out"][j], trans_b=True, name="mm_dn_out_dx")
            alog = jnp.pad(_row(w["dn_alog"][j]), ((0, 0), (DN_HEADS, LANE - 2 * DN_HEADS)))
            dtb = jnp.pad(_row(w["dn_dtb"][j]), ((0, 0), (DN_HEADS, LANE - 2 * DN_HEADS)))
            dq, dk, dv, dgate, dbg, dgt, gain8 = gdn_chunk_bwd(s["qkv"], s["bg"], s["bgt"], proj, _row(w["dn_og"][j]), s["o"], dgated,
                                                              s["hist"], hd=hd_d, name="gdn_bwd")
            g["dn_og"][j] = _fold8(gain8)
            dbg = dbg + jnp.pad(dgt.T, ((0, 0), (DN_HEADS, LANE - 2 * DN_HEADS)))
            dgl, da8, dd8 = gdn_gates_bwd(proj, alog, dtb, dbg, hd=hd_d, pad=pad, name="gdn_gates_bwd")
            g["dn_alog"][j] = _fold8(da8)[DN_HEADS:2 * DN_HEADS]
            g["dn_dtb"][j] = _fold8(dd8)[DN_HEADS:2 * DN_HEADS]
            dcv, p8 = gdn_prep_bwd(proj, w["dn_conv"][j], (dq, dk, dv), hd=hd_d, name="gdn_prep_bwd")
            g["dn_conv"][j] = _fold8(p8)
            dqkv = conv_transpose(dcv, w["dn_conv"][j], out_dtype=BF16, name="gdn_conv_t")
            dproj = jnp.concatenate([dqkv, dgate.astype(BF16), dgl], axis=1)
            w_in, key = w["dn_w_in"][j], "dn_w_in"
        da = matmul(dproj, w_in, trans_b=True, name="mm_in_dx")
        g[key][j] = matmul(s["a"], dproj, trans_a=True, out_dtype=BF16, name="mm_in_dw")
        dh, p8 = rms_bwd(s["h"], _row(w["g_pre"][i]), da, res=dh_mid, pad=pad, name="rms_pre_bwd")
        g["g_pre"][i] = _fold8(p8)

    grads = {k: (v if k in _BIG else jnp.stack(v)) for k, v in g.items()}
    grads["meta"] = dh[pad:x0]
    return loss, dh[x0:], grads


_ANY = pl.BlockSpec(memory_space=pl.ANY)


def _mesh_place():
    x, y, c = lax.axis_index("x"), lax.axis_index("y"), lax.axis_index("c")
    return x, y, c, 4 * x + 2 * y + c


def _peer(x, y, c, k):
    px, py, pc = (1 - x if k & 4 else x), (1 - y if k & 2 else y), (1 - c if k & 1 else c)
    return (px, py, pc), 4 * px + 2 * py + pc


def _window_blocks(shard):
    return max(-(-(shard * (d + 1)) // LANE) - (shard * d) // LANE for d in range(N_DEV))


def _sds(shape, dtype):
    return jax.ShapeDtypeStruct(tuple(shape), dtype)


def _plan_gather(buf):
    return _sds((N_DEV,) + buf.shape, buf.dtype), (lambda r, i: r), (lambda o, i: o.at[i])


def _plan_scatter(buf):
    return _sds(buf.shape, buf.dtype), (lambda r, i: r.at[i]), (lambda o, i: o.at[i])


def _plan_gather_rows(buf):
    l, r, c = buf.shape
    return _sds((l, N_DEV * r, c), buf.dtype), (lambda ref, i: ref), (lambda o, i: o.at[:, pl.ds(pl.multiple_of(i * r, SUB), r), :])


def _plan_scatter_rows(buf):
    r, c = buf.shape[0] // N_DEV, buf.shape[1]
    return _sds((N_DEV, r, c), buf.dtype), (lambda ref, i: ref.at[pl.ds(pl.multiple_of(i * r, SUB), r), :]), (lambda o, i: o.at[i])


def _plan_scatter_cols(buf, shard):
    ww = _window_blocks(shard) * LANE
    src = lambda ref, i: ref.at[:, pl.ds(pl.multiple_of((shard * i) // LANE * LANE, LANE), ww)]
    return _sds((N_DEV, buf.shape[0], ww), buf.dtype), src, (lambda o, i: o.at[i])


def exchange(bufs, plans, *, name):
    nbuf = len(bufs)

    def body(*refs):
        ins, outs = refs[:nbuf], refs[nbuf:2 * nbuf]
        send_sems, recv_sems, loc_sems = refs[2 * nbuf:]
        x, y, c, me = _mesh_place()
        local = [pltpu.make_async_copy(plans[b][1](ins[b], me), plans[b][2](outs[b], me), loc_sems.at[b]) for b in range(nbuf)]
        for cp in local:
            cp.start()
        sends, recvs = [], []
        for k in range(1, N_DEV):
            peer, pidx = _peer(x, y, c, k)
            for b in range(nbuf):
                sems = dict(send_sem=send_sems.at[b, k - 1], recv_sem=recv_sems.at[b, k - 1], device_id=peer,
                            device_id_type=pl.DeviceIdType.MESH)
                cp = pltpu.make_async_remote_copy(src_ref=plans[b][1](ins[b], pidx), dst_ref=plans[b][2](outs[b], me), **sems)
                cp.start()
                sends.append(cp)
                recvs.append(pltpu.make_async_remote_copy(src_ref=plans[b][1](ins[b], pidx), dst_ref=plans[b][2](outs[b], pidx), **sems))
        for cp in recvs:
            cp.wait_recv()
        for cp in sends:
            cp.wait_send()
        for cp in local:
            cp.wait()

    return pl.pallas_call(
        body, name=name, in_specs=[_ANY] * nbuf, out_specs=[_ANY] * nbuf, out_shape=[p[0] for p in plans],
        scratch_shapes=[pltpu.SemaphoreType.DMA((nbuf, N_DEV - 1)), pltpu.SemaphoreType.DMA((nbuf, N_DEV - 1)),
                        pltpu.SemaphoreType.DMA((nbuf,))],
        compiler_params=pltpu.CompilerParams(has_side_effects=True),
    )(*bufs)


def slot_sum(x, *, name):
    _, r, c = x.shape
    tr = _tile(r, 512, 16)

    def body(x_ref, o_ref):
        acc = x_ref[0].astype(F32)
        for d in range(1, N_DEV):
            acc = acc + x_ref[d].astype(F32)
        o_ref[...] = acc

    return pl.pallas_call(
        body, name=name, grid=(r // tr,), in_specs=[pl.BlockSpec((N_DEV, tr, c), lambda i: (0, i, 0))],
        out_specs=pl.BlockSpec((tr, c), lambda i: (i, 0)), out_shape=jax.ShapeDtypeStruct((r, c), F32),
        compiler_params=_cparams("parallel"),
    )(x)


def assemble_cols(win, shard, *, name):
    _, r, ww = win.shape
    wb = ww // LANE
    nbo = -(-(N_DEV * shard) // LANE)
    tab = []
    for b in range(nbo):
        hits = [(d, b - (shard * d) // LANE) for d in range(N_DEV) if 0 <= b - (shard * d) // LANE < wb]
        assert 1 <= len(hits) <= 2, (b, hits)
        tab.append([hits[0][0], hits[0][1], hits[-1][0], hits[-1][1], len(hits) - 1])
    tab = jnp.array(tab, jnp.int32).T
    tr = _tile(r, 512, 16)

    def body(tab_ref, a_ref, b_ref, o_ref):
        two = tab_ref[4, pl.program_id(1)] > 0
        o_ref[...] = a_ref[0] + jnp.where(two, b_ref[0], jnp.zeros_like(b_ref[0]))

    grid_spec = pltpu.PrefetchScalarGridSpec(
        num_scalar_prefetch=1, grid=(r // tr, nbo),
        in_specs=[pl.BlockSpec((1, tr, LANE), lambda i, b, t: (t[0, b], i, t[1, b])),
                  pl.BlockSpec((1, tr, LANE), lambda i, b, t: (t[2, b], i, t[3, b]))],
        out_specs=pl.BlockSpec((tr, LANE), lambda i, b, t: (i, b)))
    return pl.pallas_call(body, name=name, grid_spec=grid_spec, out_shape=jax.ShapeDtypeStruct((r, nbo * LANE), win.dtype),
                          compiler_params=_cparams("parallel", "parallel"))(tab, win, win)


def _pack(parts, dtype, lead=()):
    nl = len(lead)
    flat = jnp.concatenate([p.astype(dtype).reshape(lead + (-1,)) for p in parts], axis=nl)
    tot = flat.shape[nl]
    rows = -(-tot // (16 * LANE)) * 16
    flat = jnp.pad(flat, [(0, 0)] * nl + [(0, rows * LANE - tot)])
    return flat.reshape(lead + (rows, LANE))


def _unpack(buf, shapes, lead=()):
    nl = len(lead)
    flat = buf.reshape(lead + (-1,))
    out, off = [], 0
    for shp in shapes:
        size = 1
        for s in shp:
            size *= s
        out.append(lax.slice_in_dim(flat, off, off + size, axis=nl).reshape(lead + tuple(shp)))
        off += size
    return out


def _whole(g8, axis):
    t = jnp.moveaxis(g8, 0, axis)
    shp = t.shape
    return t.reshape(shp[:axis] + (shp[axis] * shp[axis + 1],) + shp[axis + 2:])


def _slots(full, axis):
    shp = full.shape
    t = full.reshape(shp[:axis] + (N_DEV, shp[axis] // N_DEV) + shp[axis + 1:])
    return jnp.moveaxis(t, axis, 0)


_PARAMS = (("meta_tokens", 1), ("norm_mix_pre", None), ("norm_mix_post", None), ("norm_ffn_pre", None), ("norm_ffn_post", None),
           ("attn_w_in", 2), ("attn_b_forget", None), ("attn_q_norm", None), ("attn_k_norm", None), ("attn_w_out", 1), ("dn_w_in", 2),
           ("dn_conv", 2), ("dn_a_log", None), ("dn_dt_bias", None), ("dn_o_norm", None), ("dn_w_out", 1), ("ffn_w_up", 2),
           ("ffn_conv", 2), ("ffn_w_down", 1))
_LOCAL_KEY = dict(meta_tokens="meta", norm_mix_pre="g_pre", norm_mix_post="g_post", norm_ffn_pre="g_fpre", norm_ffn_post="g_fpost",
                  attn_w_in="attn_w_in", attn_b_forget="attn_b", attn_q_norm="attn_qg", attn_k_norm="attn_kg", attn_w_out="attn_w_out",
                  dn_w_in="dn_w_in", dn_conv="dn_conv", dn_a_log="dn_alog", dn_dt_bias="dn_dtb", dn_o_norm="dn_og", dn_w_out="dn_w_out",
                  ffn_w_up="ffn_w_up", ffn_conv="ffn_conv", ffn_w_down="ffn_w_down")
_COL_CUT = ("attn_w_in", "dn_w_in", "ffn_w_up")
_ROW_CUT = ("attn_w_out", "dn_w_out", "ffn_w_down")


def kernel(x, meta_tokens, norm_mix_pre, norm_mix_post, norm_ffn_pre, norm_ffn_post, attn_w_in, attn_b_forget, attn_q_norm, attn_k_norm, attn_w_out, dn_w_in, dn_conv, dn_a_log, dn_dt_bias, dn_o_norm, dn_w_out, ffn_w_up, ffn_conv, ffn_w_down, loss_target, m_meta_tokens, m_norm_mix_pre, m_norm_mix_post, m_norm_ffn_pre, m_norm_ffn_post, m_attn_w_in, m_attn_b_forget, m_attn_q_norm, m_attn_k_norm, m_attn_w_out, m_dn_w_in, m_dn_conv, m_dn_a_log, m_dn_dt_bias, m_dn_o_norm, m_dn_w_out, m_ffn_w_up, m_ffn_conv, m_ffn_w_down, v_meta_tokens, v_norm_mix_pre, v_norm_mix_post, v_norm_ffn_pre, v_norm_ffn_post, v_attn_w_in, v_attn_b_forget, v_attn_q_norm, v_attn_k_norm, v_attn_w_out, v_dn_w_in, v_dn_conv, v_dn_a_log, v_dn_dt_bias, v_dn_o_norm, v_dn_w_out, v_ffn_w_up, v_ffn_conv, v_ffn_w_down):
    given = dict(locals())
    names = [p[0] for p in _PARAMS]
    cut = [p for p in _PARAMS if p[1] is not None and p[0] not in _BIG]
    rep = [p for p in _PARAMS if p[1] is None]
    me = 4 * lax.axis_index("x") + 2 * lax.axis_index("y") + lax.axis_index("c")

    bufs, plans = [], []
    for n in _COL_CUT:
        layers, d, shard = given[n].shape
        win = jnp.zeros((layers * d, _window_blocks(shard) * LANE), BF16)
        win = lax.dynamic_update_slice(win, given[n].astype(BF16).reshape(layers * d, shard), (0, (shard * me) % LANE))
        bufs.append(win)
        plans.append(_plan_gather(win))
    for n in _ROW_CUT:
        bufs.append(given[n].astype(BF16))
        plans.append(_plan_gather_rows(bufs[-1]))
    bufs.append(_pack([given[n] for n, _ in cut], F32))
    plans.append(_plan_gather(bufs[-1]))
    got = exchange(bufs, plans, name="gather_weights")
    w = {}
    for n, g8 in zip(_COL_CUT, got[:3]):
        layers, d, shard = given[n].shape
        w[n] = assemble_cols(g8, shard, name="assemble_" + n).reshape(layers, d, -1)
    for n, full in zip(_ROW_CUT, got[3:6]):
        w[n] = full
    for (n, axis), g8 in zip(cut, _unpack(got[6], [given[n].shape for n, _ in cut], lead=(N_DEV,))):
        w[_LOCAL_KEY[n]] = _whole(g8, axis)
    for n, _ in rep:
        w[_LOCAL_KEY[n]] = given[n]

    loss, grad_x, g = local_step(x[0], loss_target[0], w)

    bufs, plans, what = [], [], []
    for n in _COL_CUT:
        for layer, gl in enumerate(g[n]):
            bufs.append(gl)
            plans.append(_plan_scatter_cols(gl, given[n].shape[2]))
            what.append((n, layer))
    for n in _ROW_CUT:
        for layer, gl in enumerate(g[n]):
            bufs.append(gl)
            plans.append(_plan_scatter_rows(gl))
            what.append((n, layer))
    nbig = len(bufs)
    bufs.append(_pack([_slots(g[_LOCAL_KEY[n]], axis) for n, axis in cut], F32, lead=(N_DEV,)))
    plans.append(_plan_scatter(bufs[-1]))
    bufs.append(_pack([g[_LOCAL_KEY[n]] for n, _ in rep] + [loss.reshape(1)], F32))
    plans.append(_plan_gather(bufs[-1]))
    got = exchange(bufs, plans, name="reduce_grads")
    per_layer = {n: [] for n in _BIG}
    for (n, layer), r8 in zip(what, got[:nbig]):
        tot = slot_sum(r8, name="sum_" + n)
        if n in _COL_CUT:
            shard = given[n].shape[2]
            tot = lax.dynamic_slice_in_dim(tot, (shard * me) % LANE, shard, axis=1)
        per_layer[n].append(tot)
    grads = {n: jnp.stack(v) for n, v in per_layer.items()}
    for (n, _), gv in zip(cut, _unpack(slot_sum(got[nbig], name="sum_cut"), [given[n].shape for n, _ in cut])):
        grads[n] = gv
    rep_sum = _unpack(slot_sum(got[nbig + 1], name="sum_rep"), [given[n].shape for n, _ in rep] + [(1,)])
    for (n, _), gv in zip(rep, rep_sum):
        grads[n] = gv
    loss_all = rep_sum[-1].reshape(())

    deltas, new_m, new_v = {}, {}, {}
    for n in names:
        shp = given[n].shape
        two_d = (-1, shp[-1])
        d, mn, vn = adamw(given[n].reshape(two_d), grads[n].reshape(two_d), given["m_" + n].reshape(two_d), given["v_" + n].reshape(two_d),
                          name="adamw_" + n)
        deltas[n], new_m[n], new_v[n] = d.reshape(shp), mn.reshape(shp), vn.reshape(shp)
    return (loss_all, grad_x[None], *[grads[n] for n in names], *[deltas[n] for n in names], *[new_m[n] for n in names],
            *[new_v[n] for n in names])
```

```python
import functools

import jax
import jax.numpy as jnp
from jax import lax
from jax.experimental import pallas as pl
from jax.experimental.pallas import tpu as pltpu

F32 = jnp.float32
BF16 = jnp.bfloat16
LANE = 128
SUB = 8
N_DEV = 8
N_META = 16
ATT_HEADS, ATT_HEAD_DIM = 16, 64
DN_HEADS, DN_HEAD_DIM, DN_CHUNK, DN_CONV = 8, 128, 64, 4
FFN_CONV = 3
EPS = 1e-6
NEG = -1e30
ADAM_LR, ADAM_B1, ADAM_B2, ADAM_EPS, ADAM_WD, ADAM_STEP = 0.001, 0.9, 0.999, 1e-08, 0.01, 10
HI = lax.Precision.HIGHEST
VMEM_LIMIT = 56 * 1024 * 1024


def _tile(n, target, align=LANE):
    if n <= target:
        return n
    best = None
    for t in range(align, target + 1, align):
        if n % t == 0:
            best = t
    assert best is not None, (n, target, align)
    return best


def _iota(shape, dim):
    return lax.broadcasted_iota(jnp.int32, shape, dim)


def _colsum8(x):
    r, c = x.shape
    return x.reshape(r // SUB, SUB, c).sum(axis=0)


def _cparams(*sem):
    return pltpu.CompilerParams(dimension_semantics=sem, vmem_limit_bytes=VMEM_LIMIT)


def _sigmoid(x):
    return 1.0 / (1.0 + jnp.exp(-x))


def _softplus(x):
    return jnp.maximum(x, 0.0) + jnp.log(1.0 + jnp.exp(-jnp.abs(x)))


def _accum(ref, part, first):
    @pl.when(first)
    def _():
        ref[...] = part

    @pl.when(jnp.logical_not(first))
    def _():
        ref[...] += part


def matmul(a, b, *, trans_a=False, trans_b=False, out_dtype=F32, tm=1664, tn=1408, tk=1664, name="matmul"):
    if trans_a:
        kdim, m = a.shape
    else:
        m, kdim = a.shape
    if trans_b:
        n, kb = b.shape
    else:
        kb, n = b.shape
    assert kb == kdim, (a.shape, b.shape, trans_a, trans_b)
    tm, tn, tk = _tile(m, tm), _tile(n, tn), _tile(kdim, tk)
    nk = kdim // tk
    dims = (((0 if trans_a else 1,), (1 if trans_b else 0,)), ((), ()))
    cdt = BF16

    def body(a_ref, b_ref, o_ref, *acc):
        part = lax.dot_general(a_ref[...].astype(cdt), b_ref[...].astype(cdt), dims, preferred_element_type=F32)
        if nk == 1:
            o_ref[...] = part.astype(o_ref.dtype)
            return
        k = pl.program_id(2)
        _accum(acc[0], part, k == 0)

        @pl.when(k == nk - 1)
        def _():
            o_ref[...] = acc[0][...].astype(o_ref.dtype)

    a_spec = pl.BlockSpec((tk, tm), lambda i, j, k: (k, i)) if trans_a else pl.BlockSpec((tm, tk), lambda i, j, k: (i, k))
    b_spec = pl.BlockSpec((tn, tk), lambda i, j, k: (j, k)) if trans_b else pl.BlockSpec((tk, tn), lambda i, j, k: (k, j))
    return pl.pallas_call(
        body,
        name=name,
        grid=(m // tm, n // tn, nk),
        in_specs=[a_spec, b_spec],
        out_specs=pl.BlockSpec((tm, tn), lambda i, j, k: (i, j)),
        out_shape=jax.ShapeDtypeStruct((m, n), out_dtype),
        scratch_shapes=[] if nk == 1 else [pltpu.VMEM((tm, tn), F32)],
        compiler_params=_cparams("parallel", "parallel", "arbitrary"),
    )(a, b)


def rms_fwd(x, g, *, res=None, out_dtype, name):
    n, d = x.shape
    tr = _tile(n, 640, SUB)

    def body(*refs):
        x_ref, g_ref = refs[0], refs[1]
        o_ref = refs[-1]
        xv = x_ref[...]
        y = xv * lax.rsqrt(jnp.mean(xv * xv, axis=-1, keepdims=True) + EPS) * g_ref[...]
        if res is not None:
            y = y + refs[2][...]
        o_ref[...] = y.astype(o_ref.dtype)

    row = pl.BlockSpec((tr, d), lambda i: (i, 0))
    ins = [x, g] + ([res] if res is not None else [])
    return pl.pallas_call(
        body, name=name, grid=(n // tr,),
        in_specs=[row, pl.BlockSpec((1, d), lambda i: (0, 0))] + ([row] if res is not None else []),
        out_specs=row, out_shape=jax.ShapeDtypeStruct((n, d), out_dtype),
        compiler_params=_cparams("parallel"),
    )(*ins)


def rms_bwd(x, g, dy, *, res=None, pad, name):
    n, d = x.shape
    tr = _tile(n, 640, SUB)

    def body(*refs):
        x_ref, g_ref, dy_ref = refs[:3]
        dx_ref, dg_ref = refs[-2:]
        i = pl.program_id(0)
        xv = x_ref[...]
        r = lax.rsqrt(jnp.mean(xv * xv, axis=-1, keepdims=True) + EPS)
        xh = xv * r
        dyv = dy_ref[...].astype(F32)
        gdy = dyv * g_ref[...]
        dx = r * (gdy - xh * jnp.mean(xh * gdy, axis=-1, keepdims=True))
        if res is not None:
            dx = dx + refs[3][...]
        rows = i * tr + _iota((tr, 1), 0)
        dx_ref[...] = jnp.where(rows >= pad, dx, 0.0)
        _accum(dg_ref, _colsum8(dyv * xh), i == 0)

    row = pl.BlockSpec((tr, d), lambda i: (i, 0))
    ins = [x, g, dy] + ([res] if res is not None else [])
    return pl.pallas_call(
        body, name=name, grid=(n // tr,),
        in_specs=[row, pl.BlockSpec((1, d), lambda i: (0, 0)), row] + ([row] if res is not None else []),
        out_specs=[row, pl.BlockSpec((SUB, d), lambda i: (0, 0))],
        out_shape=[jax.ShapeDtypeStruct((n, d), F32), jax.ShapeDtypeStruct((SUB, d), F32)],
        compiler_params=_cparams("arbitrary"),
    )(*ins)


def _shift_down(cur, prev8, s):
    if s == 0:
        return cur
    out = pltpu.roll(cur, s, 0)
    row = _iota(cur.shape, 0)
    for r in range(s):
        out = jnp.where(row == r, prev8[SUB - s + r:SUB - s + r + 1, :], out)
    return out


def _shift_up(cur, next8, s):
    if s == 0:
        return cur
    tr = cur.shape[0]
    out = pltpu.roll(cur, tr - s, 0)
    row = _iota(cur.shape, 0)
    for r in range(s):
        out = jnp.where(row == tr - s + r, next8[r:r + 1, :], out)
    return out


def _conv_rows(cur, prev8, w):
    kw = w.shape[0]
    acc = w[kw - 1:kw, :] * cur
    for k in range(kw - 1):
        acc = acc + w[k:k + 1, :] * _shift_down(cur, prev8, kw - 1 - k)
    return acc


def conv_transpose(dy, w, *, out_dtype, name):
    n, c = dy.shape
    kw = w.shape[0]
    tr = _tile(n, 640, SUB)
    tc = _tile(c, 512)
    nb8 = n // SUB

    def body(dy_ref, nx_ref, w_ref, o_ref):
        i = pl.program_id(0)
        cur = dy_ref[...].astype(F32)
        nxt = jnp.where(i == pl.num_programs(0) - 1, 0.0, nx_ref[...].astype(F32))
        wv = w_ref[...]
        acc = wv[kw - 1:kw, :] * cur
        for k in range(kw - 1):
            acc = acc + wv[k:k + 1, :] * _shift_up(cur, nxt, kw - 1 - k)
        o_ref[...] = acc.astype(o_ref.dtype)

    return pl.pallas_call(
        body, name=name, grid=(n // tr, c // tc),
        in_specs=[pl.BlockSpec((tr, tc), lambda i, j: (i, j)),
                  pl.BlockSpec((SUB, tc), lambda i, j: (jnp.minimum((i + 1) * (tr // SUB), nb8 - 1), j)),
                  pl.BlockSpec((kw, tc), lambda i, j: (0, j))],
        out_specs=pl.BlockSpec((tr, tc), lambda i, j: (i, j)),
        out_shape=jax.ShapeDtypeStruct((n, c), out_dtype),
        compiler_params=_cparams("parallel", "parallel"),
    )(dy, dy, w)


_GELU_C = 0.7978845608028654
_GELU_A = 0.044715


def _gelu(x):
    return 0.5 * x * (1.0 + jnp.tanh(_GELU_C * (x + _GELU_A * x * x * x)))


def _gelu_grad(x):
    th = jnp.tanh(_GELU_C * (x + _GELU_A * x * x * x))
    return 0.5 * (1.0 + th) + 0.5 * x * (1.0 - th * th) * _GELU_C * (1.0 + 3.0 * _GELU_A * x * x)


def conv_glu_fwd(u, cw, *, name):
    n, f2 = u.shape
    f = f2 // 2
    tr = _tile(n, 640, SUB)
    tc = _tile(f, 512)
    nf = f // tc
    r8 = tr // SUB

    def body(ug_ref, uu_ref, pg_ref, pu_ref, wg_ref, wu_ref, o_ref):
        first = pl.program_id(0) == 0
        pg = jnp.where(first, 0.0, pg_ref[...])
        pu = jnp.where(first, 0.0, pu_ref[...])
        gate = _conv_rows(ug_ref[...], pg, wg_ref[...])
        up = _conv_rows(uu_ref[...], pu, wu_ref[...])
        o_ref[...] = (_gelu(gate) * up).astype(o_ref.dtype)

    prev = lambda off: pl.BlockSpec((SUB, tc), lambda i, j: (jnp.maximum(i * r8 - 1, 0), j + off))
    return pl.pallas_call(
        body, name=name, grid=(n // tr, nf),
        in_specs=[pl.BlockSpec((tr, tc), lambda i, j: (i, j)), pl.BlockSpec((tr, tc), lambda i, j: (i, j + nf)),
                  prev(0), prev(nf),
                  pl.BlockSpec((FFN_CONV, tc), lambda i, j: (0, j)), pl.BlockSpec((FFN_CONV, tc), lambda i, j: (0, j + nf))],
        out_specs=pl.BlockSpec((tr, tc), lambda i, j: (i, j)),
        out_shape=jax.ShapeDtypeStruct((n, f), BF16),
        compiler_params=_cparams("parallel", "parallel"),
    )(u, u, u, u, cw, cw)


def conv_glu_bwd(u, cw, dact, *, name):
    n, f2 = u.shape
    f = f2 // 2
    tr = _tile(n, 640, SUB)
    tc = _tile(f, 512)
    nf = f // tc
    r8 = tr // SUB

    def body(us_ref, up_ref, ps_ref, pp_ref, ws_ref, wp_ref, da_ref, o_ref, dw_ref):
        j, i = pl.program_id(0), pl.program_id(1)
        first = i == 0
        ps = jnp.where(first, 0.0, ps_ref[...])
        pp = jnp.where(first, 0.0, pp_ref[...])
        cur = us_ref[...]
        mine = _conv_rows(cur, ps, ws_ref[...])
        other = _conv_rows(up_ref[...], pp, wp_ref[...])
        da = da_ref[...].astype(F32)

        def finish(d_mine):
            o_ref[...] = d_mine
            for k in range(FFN_CONV):
                part = _colsum8(d_mine * _shift_down(cur, ps, FFN_CONV - 1 - k))

                @pl.when(first)
                def _():
                    dw_ref[k] = part

                @pl.when(jnp.logical_not(first))
                def _():
                    dw_ref[k] += part

        @pl.when(j < nf)
        def _():
            finish(da * other * _gelu_grad(mine))

        @pl.when(j >= nf)
        def _():
            finish(da * _gelu(other))

    part_of = lambda j: (j + nf) % (2 * nf)
    return pl.pallas_call(
        body, name=name, grid=(2 * nf, n // tr),
        in_specs=[pl.BlockSpec((tr, tc), lambda j, i: (i, j)), pl.BlockSpec((tr, tc), lambda j, i: (i, part_of(j))),
                  pl.BlockSpec((SUB, tc), lambda j, i: (jnp.maximum(i * r8 - 1, 0), j)),
                  pl.BlockSpec((SUB, tc), lambda j, i: (jnp.maximum(i * r8 - 1, 0), part_of(j))),
                  pl.BlockSpec((FFN_CONV, tc), lambda j, i: (0, j)), pl.BlockSpec((FFN_CONV, tc), lambda j, i: (0, part_of(j))),
                  pl.BlockSpec((tr, tc), lambda j, i: (i, j % nf))],
        out_specs=[pl.BlockSpec((tr, tc), lambda j, i: (i, j)), pl.BlockSpec((FFN_CONV, SUB, tc), lambda j, i: (0, 0, j))],
        out_shape=[jax.ShapeDtypeStruct((n, f2), F32), jax.ShapeDtypeStruct((FFN_CONV, SUB, f2), F32)],
        compiler_params=_cparams("parallel", "arbitrary"),
    )(u, u, u, u, cw, cw, dact)


def cumsum_rows(x, *, reverse, name):
    n, c = x.shape
    tr = LANE
    nb = n // tr

    def body(x_ref, o_ref, carry_ref):
        i = pl.program_id(0)

        @pl.when(i == 0)
        def _():
            carry_ref[...] = jnp.zeros_like(carry_ref)

        r, cc = _iota((tr, tr), 0), _iota((tr, tr), 1)
        tri = jnp.where((cc >= r) if reverse else (cc <= r), 1.0, 0.0).astype(F32)
        out = jnp.dot(tri, x_ref[...], precision=HI, preferred_element_type=F32) + carry_ref[...]
        o_ref[...] = out
        carry_ref[...] = out[0:1, :] if reverse else out[tr - 1:tr, :]

    idx = (lambda i: (nb - 1 - i, 0)) if reverse else (lambda i: (i, 0))
    return pl.pallas_call(
        body, name=name, grid=(nb,),
        in_specs=[pl.BlockSpec((tr, c), idx)], out_specs=pl.BlockSpec((tr, c), idx),
        out_shape=jax.ShapeDtypeStruct((n, c), F32), scratch_shapes=[pltpu.VMEM((1, c), F32)],
        compiler_params=_cparams("arbitrary"),
    )(x)


def _group_sum64(x):
    r, c = x.shape
    a, b = _iota((LANE, LANE), 0), _iota((LANE, LANE), 1)
    bd = jnp.where((a // 64) == (b // 64), 1.0, 0.0).astype(F32)
    parts = [jnp.dot(x[:, k * LANE:(k + 1) * LANE], bd, precision=HI, preferred_element_type=F32) for k in range(c // LANE)]
    return parts[0] if len(parts) == 1 else jnp.concatenate(parts, axis=1)


def attn_prep(proj, qg, kg, bf, *, hd, name):
    n = proj.shape[0]
    tr = _tile(n, 640, SUB)
    scale = ATT_HEAD_DIM ** -0.5
    nh = hd // LANE

    def body(q_ref, k_ref, v_ref, f_ref, qg_ref, kg_ref, bf_ref, qo_ref, ko_ref, vo_ref, lf_ref):
        def norm(x, g):
            ms = _group_sum64(x * x) * (1.0 / ATT_HEAD_DIM)
            return x * lax.rsqrt(ms + EPS) * g

        qo_ref[...] = (norm(q_ref[...], qg_ref[...]) * scale).astype(qo_ref.dtype)
        ko_ref[...] = norm(k_ref[...], kg_ref[...]).astype(ko_ref.dtype)
        vo_ref[...] = v_ref[...].astype(vo_ref.dtype)
        lf_ref[...] = -_softplus(-(f_ref[...] + bf_ref[...]))

    col = lambda c: pl.BlockSpec((tr, hd), lambda i: (i, c))
    vec = lambda w: pl.BlockSpec((1, w), lambda i: (0, 0))
    return pl.pallas_call(
        body, name=name, grid=(n // tr,),
        in_specs=[col(0), col(1), col(2), pl.BlockSpec((tr, LANE), lambda i: (i, 4 * nh)), vec(hd), vec(hd), vec(LANE)],
        out_specs=[col(0), col(0), col(0), pl.BlockSpec((tr, LANE), lambda i: (i, 0))],
        out_shape=[jax.ShapeDtypeStruct((n, hd), BF16)] * 3 + [jax.ShapeDtypeStruct((n, LANE), F32)],
        compiler_params=_cparams("parallel"),
    )(proj, proj, proj, proj, qg, kg, bf)


def _half_mask(shape):
    return _iota(shape, 1) < ATT_HEAD_DIM


def flash_fwd(qs, kn, v, proj, ct, *, pad, t, name):
    n, hd = qs.shape
    npair = hd // LANE
    nb = n // t
    gate0 = 3 * npair

    def body(q_ref, k_ref, v_ref, g_ref, c_ref, o_ref, og_ref, lse_ref, m_ref, acc_ref):
        i, j = pl.program_id(1), pl.program_id(2)

        @pl.when(j == 0)
        def _():
            m_ref[...] = jnp.full_like(m_ref, NEG)
            acc_ref[...] = jnp.zeros_like(acc_ref)

        def step(masked):
            q, k, vv = q_ref[...], k_ref[...], v_ref[...]
            half0 = _half_mask(q.shape)
            if masked:
                rowpos = i * t + _iota((t, t), 0)
                colpos = j * t + _iota((t, t), 1)
                mask = (colpos <= rowpos) & (colpos >= pad)
            for hh in range(2):
                sel = half0 if hh == 0 else jnp.logical_not(half0)
                qm = jnp.where(sel, q, jnp.zeros_like(q))
                v1 = jnp.where(sel, vv, jnp.ones_like(vv))
                s = lax.dot_general(qm, k, (((1,), (1,)), ((), ())), preferred_element_type=F32) - c_ref[0, hh:hh + 1, :]
                if masked:
                    s = jnp.where(mask, s, NEG)
                m_prev = m_ref[hh]
                m_new = jnp.maximum(m_prev, jnp.max(s, axis=1, keepdims=True))
                p = jnp.exp(s - m_new[:, 0:1])
                if masked:
                    p = jnp.where(mask, p, 0.0)
                acc_ref[hh] = jnp.exp(m_prev - m_new) * acc_ref[hh] + jnp.dot(p.astype(vv.dtype), v1, preferred_element_type=F32)
                m_ref[hh] = m_new

        edge = (j == i) | (j == 0)

        @pl.when(edge & (j <= i))
        def _():
            step(True)

        @pl.when(jnp.logical_not(edge) & (j < i))
        def _():
            step(False)

        @pl.when(j == i)
        def _():
            half0 = _half_mask((t, LANE))
            a0, a1 = acc_ref[0], acc_ref[1]
            l = jnp.where(half0, a0[:, ATT_HEAD_DIM:ATT_HEAD_DIM + 1], a1[:, 0:1])
            acc = jnp.where(half0, a0, a1)
            m = jnp.where(half0, m_ref[0], m_ref[1])
            live = l > 0.0
            o = jnp.where(live, acc / jnp.where(live, l, 1.0), 0.0)
            o_ref[...] = o
            og_ref[...] = (o * _sigmoid(g_ref[...])).astype(og_ref.dtype)
            lse_ref[...] = jnp.where(live, m + jnp.log(jnp.where(live, l, 1.0)), 0.0)

    qspec = pl.BlockSpec((t, LANE), lambda p, i, j: (i, p))
    kspec = pl.BlockSpec((t, LANE), lambda p, i, j: (jnp.minimum(j, i), p))
    return pl.pallas_call(
        body, name=name, grid=(npair, nb, nb),
        in_specs=[qspec, kspec, kspec, pl.BlockSpec((t, LANE), lambda p, i, j: (i, gate0 + p)),
                  pl.BlockSpec((1, 2, t), lambda p, i, j: (p, 0, jnp.minimum(j, i)))],
        out_specs=[qspec, qspec, qspec],
        out_shape=[jax.ShapeDtypeStruct((n, hd), F32), jax.ShapeDtypeStruct((n, hd), BF16), jax.ShapeDtypeStruct((n, hd), F32)],
        scratch_shapes=[pltpu.VMEM((2, t, LANE), F32)] * 2,
        compiler_params=_cparams("parallel", "parallel", "arbitrary"),
    )(qs, kn, v, proj, ct)


def attn_bwd_prep(dgated, o, proj, *, hd, name):
    n = o.shape[0]
    tr = _tile(n, 640, SUB)
    gate0 = 3

    def body(dg_ref, o_ref, g_ref, do_ref, dl_ref, dgate_ref):
        dg, ov = dg_ref[...], o_ref[...]
        sg = _sigmoid(g_ref[...])
        do = dg * sg
        do_ref[...] = do.astype(do_ref.dtype)
        dl_ref[...] = _group_sum64(do * ov)
        dgate_ref[...] = dg * ov * sg * (1.0 - sg)

    row = pl.BlockSpec((tr, hd), lambda i: (i, 0))
    return pl.pallas_call(
        body, name=name, grid=(n // tr,),
        in_specs=[row, row, pl.BlockSpec((tr, hd), lambda i: (i, gate0))],
        out_specs=[row, row, row],
        out_shape=[jax.ShapeDtypeStruct((n, hd), BF16), jax.ShapeDtypeStruct((n, hd), F32), jax.ShapeDtypeStruct((n, hd), F32)],
        compiler_params=_cparams("parallel"),
    )(dgated, o, proj)


def flash_bwd(qs, kn, v, do, lse, delta, ct, *, pad, t, name):
    n, hd = qs.shape
    npair = hd // LANE
    nb = n // t

    def body(q_ref, k_ref, v_ref, do_ref, lse_ref, dl_ref, c_ref, dq_ref, dk_ref, dv_ref, dck_ref, dcr_ref, dk_acc, dv_acc, dck_acc):
        j, i = pl.program_id(1), pl.program_id(2)

        @pl.when((j == 0) & (i == 0))
        def _():
            dq_ref[...] = jnp.zeros_like(dq_ref)
            dcr_ref[...] = jnp.zeros_like(dcr_ref)

        @pl.when(i == j)
        def _():
            dk_acc[...] = jnp.zeros_like(dk_acc)
            dv_acc[...] = jnp.zeros_like(dv_acc)
            dck_acc[...] = jnp.zeros_like(dck_acc)

        def step(masked):
            q, k, vv, dov = q_ref[...], k_ref[...], v_ref[...], do_ref[...]
            half0 = _half_mask(q.shape)
            if masked:
                rowpos = i * t + _iota((t, t), 0)
                colpos = j * t + _iota((t, t), 1)
                mask = (colpos <= rowpos) & (colpos >= pad)
            nt = (((1,), (1,)), ((), ()))
            tn = (((0,), (0,)), ((), ()))
            dq_h, dk_h, dv_h = [], [], []
            for hh in range(2):
                sel = half0 if hh == 0 else jnp.logical_not(half0)
                qm = jnp.where(sel, q, jnp.zeros_like(q))
                dom = jnp.where(sel, dov, jnp.zeros_like(dov))
                q1 = jnp.where(sel, q, jnp.ones_like(q))
                k1 = jnp.where(sel, k, jnp.ones_like(k))
                x = lax.dot_general(qm, k, nt, preferred_element_type=F32) - c_ref[0, hh:hh + 1, :] - lse_ref[:, hh * 64:hh * 64 + 1]
                if masked:
                    p = jnp.where(mask, jnp.exp(jnp.where(mask, x, NEG)), 0.0)
                else:
                    p = jnp.exp(x)
                dp = lax.dot_general(dom, vv, nt, preferred_element_type=F32)
                ds = p * (dp - dl_ref[:, hh * 64:hh * 64 + 1])
                dsb, pb = ds.astype(k.dtype), p.astype(k.dtype)
                dq_h.append(jnp.dot(dsb, k1, preferred_element_type=F32))
                dk_h.append(lax.dot_general(dsb, q1, tn, preferred_element_type=F32))
                dv_h.append(lax.dot_general(pb, dov, tn, preferred_element_type=F32))
            rows = pl.ds(pl.multiple_of(i * t, t), t)
            dq_ref[rows, :] += jnp.where(half0, dq_h[0], dq_h[1])
            dcr_ref[rows, :] += jnp.where(half0, dq_h[1], dq_h[0])
            dk_acc[...] += jnp.where(half0, dk_h[0], dk_h[1])
            dck_acc[...] += jnp.where(half0, dk_h[1], dk_h[0])
            dv_acc[...] += jnp.where(half0, dv_h[0], dv_h[1])

        edge = (i == j) | (j == 0)

        @pl.when(edge & (i >= j))
        def _():
            step(True)

        @pl.when(jnp.logical_not(edge) & (i > j))
        def _():
            step(False)

        @pl.when(i == nb - 1)
        def _():
            dk_ref[...] = dk_acc[...]
            dv_ref[...] = dv_acc[...]
            dck_ref[...] = dck_acc[...]

    qspec = pl.BlockSpec((t, LANE), lambda p, j, i: (jnp.maximum(i, j), p))
    kspec = pl.BlockSpec((t, LANE), lambda p, j, i: (j, p))
    cspec = pl.BlockSpec((1, 2, t), lambda p, j, i: (p, 0, j))
    whole = pl.BlockSpec((n, LANE), lambda p, j, i: (0, p))
    return pl.pallas_call(
        body, name=name, grid=(npair, nb, nb),
        in_specs=[qspec, kspec, kspec, qspec, qspec, qspec, cspec],
        out_specs=[whole, kspec, kspec, kspec, whole],
        out_shape=[jax.ShapeDtypeStruct((n, hd), F32)] * 5,
        scratch_shapes=[pltpu.VMEM((t, LANE), F32)] * 3,
        compiler_params=_cparams("parallel", "arbitrary", "arbitrary"),
    )(qs, kn, v, do, lse, delta, ct)


def attn_in_bwd(dqs, dkn, proj, qg, kg, bf, dlogf, *, hd, pad, name):
    n = proj.shape[0]
    tr = _tile(n, 640, SUB)
    scale = ATT_HEAD_DIM ** -0.5
    nh = hd // LANE

    def body(dq_ref, dk_ref, q_ref, k_ref, f_ref, qg_ref, kg_ref, bf_ref, dl_ref, oq_ref, ok_ref, of_ref, gq_ref, gk_ref, gb_ref):
        i = pl.program_id(0)

        def back(x, g, dy):
            r = lax.rsqrt(_group_sum64(x * x) * (1.0 / ATT_HEAD_DIM) + EPS)
            xh = x * r
            gdy = dy * g
            dx = r * (gdy - xh * _group_sum64(xh * gdy) * (1.0 / ATT_HEAD_DIM))
            return dx, _colsum8(dy * xh)

        dxq, gq = back(q_ref[...], qg_ref[...], dq_ref[...] * scale)
        dxk, gk = back(k_ref[...], kg_ref[...], dk_ref[...])
        oq_ref[...] = dxq.astype(oq_ref.dtype)
        ok_ref[...] = dxk.astype(ok_ref.dtype)
        rows = i * tr + _iota((tr, 1), 0)
        dfl = jnp.where(rows >= pad, dl_ref[...] * _sigmoid(-(f_ref[...] + bf_ref[...])), 0.0)
        of_ref[...] = dfl.astype(of_ref.dtype)
        _accum(gq_ref, gq, i == 0)
        _accum(gk_ref, gk, i == 0)
        _accum(gb_ref, _colsum8(dfl), i == 0)

    row = pl.BlockSpec((tr, hd), lambda i: (i, 0))
    col = lambda c: pl.BlockSpec((tr, hd), lambda i: (i, c))
    nar = pl.BlockSpec((tr, LANE), lambda i: (i, 0))
    vec = lambda w: pl.BlockSpec((1, w), lambda i: (0, 0))
    acc = lambda w: pl.BlockSpec((SUB, w), lambda i: (0, 0))
    return pl.pallas_call(
        body, name=name, grid=(n // tr,),
        in_specs=[row, row, col(0), col(1), pl.BlockSpec((tr, LANE), lambda i: (i, 4 * nh)), vec(hd), vec(hd), vec(LANE), nar],
        out_specs=[row, row, nar, acc(hd), acc(hd), acc(LANE)],
        out_shape=[jax.ShapeDtypeStruct((n, hd), BF16)] * 2 + [jax.ShapeDtypeStruct((n, LANE), BF16),
                   jax.ShapeDtypeStruct((SUB, hd), F32), jax.ShapeDtypeStruct((SUB, hd), F32), jax.ShapeDtypeStruct((SUB, LANE), F32)],
        compiler_params=_cparams("arbitrary"),
    )(dqs, dkn, proj, proj, proj, qg, kg, bf, dlogf)


def _silu(x):
    return x * _sigmoid(x)


def _silu_grad(x):
    s = _sigmoid(x)
    return s * (1.0 + x * (1.0 - s))


def gdn_prep(proj, cw, *, hd, name):
    n = proj.shape[0]
    tr = _tile(n, 640, SUB)
    nh = hd // LANE
    r8 = tr // SUB
    qscale = DN_HEAD_DIM ** -0.5

    def body(x_ref, p_ref, w_ref, o_ref):
        i, c = pl.program_id(0), pl.program_id(1)
        prev = jnp.where(i == 0, 0.0, p_ref[...])
        s = _silu(_conv_rows(x_ref[...], prev, w_ref[...]))
        r = lax.rsqrt(jnp.sum(s * s, axis=-1, keepdims=True) + EPS)
        mult = jnp.where(c < nh, r * qscale, jnp.where(c < 2 * nh, r, 1.0))
        o_ref[...] = s * mult

    return pl.pallas_call(
        body, name=name, grid=(n // tr, 3 * nh),
        in_specs=[pl.BlockSpec((tr, LANE), lambda i, c: (i, c)),
                  pl.BlockSpec((SUB, LANE), lambda i, c: (jnp.maximum(i * r8 - 1, 0), c)),
                  pl.BlockSpec((DN_CONV, LANE), lambda i, c: (0, c))],
        out_specs=pl.BlockSpec((tr, LANE), lambda i, c: (i, c)),
        out_shape=jax.ShapeDtypeStruct((n, 3 * hd), F32),
        compiler_params=_cparams("parallel", "parallel"),
    )(proj, proj, cw)


def _chunk_tri(reverse):
    r, c = _iota((LANE, LANE), 0), _iota((LANE, LANE), 1)
    same = (r // DN_CHUNK) == (c // DN_CHUNK)
    return jnp.where(same & ((c >= r) if reverse else (c <= r)), 1.0, 0.0).astype(F32)


def gdn_gates(proj, alog, dtb, *, hd, name):
    n = proj.shape[0]
    gcol = 4 * (hd // LANE)

    def body(x_ref, a_ref, d_ref, o_ref):
        x = x_ref[...]
        lane = _iota(x.shape, 1)
        g = -jnp.exp(a_ref[...]) * _softplus(x + d_ref[...])
        gc = jnp.dot(_chunk_tri(False), jnp.where((lane >= DN_HEADS) & (lane < 2 * DN_HEADS), g, 0.0), precision=HI,
                     preferred_element_type=F32)
        o_ref[...] = jnp.where(lane < DN_HEADS, _sigmoid(x), gc)

    vec = pl.BlockSpec((1, LANE), lambda i: (0, 0))
    return pl.pallas_call(
        body, name=name, grid=(n // LANE,),
        in_specs=[pl.BlockSpec((LANE, LANE), lambda i: (i, gcol)), vec, vec],
        out_specs=pl.BlockSpec((LANE, LANE), lambda i: (i, 0)),
        out_shape=jax.ShapeDtypeStruct((n, LANE), F32),
        compiler_params=_cparams("parallel"),
    )(proj, alog, dtb)


def _mm(a, b, ca=1, cb=0):
    return lax.dot_general(a.astype(BF16), b.astype(BF16), (((ca,), (cb,)), ((), ())), preferred_element_type=F32)


def _mmh(a, b):
    return jnp.dot(a, b, precision=HI, preferred_element_type=F32)


def _lockstep(gens):
    gens = list(gens)
    out = [None] * len(gens)
    live = list(range(len(gens)))
    while live:
        nxt = []
        for i in live:
            try:
                next(gens[i])
                nxt.append(i)
            except StopIteration as stop:
                out[i] = stop.value
        live = nxt
    return out


def _gdn_common(q, k, v, beta, gc_c, gc_r):
    r, c = _iota((LANE, LANE), 0), _iota((LANE, LANE), 1)
    same = (r // DN_CHUNK) == (c // DN_CHUNK)
    incl, strict = same & (r >= c), same & (r > c)
    d = jnp.exp(jnp.where(incl, gc_c - gc_r, NEG))
    kk = _mm(k, k, 1, 1)
    qk = _mm(q, k, 1, 1)
    yield
    ahat = jnp.where(strict, kk * d, 0.0)
    a = ahat * beta
    eye = jnp.where(r == c, 1.0, 0.0).astype(F32)
    t = eye - a
    pw = _mmh(a, a)
    yield
    for step in range(5):
        t = t + _mmh(t, pw)
        if step < 4:
            pw = _mmh(pw, pw)
        yield
    row = _iota((LANE, 1), 0)
    gl0 = jnp.sum(jnp.where(row == DN_CHUNK - 1, gc_c, 0.0), axis=0, keepdims=True)
    gl1 = jnp.sum(jnp.where(row == LANE - 1, gc_c, 0.0), axis=0, keepdims=True)
    gam = jnp.exp(gc_c)
    lam = jnp.exp(jnp.where(row < DN_CHUNK, gl0, gl1) - gc_c)
    kb, vb = k * (beta * gam), v * beta
    cm = dict(incl=incl, strict=strict, d=d, kk=kk, ahat=ahat, t=t, gam=gam, lam=lam, kb=kb, vb=vb, w=_mm(t, kb), u0=_mm(t, vb),
              qk=qk, pm=jnp.where(incl, qk * d, 0.0), qg=q * gam, kl=k * lam, g0=jnp.exp(gl0), g1=jnp.exp(gl1))
    yield
    return cm


def _gdn_states(cm, s0):
    c = DN_CHUNK
    u_a = cm["u0"][:c] - _mm(cm["w"][:c], s0, 1, 1)
    yield
    s1 = cm["g0"] * s0 + _mm(u_a, cm["kl"][:c], 0, 0)
    yield
    u_b = cm["u0"][c:] - _mm(cm["w"][c:], s1, 1, 1)
    yield
    s2 = cm["g1"] * s1 + _mm(u_b, cm["kl"][c:], 0, 0)
    yield
    return u_a, s1, u_b, s2


def gdn_chunk_fwd(qkv, bg, bgt, proj, ogain, *, hd, name):
    n = qkv.shape[0]
    nb = n // LANE
    nh = hd // LANE
    c = DN_CHUNK

    def body(q_ref, k_ref, v_ref, bg_ref, bgt_ref, g_ref, gain_ref, o_ref, og_ref, hist_ref, s_ref):
        @pl.when(pl.program_id(0) == 0)
        def _():
            s_ref[...] = jnp.zeros_like(s_ref)

        hist_ref[0] = s_ref[...]

        def head(h):
            cols = slice(h * LANE, (h + 1) * LANE)
            cm = yield from _gdn_common(q_ref[:, cols], k_ref[:, cols], v_ref[:, cols], bg_ref[:, h:h + 1],
                                        bg_ref[:, nh + h:nh + h + 1], bgt_ref[nh + h:nh + h + 1, :])
            s0 = s_ref[h]
            u_a, s1, u_b, s2 = yield from _gdn_states(cm, s0)
            u_all = jnp.concatenate([u_a, u_b], axis=0)
            o = jnp.concatenate([_mm(cm["qg"][:c], s0, 1, 1), _mm(cm["qg"][c:], s1, 1, 1)], axis=0) + _mm(cm["pm"], u_all)
            s_ref[h] = s2
            o_ref[:, cols] = o
            rn = lax.rsqrt(jnp.mean(o * o, axis=-1, keepdims=True) + EPS)
            og_ref[:, cols] = (o * rn * gain_ref[...] * _silu(g_ref[:, cols])).astype(og_ref.dtype)

        _lockstep(head(h) for h in range(nh))

    col = lambda cc: pl.BlockSpec((LANE, hd), lambda b: (b, cc))
    return pl.pallas_call(
        body, name=name, grid=(nb,),
        in_specs=[col(0), col(1), col(2), pl.BlockSpec((LANE, LANE), lambda b: (b, 0)),
                  pl.BlockSpec((2 * nh, LANE), lambda b: (0, b)), pl.BlockSpec((LANE, hd), lambda b: (b, 3)),
                  pl.BlockSpec((1, LANE), lambda b: (0, 0))],
        out_specs=[col(0), col(0), pl.BlockSpec((1, nh, LANE, LANE), lambda b: (b, 0, 0, 0))],
        out_shape=[jax.ShapeDtypeStruct((n, hd), F32), jax.ShapeDtypeStruct((n, hd), BF16),
                   jax.ShapeDtypeStruct((nb, nh, LANE, LANE), F32)],
        scratch_shapes=[pltpu.VMEM((nh, LANE, LANE), F32)],
        compiler_params=_cparams("arbitrary"),
    )(qkv, qkv, qkv, bg, bgt, proj, ogain)


def gdn_chunk_bwd(qkv, bg, bgt, proj, ogain, o_raw, dog, hist, *, hd, name):
    n = qkv.shape[0]
    nb = n // LANE
    nh = hd // LANE
    c = DN_CHUNK

    def body(q_ref, k_ref, v_ref, bg_ref, bgt_ref, g_ref, gain_ref, o_ref, dog_ref, hist_ref,
             dq_ref, dk_ref, dv_ref, dgate_ref, dbg_ref, dgt_ref, dgain_ref, ds_ref):
        first = pl.program_id(0) == 0

        @pl.when(first)
        def _():
            ds_ref[...] = jnp.zeros_like(ds_ref)

        lane = _iota((LANE, LANE), 1)
        row = _iota((LANE, 1), 0)

        def head(h):
            cols = slice(h * LANE, (h + 1) * LANE)
            q, k, v = q_ref[:, cols], k_ref[:, cols], v_ref[:, cols]
            beta = bg_ref[:, h:h + 1]
            cm = yield from _gdn_common(q, k, v, beta, bg_ref[:, nh + h:nh + h + 1], bgt_ref[nh + h:nh + h + 1, :])
            s0 = hist_ref[0, h]
            u_a, s1, u_b, _ = yield from _gdn_states(cm, s0)
            u_all = jnp.concatenate([u_a, u_b], axis=0)
            o, gate, d_out, gain = o_ref[:, cols], g_ref[:, cols], dog_ref[:, cols], gain_ref[...]
            rn = lax.rsqrt(jnp.mean(o * o, axis=-1, keepdims=True) + EPS)
            xh = o * rn
            d_on = d_out * _silu(gate)
            dgate_ref[:, cols] = d_out * xh * gain * _silu_grad(gate)
            dgain = _colsum8(d_on * xh)
            gdy = d_on * gain
            d_o = rn * (gdy - xh * jnp.mean(xh * gdy, axis=-1, keepdims=True))
            pt_do = _mm(cm["pm"], d_o, 0, 0)
            ds_in = ds_ref[h]
            yield
            du_b = _mm(cm["kl"][c:], ds_in, 1, 1) + pt_do[c:]
            dkl_b = _mm(u_b, ds_in)
            dqg_b = _mm(d_o[c:], s1)
            dg1 = jnp.sum(jnp.sum(ds_in * s1, axis=1, keepdims=True), axis=0, keepdims=True)
            dw_b = -_mm(du_b, s1)
            yield
            ds_mid = cm["g1"] * ds_in + _mm(d_o[c:], cm["qg"][c:], 0, 0) - _mm(du_b, cm["w"][c:], 0, 0)
            yield
            du_a = _mm(cm["kl"][:c], ds_mid, 1, 1) + pt_do[:c]
            dkl_a = _mm(u_a, ds_mid)
            dqg_a = _mm(d_o[:c], s0)
            dg0 = jnp.sum(jnp.sum(ds_mid * s0, axis=1, keepdims=True), axis=0, keepdims=True)
            yield
            dw_a = -_mm(du_a, s0)
            ds_ref[h] = cm["g0"] * ds_mid + _mm(d_o[:c], cm["qg"][:c], 0, 0) - _mm(du_a, cm["w"][:c], 0, 0)
            du = jnp.concatenate([du_a, du_b], axis=0)
            dkl = jnp.concatenate([dkl_a, dkl_b], axis=0)
            dqg = jnp.concatenate([dqg_a, dqg_b], axis=0)
            dw = jnp.concatenate([dw_a, dw_b], axis=0)
            t, d, gam, lam = cm["t"], cm["d"], cm["gam"], cm["lam"]
            dp = jnp.where(cm["incl"], _mm(d_o, u_all, 1, 1), 0.0)
            dt = _mm(dw, cm["kb"], 1, 1) + _mm(du, cm["vb"], 1, 1)
            dkb = _mm(t, dw, 0, 0)
            dvb = _mm(t, du, 0, 0)
            yield
            x_t = _mm(t, dt, 0, 0)
            yield
            da = jnp.where(cm["strict"], -_mm(x_t, t, 1, 1), 0.0)
            yield
            kb_k = jnp.sum(dkb * k, axis=1, keepdims=True)
            dbeta = jnp.sum(da * cm["ahat"], axis=1, keepdims=True) + gam * kb_k + jnp.sum(dvb * v, axis=1, keepdims=True)
            dahat = da * beta
            dkk = dahat * d
            dqk = dp * d
            e = (dahat * cm["kk"] + dp * cm["qk"]) * d
            dk_ref[:, cols] = (_mm(dkk, k) + _mm(dkk, k, 0, 0) + _mm(dqk, q, 0, 0) + dkb * (beta * gam) + dkl * lam)
            dq_ref[:, cols] = _mm(dqk, k) + dqg * gam
            dv_ref[:, cols] = dvb * beta
            dgam = beta * kb_k + jnp.sum(dqg * q, axis=1, keepdims=True)
            dlam_lam = jnp.sum(dkl * k, axis=1, keepdims=True) * lam
            dgl0 = jnp.sum(jnp.where(row < c, dlam_lam, 0.0), axis=0, keepdims=True) + dg0 * cm["g0"]
            dgl1 = jnp.sum(jnp.where(row >= c, dlam_lam, 0.0), axis=0, keepdims=True) + dg1 * cm["g1"]
            dgc = (jnp.sum(e, axis=1, keepdims=True) + dgam * gam - dlam_lam
                   + jnp.where(row == c - 1, dgl0, 0.0) + jnp.where(row == LANE - 1, dgl1, 0.0))
            dgt_ref[h:h + 1, :] = -jnp.sum(e, axis=0, keepdims=True)
            return jnp.where(lane == h, dbeta, 0.0) + jnp.where(lane == nh + h, dgc, 0.0), dgain

        parts = _lockstep(head(h) for h in range(nh))
        dbg_ref[...] = sum(p[0] for p in parts)
        _accum(dgain_ref, sum(p[1] for p in parts), first)

    rev = lambda b: nb - 1 - b
    col = lambda cc: pl.BlockSpec((LANE, hd), lambda b: (rev(b), cc))
    return pl.pallas_call(
        body, name=name, grid=(nb,),
        in_specs=[col(0), col(1), col(2), pl.BlockSpec((LANE, LANE), lambda b: (rev(b), 0)),
                  pl.BlockSpec((2 * nh, LANE), lambda b: (0, rev(b))), pl.BlockSpec((LANE, hd), lambda b: (rev(b), 3)),
                  pl.BlockSpec((1, LANE), lambda b: (0, 0)), col(0), col(0),
                  pl.BlockSpec((1, nh, LANE, LANE), lambda b: (rev(b), 0, 0, 0))],
        out_specs=[col(0), col(0), col(0), col(0), pl.BlockSpec((LANE, LANE), lambda b: (rev(b), 0)),
                   pl.BlockSpec((nh, LANE), lambda b: (0, rev(b))), pl.BlockSpec((SUB, LANE), lambda b: (0, 0))],
        out_shape=[jax.ShapeDtypeStruct((n, hd), F32)] * 4 + [jax.ShapeDtypeStruct((n, LANE), F32),
                   jax.ShapeDtypeStruct((nh, n), F32), jax.ShapeDtypeStruct((SUB, LANE), F32)],
        scratch_shapes=[pltpu.VMEM((nh, LANE, LANE), F32)],
        compiler_params=_cparams("arbitrary"),
    )(qkv, qkv, qkv, bg, bgt, proj, ogain, o_raw, dog, hist)


def gdn_gates_bwd(proj, alog, dtb, dbg, *, hd, pad, name):
    n = proj.shape[0]
    gcol = 4 * (hd // LANE)

    def body(x_ref, a_ref, d_ref, dbg_ref, o_ref, da_ref, dd_ref):
        i = pl.program_id(0)
        x = x_ref[...]
        lane = _iota(x.shape, 1)
        rows = i * LANE + _iota((LANE, 1), 0)
        isg = (lane >= DN_HEADS) & (lane < 2 * DN_HEADS)
        dbgv = jnp.where(rows >= pad, dbg_ref[...], 0.0)
        dg = jnp.dot(_chunk_tri(True), jnp.where(isg, dbgv, 0.0), precision=HI, preferred_element_type=F32)
        ea = jnp.exp(a_ref[...])
        z = x + d_ref[...]
        dg = jnp.where(rows >= pad, dg, 0.0)
        dz = jnp.where(isg, dg * (-ea) * _sigmoid(z), 0.0)
        sb = _sigmoid(x)
        o_ref[...] = jnp.where(lane < DN_HEADS, dbgv * sb * (1.0 - sb), dz).astype(o_ref.dtype)
        _accum(da_ref, _colsum8(jnp.where(isg, dg * (-ea) * _softplus(z), 0.0)), i == 0)
        _accum(dd_ref, _colsum8(dz), i == 0)

    vec = pl.BlockSpec((1, LANE), lambda i: (0, 0))
    blk = pl.BlockSpec((LANE, LANE), lambda i: (i, 0))
    acc = pl.BlockSpec((SUB, LANE), lambda i: (0, 0))
    return pl.pallas_call(
        body, name=name, grid=(n // LANE,),
        in_specs=[pl.BlockSpec((LANE, LANE), lambda i: (i, gcol)), vec, vec, blk],
        out_specs=[blk, acc, acc],
        out_shape=[jax.ShapeDtypeStruct((n, LANE), BF16), jax.ShapeDtypeStruct((SUB, LANE), F32), jax.ShapeDtypeStruct((SUB, LANE), F32)],
        compiler_params=_cparams("arbitrary"),
    )(proj, alog, dtb, dbg)


def gdn_prep_bwd(proj, cw, dqkv, *, hd, name):
    n = proj.shape[0]
    tr = _tile(n, 640, SUB)
    nh = hd // LANE
    r8 = tr // SUB
    qscale = DN_HEAD_DIM ** -0.5

    def body(x_ref, p_ref, w_ref, dq_ref, dk_ref, dv_ref, o_ref, dw_ref):
        c, i = pl.program_id(0), pl.program_id(1)
        first = i == 0
        prev = jnp.where(first, 0.0, p_ref[...])
        cur = x_ref[...]
        cv = _conv_rows(cur, prev, w_ref[...])
        s = _silu(cv)
        r = lax.rsqrt(jnp.sum(s * s, axis=-1, keepdims=True) + EPS)
        y = s * r
        dy = jnp.where(c < nh, dq_ref[...] * qscale, dk_ref[...])
        ds_norm = r * (dy - y * jnp.sum(dy * y, axis=-1, keepdims=True))
        dcv = jnp.where(c < 2 * nh, ds_norm, dv_ref[...]) * _silu_grad(cv)
        o_ref[...] = dcv
        for k in range(DN_CONV):
            part = _colsum8(dcv * _shift_down(cur, prev, DN_CONV - 1 - k))

            @pl.when(first)
            def _():
                dw_ref[k] = part

            @pl.when(jnp.logical_not(first))
            def _():
                dw_ref[k] += part

    blk = lambda f: pl.BlockSpec((tr, LANE), f)
    return pl.pallas_call(
        body, name=name, grid=(3 * nh, n // tr),
        in_specs=[blk(lambda c, i: (i, c)), pl.BlockSpec((SUB, LANE), lambda c, i: (jnp.maximum(i * r8 - 1, 0), c)),
                  pl.BlockSpec((DN_CONV, LANE), lambda c, i: (0, c)),
                  blk(lambda c, i: (i, jnp.minimum(c, nh - 1))), blk(lambda c, i: (i, jnp.clip(c - nh, 0, nh - 1))),
                  blk(lambda c, i: (i, jnp.clip(c - 2 * nh, 0, nh - 1)))],
        out_specs=[blk(lambda c, i: (i, c)), pl.BlockSpec((DN_CONV, SUB, LANE), lambda c, i: (0, 0, c))],
        out_shape=[jax.ShapeDtypeStruct((n, 3 * hd), F32), jax.ShapeDtypeStruct((DN_CONV, SUB, 3 * hd), F32)],
        compiler_params=_cparams("parallel", "arbitrary"),
    )(proj, proj, cw, *dqkv)


def loss_head(h, target, *, x0, name):
    n, d = h.shape
    tr = LANE
    nb0 = x0 // tr

    def body(h_ref, t_ref, dh_ref, sq_ref):
        i = pl.program_id(0)
        live = i >= nb0
        err = jnp.where(live, h_ref[...] - t_ref[...], 0.0)
        dh_ref[...] = err * (1.0 / d)
        _accum(sq_ref, _colsum8(err * err), i == 0)

    row = pl.BlockSpec((tr, d), lambda i: (i, 0))
    return pl.pallas_call(
        body, name=name, grid=(n // tr,),
        in_specs=[row, pl.BlockSpec((tr, d), lambda i: (jnp.maximum(i - nb0, 0), 0))],
        out_specs=[row, pl.BlockSpec((SUB, d), lambda i: (0, 0))],
        out_shape=[jax.ShapeDtypeStruct((n, d), F32), jax.ShapeDtypeStruct((SUB, d), F32)],
        compiler_params=_cparams("arbitrary"),
    )(h, target)


def adamw(w, g, m, v, *, name):
    r, c = w.shape
    tr = _tile(r, 512, SUB) if r % SUB == 0 else r
    c1 = 1.0 / (1.0 - ADAM_B1 ** ADAM_STEP)
    c2 = 1.0 / (1.0 - ADAM_B2 ** ADAM_STEP)

    def body(w_ref, g_ref, m_ref, v_ref, d_ref, mo_ref, vo_ref):
        gv = g_ref[...]
        mn = ADAM_B1 * m_ref[...] + (1.0 - ADAM_B1) * gv
        vn = ADAM_B2 * v_ref[...] + (1.0 - ADAM_B2) * (gv * gv)
        d_ref[...] = -ADAM_LR * ((mn * c1) / (jnp.sqrt(vn * c2) + ADAM_EPS) + ADAM_WD * w_ref[...])
        mo_ref[...] = mn
        vo_ref[...] = vn

    blk = pl.BlockSpec((tr, c), lambda i: (i, 0))
    return pl.pallas_call(
        body, name=name, grid=(r // tr,), in_specs=[blk] * 4, out_specs=[blk] * 3,
        out_shape=[jax.ShapeDtypeStruct((r, c), F32)] * 3, compiler_params=_cparams("parallel"),
    )(w, g, m, v)


_BIG = ("attn_w_in", "attn_w_out", "dn_w_in", "dn_w_out", "ffn_w_up", "ffn_w_down")


def _row(v, width=None):
    v = v.astype(F32).reshape(1, -1)
    if width is not None and v.shape[1] < width:
        v = jnp.pad(v, ((0, 0), (0, width - v.shape[1])))
    return v


def _fold8(p):
    return jnp.sum(p, axis=-2)


def local_step(x, target, w):
    seq, d = x.shape
    pad = (-(N_META + seq)) % LANE
    x0 = pad + N_META
    n = x0 + seq
    depth = w["g_pre"].shape[0]
    hd_a = ATT_HEADS * ATT_HEAD_DIM
    hd_d = DN_HEADS * DN_HEAD_DIM
    t_att = _tile(n, 640)
    h = jnp.concatenate([jnp.zeros((pad, d), F32), w["meta"].astype(F32), x], axis=0)
    saved = []
    for i in range(depth):
        j = i // 2
        s = dict(h=h)
        s["a"] = rms_fwd(h, _row(w["g_pre"][i]), out_dtype=BF16, name="rms_pre")
        if i % 2 == 0:
            s["proj"] = proj = matmul(s["a"], w["attn_w_in"][j], name="mm_attn_in")
            qg, kg = _row(jnp.tile(w["attn_qg"][j], ATT_HEADS)), _row(jnp.tile(w["attn_kg"][j], ATT_HEADS))
            bf = _row(w["attn_b"][j], LANE)
            s["qs"], s["kn"], s["v"], logf = attn_prep(proj, qg, kg, bf, hd=hd_a, name="attn_prep")
            c = cumsum_rows(logf, reverse=False, name="cumsum_fwd")
            s["ct"] = c[:, :ATT_HEADS].T.reshape(ATT_HEADS // 2, 2, n)
            s["o"], s["og"], s["lse"] = flash_fwd(s["qs"], s["kn"], s["v"], proj, s["ct"], pad=pad, t=t_att, name="flash_fwd")
            s["m"] = matmul(s["og"], w["attn_w_out"][j], name="mm_attn_out")
        else:
            s["proj"] = proj = matmul(s["a"], w["dn_w_in"][j], name="mm_dn_in")
            s["qkv"] = gdn_prep(proj, w["dn_conv"][j], hd=hd_d, name="gdn_prep")
            alog = jnp.pad(_row(w["dn_alog"][j]), ((0, 0), (DN_HEADS, LANE - 2 * DN_HEADS)))
            dtb = jnp.pad(_row(w["dn_dtb"][j]), ((0, 0), (DN_HEADS, LANE - 2 * DN_HEADS)))
            s["bg"] = gdn_gates(proj, alog, dtb, hd=hd_d, name="gdn_gates")
            s["bgt"] = s["bg"][:, :2 * DN_HEADS].T
            s["o"], s["og"], s["hist"] = gdn_chunk_fwd(s["qkv"], s["bg"], s["bgt"], proj, _row(w["dn_og"][j]), hd=hd_d, name="gdn_fwd")
            s["m"] = matmul(s["og"], w["dn_w_out"][j], name="mm_dn_out")
        s["h_mid"] = rms_fwd(s["m"], _row(w["g_post"][i]), res=h, out_dtype=F32, name="rms_post")
        s["b"] = rms_fwd(s["h_mid"], _row(w["g_fpre"][i]), out_dtype=BF16, name="rms_fpre")
        s["u"] = matmul(s["b"], w["ffn_w_up"][i], name="mm_ffn_up")
        s["act"] = conv_glu_fwd(s["u"], w["ffn_conv"][i], name="ffn_glu")
        s["f"] = matmul(s["act"], w["ffn_w_down"][i], name="mm_ffn_down")
        h = rms_fwd(s["f"], _row(w["g_fpost"][i]), res=s["h_mid"], out_dtype=F32, name="rms_fpost")
        saved.append(s)

    dh, sq = loss_head(h, target, x0=x0, name="loss_head")
    loss = 0.5 * jnp.sum(sq) / d

    g = {k: [None] * depth for k in ("g_pre", "g_post", "g_fpre", "g_fpost", "ffn_w_up", "ffn_conv", "ffn_w_down")}
    for k in ("attn_w_in", "attn_b", "attn_qg", "attn_kg", "attn_w_out", "dn_w_in", "dn_conv", "dn_alog", "dn_dtb", "dn_og", "dn_w_out"):
        g[k] = [None] * (depth // 2)
    for i in reversed(range(depth)):
        j = i // 2
        s = saved[i]
        proj = s["proj"]
        df, p8 = rms_bwd(s["f"], _row(w["g_fpost"][i]), dh, pad=pad, name="rms_fpost_bwd")
        g["g_fpost"][i] = _fold8(p8)
        dact = matmul(df, w["ffn_w_down"][i], trans_b=True, out_dtype=BF16, name="mm_ffn_down_dx")
        g["ffn_w_down"][i] = matmul(s["act"], df, trans_a=True, out_dtype=BF16, name="mm_ffn_down_dw")
        duc, p8 = conv_glu_bwd(s["u"], w["ffn_conv"][i], dact, name="ffn_glu_bwd")
        g["ffn_conv"][i] = _fold8(p8)
        du = conv_transpose(duc, w["ffn_conv"][i], out_dtype=BF16, name="ffn_conv_t")
        db = matmul(du, w["ffn_w_up"][i], trans_b=True, name="mm_ffn_up_dx")
        g["ffn_w_up"][i] = matmul(s["b"], du, trans_a=True, out_dtype=BF16, name="mm_ffn_up_dw")
        dh_mid, p8 = rms_bwd(s["h_mid"], _row(w["g_fpre"][i]), db, res=dh, pad=pad, name="rms_fpre_bwd")
        g["g_fpre"][i] = _fold8(p8)
        dm, p8 = rms_bwd(s["m"], _row(w["g_post"][i]), dh_mid, pad=pad, name="rms_post_bwd")
        g["g_post"][i] = _fold8(p8)
        if i % 2 == 0:
            g["attn_w_out"][j] = matmul(s["og"], dm, trans_a=True, out_dtype=BF16, name="mm_attn_out_dw")
            dgated = matmul(dm, w["attn_w_out"][j], trans_b=True, name="mm_attn_out_dx")
            do, delta, dgate = attn_bwd_prep(dgated, s["o"], proj, hd=hd_a, name="attn_bwd_prep")
            dqs, dkn, dv, dck, dcr = flash_bwd(s["qs"], s["kn"], s["v"], do, s["lse"], delta, s["ct"], pad=pad, t=t_att, name="flash_bwd")
            dc = (dcr - dck)[:, ::ATT_HEAD_DIM].reshape(n, ATT_HEADS // 2, 2)[:, :, ::-1].reshape(n, ATT_HEADS)
            dc = jnp.pad(dc, ((0, 0), (0, LANE - ATT_HEADS)))
            dlogf = cumsum_rows(dc, reverse=True, name="cumsum_bwd")
            qg, kg = _row(jnp.tile(w["attn_qg"][j], ATT_HEADS)), _row(jnp.tile(w["attn_kg"][j], ATT_HEADS))
            bf = _row(w["attn_b"][j], LANE)
            dq_raw, dk_raw, dfl, gq8, gk8, gb8 = attn_in_bwd(dqs, dkn, proj, qg, kg, bf, dlogf, hd=hd_a, pad=pad, name="attn_in_bwd")
            g["attn_qg"][j] = _fold8(gq8).reshape(ATT_HEADS, ATT_HEAD_DIM).sum(axis=0)
            g["attn_kg"][j] = _fold8(gk8).reshape(ATT_HEADS, ATT_HEAD_DIM).sum(axis=0)
            g["attn_b"][j] = _fold8(gb8)[:ATT_HEADS]
            dproj = jnp.concatenate([dq_raw, dk_raw, dv.astype(BF16), dgate.astype(BF16), dfl], axis=1)
            w_in, key = w["attn_w_in"][j], "attn_w_in"
        else:
            g["dn_w_out"][j] = matmul(s["og"], dm, trans_a=True, out_dtype=BF16, name="mm_dn_out_dw")
            dgated = matmul(dm, w["dn_w_out"][j], trans_b=True, name="mm_dn_out_dx")
            alog = jnp.pad(_row(w["dn_alog"][j]), ((0, 0), (DN_HEADS, LANE - 2 * DN_HEADS)))
            dtb = jnp.pad(_row(w["dn_dtb"][j]), ((0, 0), (DN_HEADS, LANE - 2 * DN_HEADS)))
            dq, dk, dv, dgate, dbg, dgt, gain8 = gdn_chunk_bwd(s["qkv"], s["bg"], s["bgt"], proj, _row(w["dn_og"][j]), s["o"], dgated,
                                                              s["hist"], hd=hd_d, name="gdn_bwd")
            g["dn_og"][j] = _fold8(gain8)
            dbg = dbg + jnp.pad(dgt.T, ((0, 0), (DN_HEADS, LANE - 2 * DN_HEADS)))
            dgl, da8, dd8 = gdn_gates_bwd(proj, alog, dtb, dbg, hd=hd_d, pad=pad, name="gdn_gates_bwd")
            g["dn_alog"][j] = _fold8(da8)[DN_HEADS:2 * DN_HEADS]
            g["dn_dtb"][j] = _fold8(dd8)[DN_HEADS:2 * DN_HEADS]
            dcv, p8 = gdn_prep_bwd(proj, w["dn_conv"][j], (dq, dk, dv), hd=hd_d, name="gdn_prep_bwd")
            g["dn_conv"][j] = _fold8(p8)
            dqkv = conv_transpose(dcv, w["dn_conv"][j], out_dtype=BF16, name="gdn_conv_t")
            dproj = jnp.concatenate([dqkv, dgate.astype(BF16), dgl], axis=1)
            w_in, key = w["dn_w_in"][j], "dn_w_in"
        da = matmul(dproj, w_in, trans_b=True, name="mm_in_dx")
        g[key][j] = matmul(s["a"], dproj, trans_a=True, out_dtype=BF16, name="mm_in_dw")
        dh, p8 = rms_bwd(s["h"], _row(w["g_pre"][i]), da, res=dh_mid, pad=pad, name="rms_pre_bwd")
        g["g_pre"][i] = _fold8(p8)

    grads = {k: (v if k in _BIG else jnp.stack(v)) for k, v in g.items()}
    grads["meta"] = dh[pad:x0]
    return loss, dh[x0:], grads


_ANY = pl.BlockSpec(memory_space=pl.ANY)


def _mesh_place():
    x, y, c = lax.axis_index("x"), lax.axis_index("y"), lax.axis_index("c")
    return x, y, c, 4 * x + 2 * y + c


def _peer(x, y, c, k):
    px, py, pc = (1 - x if k & 4 else x), (1 - y if k & 2 else y), (1 - c if k & 1 else c)
    return (px, py, pc), 4 * px + 2 * py + pc


def _window_blocks(shard):
    return max(-(-(shard * (d + 1)) // LANE) - (shard * d) // LANE for d in range(N_DEV))


def _sds(shape, dtype):
    return jax.ShapeDtypeStruct(tuple(shape), dtype)


def _plan_gather(buf):
    return _sds((N_DEV,) + buf.shape, buf.dtype), (lambda r, i: r), (lambda o, i: o.at[i])


def _plan_scatter(buf):
    return _sds(buf.shape, buf.dtype), (lambda r, i: r.at[i]), (lambda o, i: o.at[i])


def _plan_gather_rows(buf):
    l, r, c = buf.shape
    return _sds((l, N_DEV * r, c), buf.dtype), (lambda ref, i: ref), (lambda o, i: o.at[:, pl.ds(pl.multiple_of(i * r, SUB), r), :])


def _plan_scatter_rows(buf):
    r, c = buf.shape[0] // N_DEV, buf.shape[1]
    return _sds((N_DEV, r, c), buf.dtype), (lambda ref, i: ref.at[pl.ds(pl.multiple_of(i * r, SUB), r), :]), (lambda o, i: o.at[i])


def _plan_scatter_cols(buf, shard):
    ww = _window_blocks(shard) * LANE
    src = lambda ref, i: ref.at[:, pl.ds(pl.multiple_of((shard * i) // LANE * LANE, LANE), ww)]
    return _sds((N_DEV, buf.shape[0], ww), buf.dtype), src, (lambda o, i: o.at[i])


def exchange(bufs, plans, *, name):
    nbuf = len(bufs)

    def body(*refs):
        ins, outs = refs[:nbuf], refs[nbuf:2 * nbuf]
        send_sems, recv_sems, loc_sems = refs[2 * nbuf:]
        x, y, c, me = _mesh_place()
        local = [pltpu.make_async_copy(plans[b][1](ins[b], me), plans[b][2](outs[b], me), loc_sems.at[b]) for b in range(nbuf)]
        for cp in local:
            cp.start()
        sends, recvs = [], []
        for k in range(1, N_DEV):
            peer, pidx = _peer(x, y, c, k)
            for b in range(nbuf):
                sems = dict(send_sem=send_sems.at[b, k - 1], recv_sem=recv_sems.at[b, k - 1], device_id=peer,
                            device_id_type=pl.DeviceIdType.MESH)
                cp = pltpu.make_async_remote_copy(src_ref=plans[b][1](ins[b], pidx), dst_ref=plans[b][2](outs[b], me), **sems)
                cp.start()
                sends.append(cp)
                recvs.append(pltpu.make_async_remote_copy(src_ref=plans[b][1](ins[b], pidx), dst_ref=plans[b][2](outs[b], pidx), **sems))
        for cp in recvs:
            cp.wait_recv()
        for cp in sends:
            cp.wait_send()
        for cp in local:
            cp.wait()

    return pl.pallas_call(
        body, name=name, in_specs=[_ANY] * nbuf, out_specs=[_ANY] * nbuf, out_shape=[p[0] for p in plans],
        scratch_shapes=[pltpu.SemaphoreType.DMA((nbuf, N_DEV - 1)), pltpu.SemaphoreType.DMA((nbuf, N_DEV - 1)),
                        pltpu.SemaphoreType.DMA((nbuf,))],
        compiler_params=pltpu.CompilerParams(has_side_effects=True),
    )(*bufs)


def slot_sum(x, *, name):
    _, r, c = x.shape
    tr = _tile(r, 512, 16)

    def body(x_ref, o_ref):
        acc = x_ref[0].astype(F32)
        for d in range(1, N_DEV):
            acc = acc + x_ref[d].astype(F32)
        o_ref[...] = acc

    return pl.pallas_call(
        body, name=name, grid=(r // tr,), in_specs=[pl.BlockSpec((N_DEV, tr, c), lambda i: (0, i, 0))],
        out_specs=pl.BlockSpec((tr, c), lambda i: (i, 0)), out_shape=jax.ShapeDtypeStruct((r, c), F32),
        compiler_params=_cparams("parallel"),
    )(x)


def assemble_cols(win, shard, *, name):
    _, r, ww = win.shape
    wb = ww // LANE
    nbo = -(-(N_DEV * shard) // LANE)
    tab = []
    for b in range(nbo):
        hits = [(d, b - (shard * d) // LANE) for d in range(N_DEV) if 0 <= b - (shard * d) // LANE < wb]
        assert 1 <= len(hits) <= 2, (b, hits)
        tab.append([hits[0][0], hits[0][1], hits[-1][0], hits[-1][1], len(hits) - 1])
    tab = jnp.array(tab, jnp.int32).T
    tr = _tile(r, 512, 16)

    def body(tab_ref, a_ref, b_ref, o_ref):
        two = tab_ref[4, pl.program_id(1)] > 0
        o_ref[...] = a_ref[0] + jnp.where(two, b_ref[0], jnp.zeros_like(b_ref[0]))

    grid_spec = pltpu.PrefetchScalarGridSpec(
        num_scalar_prefetch=1, grid=(r // tr, nbo),
        in_specs=[pl.BlockSpec((1, tr, LANE), lambda i, b, t: (t[0, b], i, t[1, b])),
                  pl.BlockSpec((1, tr, LANE), lambda i, b, t: (t[2, b], i, t[3, b]))],
        out_specs=pl.BlockSpec((tr, LANE), lambda i, b, t: (i, b)))
    return pl.pallas_call(body, name=name, grid_spec=grid_spec, out_shape=jax.ShapeDtypeStruct((r, nbo * LANE), win.dtype),
                          compiler_params=_cparams("parallel", "parallel"))(tab, win, win)


def _pack(parts, dtype, lead=()):
    nl = len(lead)
    flat = jnp.concatenate([p.astype(dtype).reshape(lead + (-1,)) for p in parts], axis=nl)
    tot = flat.shape[nl]
    rows = -(-tot // (16 * LANE)) * 16
    flat = jnp.pad(flat, [(0, 0)] * nl + [(0, rows * LANE - tot)])
    return flat.reshape(lead + (rows, LANE))


def _unpack(buf, shapes, lead=()):
    nl = len(lead)
    flat = buf.reshape(lead + (-1,))
    out, off = [], 0
    for shp in shapes:
        size = 1
        for s in shp:
            size *= s
        out.append(lax.slice_in_dim(flat, off, off + size, axis=nl).reshape(lead + tuple(shp)))
        off += size
    return out


def _whole(g8, axis):
    t = jnp.moveaxis(g8, 0, axis)
    shp = t.shape
    return t.reshape(shp[:axis] + (shp[axis] * shp[axis + 1],) + shp[axis + 2:])


def _slots(full, axis):
    shp = full.shape
    t = full.reshape(shp[:axis] + (N_DEV, shp[axis] // N_DEV) + shp[axis + 1:])
    return jnp.moveaxis(t, axis, 0)


_PARAMS = (("meta_tokens", 1), ("norm_mix_pre", None), ("norm_mix_post", None), ("norm_ffn_pre", None), ("norm_ffn_post", None),
           ("attn_w_in", 2), ("attn_b_forget", None), ("attn_q_norm", None), ("attn_k_norm", None), ("attn_w_out", 1), ("dn_w_in", 2),
           ("dn_conv", 2), ("dn_a_log", None), ("dn_dt_bias", None), ("dn_o_norm", None), ("dn_w_out", 1), ("ffn_w_up", 2),
           ("ffn_conv", 2), ("ffn_w_down", 1))
_LOCAL_KEY = dict(meta_tokens="meta", norm_mix_pre="g_pre", norm_mix_post="g_post", norm_ffn_pre="g_fpre", norm_ffn_post="g_fpost",
                  attn_w_in="attn_w_in", attn_b_forget="attn_b", attn_q_norm="attn_qg", attn_k_norm="attn_kg", attn_w_out="attn_w_out",
                  dn_w_in="dn_w_in", dn_conv="dn_conv", dn_a_log="dn_alog", dn_dt_bias="dn_dtb", dn_o_norm="dn_og", dn_w_out="dn_w_out",
                  ffn_w_up="ffn_w_up", ffn_conv="ffn_conv", ffn_w_down="ffn_w_down")
_COL_CUT = ("attn_w_in", "dn_w_in", "ffn_w_up")
_ROW_CUT = ("attn_w_out", "dn_w_out", "ffn_w_down")


def kernel(x, meta_tokens, norm_mix_pre, norm_mix_post, norm_ffn_pre, norm_ffn_post, attn_w_in, attn_b_forget, attn_q_norm, attn_k_norm, attn_w_out, dn_w_in, dn_conv, dn_a_log, dn_dt_bias, dn_o_norm, dn_w_out, ffn_w_up, ffn_conv, ffn_w_down, loss_target, m_meta_tokens, m_norm_mix_pre, m_norm_mix_post, m_norm_ffn_pre, m_norm_ffn_post, m_attn_w_in, m_attn_b_forget, m_attn_q_norm, m_attn_k_norm, m_attn_w_out, m_dn_w_in, m_dn_conv, m_dn_a_log, m_dn_dt_bias, m_dn_o_norm, m_dn_w_out, m_ffn_w_up, m_ffn_conv, m_ffn_w_down, v_meta_tokens, v_norm_mix_pre, v_norm_mix_post, v_norm_ffn_pre, v_norm_ffn_post, v_attn_w_in, v_attn_b_forget, v_attn_q_norm, v_attn_k_norm, v_attn_w_out, v_dn_w_in, v_dn_conv, v_dn_a_log, v_dn_dt_bias, v_dn_o_norm, v_dn_w_out, v_ffn_w_up, v_ffn_conv, v_ffn_w_down):
    given = dict(locals())
    names = [p[0] for p in _PARAMS]
    cut = [p for p in _PARAMS if p[1] is not None and p[0] not in _BIG]
    rep = [p for p in _PARAMS if p[1] is None]
    me = 4 * lax.axis_index("x") + 2 * lax.axis_index("y") + lax.axis_index("c")

    bufs, plans = [], []
    for n in _COL_CUT:
        layers, d, shard = given[n].shape
        win = jnp.zeros((layers * d, _window_blocks(shard) * LANE), BF16)
        win = lax.dynamic_update_slice(win, given[n].astype(BF16).reshape(layers * d, shard), (0, (shard * me) % LANE))
        bufs.append(win)
        plans.append(_plan_gather(win))
    for n in _ROW_CUT:
        bufs.append(given[n].astype(BF16))
        plans.append(_plan_gather_rows(bufs[-1]))
    bufs.append(_pack([given[n] for n, _ in cut], F32))
    plans.append(_plan_gather(bufs[-1]))
    got = exchange(bufs, plans, name="gather_weights")
    w = {}
    for n, g8 in zip(_COL_CUT, got[:3]):
        layers, d, shard = given[n].shape
        w[n] = assemble_cols(g8, shard, name="assemble_" + n).reshape(layers, d, -1)
    for n, full in zip(_ROW_CUT, got[3:6]):
        w[n] = full
    for (n, axis), g8 in zip(cut, _unpack(got[6], [given[n].shape for n, _ in cut], lead=(N_DEV,))):
        w[_LOCAL_KEY[n]] = _whole(g8, axis)
    for n, _ in rep:
        w[_LOCAL_KEY[n]] = given[n]

    loss, grad_x, g = local_step(x[0], loss_target[0], w)

    bufs, plans, what = [], [], []
    for n in _COL_CUT:
        for layer, gl in enumerate(g[n]):
            bufs.append(gl)
            plans.append(_plan_scatter_cols(gl, given[n].shape[2]))
            what.append((n, layer))
    for n in _ROW_CUT:
        for layer, gl in enumerate(g[n]):
            bufs.append(gl)
            plans.append(_plan_scatter_rows(gl))
            what.append((n, layer))
    nbig = len(bufs)
    bufs.append(_pack([_slots(g[_LOCAL_KEY[n]], axis) for n, axis in cut], F32, lead=(N_DEV,)))
    plans.append(_plan_scatter(bufs[-1]))
    bufs.append(_pack([g[_LOCAL_KEY[n]] for n, _ in rep] + [loss.reshape(1)], F32))
    plans.append(_plan_gather(bufs[-1]))
    got = exchange(bufs, plans, name="reduce_grads")
    per_layer = {n: [] for n in _BIG}
    for (n, layer), r8 in zip(what, got[:nbig]):
        tot = slot_sum(r8, name="sum_" + n)
        if n in _COL_CUT:
            shard = given[n].shape[2]
            tot = lax.dynamic_slice_in_dim(tot, (shard * me) % LANE, shard, axis=1)
        per_layer[n].append(tot)
    grads = {n: jnp.stack(v) for n, v in per_layer.items()}
    for (n, _), gv in zip(cut, _unpack(slot_sum(got[nbig], name="sum_cut"), [given[n].shape for n, _ in cut])):
        grads[n] = gv
    rep_sum = _unpack(slot_sum(got[nbig + 1], name="sum_rep"), [given[n].shape for n, _ in rep] + [(1,)])
    for (n, _), gv in zip(rep, rep_sum):
        grads[n] = gv
    loss_all = rep_sum[-1].reshape(())

    deltas, new_m, new_v = {}, {}, {}
    for n in names:
        shp = given[n].shape
        two_d = (-1, shp[-1])
        d, mn, vn = adamw(given[n].reshape(two_d), grads[n].reshape(two_d), given["m_" + n].reshape(two_d), given["v_" + n].reshape(two_d),
                          name="adamw_" + n)
        deltas[n], new_m[n], new_v[n] = d.reshape(shp), mn.reshape(shp), vn.reshape(shp)
    return (loss_all, grad_x[None], *[grads[n] for n in names], *[deltas[n] for n in names], *[new_m[n] for n in names],
            *[new_v[n] for n in names])
```

```python
import functools

import jax
import jax.numpy as jnp
from jax import lax
from jax.experimental import pallas as pl
from jax.experimental.pallas import tpu as pltpu

F32 = jnp.float32
BF16 = jnp.bfloat16
LANE = 128
SUB = 8
N_DEV = 8
N_META = 16
ATT_HEADS, ATT_HEAD_DIM = 16, 64
DN_HEADS, DN_HEAD_DIM, DN_CHUNK, DN_CONV = 8, 128, 64, 4
FFN_CONV = 3
EPS = 1e-6
NEG = -1e30
ADAM_LR, ADAM_B1, ADAM_B2, ADAM_EPS, ADAM_WD, ADAM_STEP = 0.001, 0.9, 0.999, 1e-08, 0.01, 10
HI = lax.Precision.HIGHEST
VMEM_LIMIT = 56 * 1024 * 1024


def _tile(n, target, align=LANE):
    if n <= target:
        return n
    best = None
    for t in range(align, target + 1, align):
        if n % t == 0:
            best = t
    assert best is not None, (n, target, align)
    return best


def _iota(shape, dim):
    return lax.broadcasted_iota(jnp.int32, shape, dim)


def _colsum8(x):
    r, c = x.shape
    return x.reshape(r // SUB, SUB, c).sum(axis=0)


def _cparams(*sem):
    return pltpu.CompilerParams(dimension_semantics=sem, vmem_limit_bytes=VMEM_LIMIT)


def _sigmoid(x):
    return 1.0 / (1.0 + jnp.exp(-x))


def _softplus(x):
    return jnp.maximum(x, 0.0) + jnp.log(1.0 + jnp.exp(-jnp.abs(x)))


def _accum(ref, part, first):
    @pl.when(first)
    def _():
        ref[...] = part

    @pl.when(jnp.logical_not(first))
    def _():
        ref[...] += part


def _lockstep(gens):
    gens = list(gens)
    out = [None] * len(gens)
    live = list(range(len(gens)))
    while live:
        nxt = []
        for i in live:
            try:
                next(gens[i])
                nxt.append(i)
            except StopIteration as stop:
                out[i] = stop.value
        live = nxt
    return out


def matmul(a, b, *, trans_a=False, trans_b=False, out_dtype=F32, tm=1664, tn=1408, tk=1664, name="matmul"):
    if trans_a:
        kdim, m = a.shape
    else:
        m, kdim = a.shape
    if trans_b:
        n, kb = b.shape
    else:
        kb, n = b.shape
    assert kb == kdim, (a.shape, b.shape, trans_a, trans_b)
    tm, tn, tk = _tile(m, tm), _tile(n, tn), _tile(kdim, tk)
    nk = kdim // tk
    dims = (((0 if trans_a else 1,), (1 if trans_b else 0,)), ((), ()))
    cdt = BF16

    def body(a_ref, b_ref, o_ref, *acc):
        part = lax.dot_general(a_ref[...].astype(cdt), b_ref[...].astype(cdt), dims, preferred_element_type=F32)
        if nk == 1:
            o_ref[...] = part.astype(o_ref.dtype)
            return
        k = pl.program_id(2)
        _accum(acc[0], part, k == 0)

        @pl.when(k == nk - 1)
        def _():
            o_ref[...] = acc[0][...].astype(o_ref.dtype)

    a_spec = pl.BlockSpec((tk, tm), lambda i, j, k: (k, i)) if trans_a else pl.BlockSpec((tm, tk), lambda i, j, k: (i, k))
    b_spec = pl.BlockSpec((tn, tk), lambda i, j, k: (j, k)) if trans_b else pl.BlockSpec((tk, tn), lambda i, j, k: (k, j))
    return pl.pallas_call(
        body,
        name=name,
        grid=(m // tm, n // tn, nk),
        in_specs=[a_spec, b_spec],
        out_specs=pl.BlockSpec((tm, tn), lambda i, j, k: (i, j)),
        out_shape=jax.ShapeDtypeStruct((m, n), out_dtype),
        scratch_shapes=[] if nk == 1 else [pltpu.VMEM((tm, tn), F32)],
        compiler_params=_cparams("parallel", "parallel", "arbitrary"),
    )(a, b)


def rms_fwd(x, g, *, res=None, out_dtype, name):
    n, d = x.shape
    tr = _tile(n, 640, SUB)

    def body(*refs):
        x_ref, g_ref = refs[0], refs[1]
        o_ref = refs[-1]
        xv = x_ref[...]
        y = xv * lax.rsqrt(jnp.mean(xv * xv, axis=-1, keepdims=True) + EPS) * g_ref[...]
        if res is not None:
            y = y + refs[2][...]
        o_ref[...] = y.astype(o_ref.dtype)

    row = pl.BlockSpec((tr, d), lambda i: (i, 0))
    ins = [x, g] + ([res] if res is not None else [])
    return pl.pallas_call(
        body, name=name, grid=(n // tr,),
        in_specs=[row, pl.BlockSpec((1, d), lambda i: (0, 0))] + ([row] if res is not None else []),
        out_specs=row, out_shape=jax.ShapeDtypeStruct((n, d), out_dtype),
        compiler_params=_cparams("parallel"),
    )(*ins)


def rms_bwd(x, g, dy, *, res=None, pad, name):
    n, d = x.shape
    tr = _tile(n, 640, SUB)

    def body(*refs):
        x_ref, g_ref, dy_ref = refs[:3]
        dx_ref, dg_ref = refs[-2:]
        i = pl.program_id(0)
        xv = x_ref[...]
        r = lax.rsqrt(jnp.mean(xv * xv, axis=-1, keepdims=True) + EPS)
        xh = xv * r
        dyv = dy_ref[...].astype(F32)
        gdy = dyv * g_ref[...]
        dx = r * (gdy - xh * jnp.mean(xh * gdy, axis=-1, keepdims=True))
        if res is not None:
            dx = dx + refs[3][...]
        rows = i * tr + _iota((tr, 1), 0)
        dx_ref[...] = jnp.where(rows >= pad, dx, 0.0)
        _accum(dg_ref, _colsum8(dyv * xh), i == 0)

    row = pl.BlockSpec((tr, d), lambda i: (i, 0))
    ins = [x, g, dy] + ([res] if res is not None else [])
    return pl.pallas_call(
        body, name=name, grid=(n // tr,),
        in_specs=[row, pl.BlockSpec((1, d), lambda i: (0, 0)), row] + ([row] if res is not None else []),
        out_specs=[row, pl.BlockSpec((SUB, d), lambda i: (0, 0))],
        out_shape=[jax.ShapeDtypeStruct((n, d), F32), jax.ShapeDtypeStruct((SUB, d), F32)],
        compiler_params=_cparams("arbitrary"),
    )(*ins)


def _halo_rows(dtype):
    return SUB * 4 // jnp.dtype(dtype).itemsize


def _shift_down(cur, prev, s):
    if s == 0:
        return cur
    hb = prev.shape[0]
    out = pltpu.roll(cur, s, 0)
    row = _iota(cur.shape, 0)
    for r in range(s):
        out = jnp.where(row == r, prev[hb - s + r:hb - s + r + 1, :], out)
    return out


def _shift_up(cur, next8, s):
    if s == 0:
        return cur
    tr = cur.shape[0]
    out = pltpu.roll(cur, tr - s, 0)
    row = _iota(cur.shape, 0)
    for r in range(s):
        out = jnp.where(row == tr - s + r, next8[r:r + 1, :], out)
    return out


def _conv_rows(cur, prev8, w):
    kw = w.shape[0]
    acc = w[kw - 1:kw, :] * cur
    for k in range(kw - 1):
        acc = acc + w[k:k + 1, :] * _shift_down(cur, prev8, kw - 1 - k)
    return acc


def conv_transpose(dy, w, *, out_dtype, name):
    n, c = dy.shape
    kw = w.shape[0]
    tr = _tile(n, 640, SUB)
    tc = _tile(c, 512)
    hb = _halo_rows(dy.dtype)
    nbh = n // hb

    def body(dy_ref, nx_ref, w_ref, o_ref):
        i = pl.program_id(0)
        cur = dy_ref[...].astype(F32)
        nxt = jnp.where(i == pl.num_programs(0) - 1, 0.0, nx_ref[...].astype(F32))
        wv = w_ref[...]
        acc = wv[kw - 1:kw, :] * cur
        for k in range(kw - 1):
            acc = acc + wv[k:k + 1, :] * _shift_up(cur, nxt, kw - 1 - k)
        o_ref[...] = acc.astype(o_ref.dtype)

    return pl.pallas_call(
        body, name=name, grid=(n // tr, c // tc),
        in_specs=[pl.BlockSpec((tr, tc), lambda i, j: (i, j)),
                  pl.BlockSpec((hb, tc), lambda i, j: (jnp.minimum((i + 1) * (tr // hb), nbh - 1), j)),
                  pl.BlockSpec((kw, tc), lambda i, j: (0, j))],
        out_specs=pl.BlockSpec((tr, tc), lambda i, j: (i, j)),
        out_shape=jax.ShapeDtypeStruct((n, c), out_dtype),
        compiler_params=_cparams("parallel", "parallel"),
    )(dy, dy, w)


_GELU_C = 0.7978845608028654
_GELU_A = 0.044715


def _gelu(x):
    return 0.5 * x * (1.0 + jnp.tanh(_GELU_C * (x + _GELU_A * x * x * x)))


def _gelu_grad(x):
    th = jnp.tanh(_GELU_C * (x + _GELU_A * x * x * x))
    return 0.5 * (1.0 + th) + 0.5 * x * (1.0 - th * th) * _GELU_C * (1.0 + 3.0 * _GELU_A * x * x)


def conv_glu_fwd(u, cw, *, name):
    n, f2 = u.shape
    f = f2 // 2
    tr = _tile(n, 640, SUB)
    tc = _tile(f, 512)
    nf = f // tc
    hb = _halo_rows(u.dtype)
    r8 = tr // hb

    def body(ug_ref, uu_ref, pg_ref, pu_ref, wg_ref, wu_ref, o_ref):
        first = pl.program_id(0) == 0
        pg = jnp.where(first, 0.0, pg_ref[...].astype(F32))
        pu = jnp.where(first, 0.0, pu_ref[...].astype(F32))
        gate = _conv_rows(ug_ref[...].astype(F32), pg, wg_ref[...])
        up = _conv_rows(uu_ref[...].astype(F32), pu, wu_ref[...])
        o_ref[...] = (_gelu(gate) * up).astype(o_ref.dtype)

    prev = lambda off: pl.BlockSpec((hb, tc), lambda i, j: (jnp.maximum(i * r8 - 1, 0), j + off))
    return pl.pallas_call(
        body, name=name, grid=(n // tr, nf),
        in_specs=[pl.BlockSpec((tr, tc), lambda i, j: (i, j)), pl.BlockSpec((tr, tc), lambda i, j: (i, j + nf)),
                  prev(0), prev(nf),
                  pl.BlockSpec((FFN_CONV, tc), lambda i, j: (0, j)), pl.BlockSpec((FFN_CONV, tc), lambda i, j: (0, j + nf))],
        out_specs=pl.BlockSpec((tr, tc), lambda i, j: (i, j)),
        out_shape=jax.ShapeDtypeStruct((n, f), BF16),
        compiler_params=_cparams("parallel", "parallel"),
    )(u, u, u, u, cw, cw)


def conv_glu_bwd(u, cw, dact, *, name):
    n, f2 = u.shape
    f = f2 // 2
    tr = _tile(n, 640, SUB)
    tc = _tile(f, 512)
    nf = f // tc
    hb = _halo_rows(u.dtype)
    r8 = tr // hb

    def body(us_ref, up_ref, ps_ref, pp_ref, ws_ref, wp_ref, da_ref, o_ref, dw_ref):
        j, i = pl.program_id(0), pl.program_id(1)
        first = i == 0
        ps = jnp.where(first, 0.0, ps_ref[...].astype(F32))
        pp = jnp.where(first, 0.0, pp_ref[...].astype(F32))
        cur = us_ref[...].astype(F32)
        mine = _conv_rows(cur, ps, ws_ref[...])
        other = _conv_rows(up_ref[...].astype(F32), pp, wp_ref[...])
        da = da_ref[...].astype(F32)

        def finish(d_mine):
            o_ref[...] = d_mine.astype(o_ref.dtype)
            for k in range(FFN_CONV):
                part = _colsum8(d_mine * _shift_down(cur, ps, FFN_CONV - 1 - k))

                @pl.when(first)
                def _():
                    dw_ref[k] = part

                @pl.when(jnp.logical_not(first))
                def _():
                    dw_ref[k] += part

        @pl.when(j < nf)
        def _():
            finish(da * other * _gelu_grad(mine))

        @pl.when(j >= nf)
        def _():
            finish(da * _gelu(other))

    part_of = lambda j: (j + nf) % (2 * nf)
    return pl.pallas_call(
        body, name=name, grid=(2 * nf, n // tr),
        in_specs=[pl.BlockSpec((tr, tc), lambda j, i: (i, j)), pl.BlockSpec((tr, tc), lambda j, i: (i, part_of(j))),
                  pl.BlockSpec((hb, tc), lambda j, i: (jnp.maximum(i * r8 - 1, 0), j)),
                  pl.BlockSpec((hb, tc), lambda j, i: (jnp.maximum(i * r8 - 1, 0), part_of(j))),
                  pl.BlockSpec((FFN_CONV, tc), lambda j, i: (0, j)), pl.BlockSpec((FFN_CONV, tc), lambda j, i: (0, part_of(j))),
                  pl.BlockSpec((tr, tc), lambda j, i: (i, j % nf))],
        out_specs=[pl.BlockSpec((tr, tc), lambda j, i: (i, j)), pl.BlockSpec((FFN_CONV, SUB, tc), lambda j, i: (0, 0, j))],
        out_shape=[jax.ShapeDtypeStruct((n, f2), BF16), jax.ShapeDtypeStruct((FFN_CONV, SUB, f2), F32)],
        compiler_params=_cparams("parallel", "arbitrary"),
    )(u, u, u, u, cw, cw, dact)


def cumsum_rows(x, *, reverse, name):
    n, c = x.shape
    tr = LANE
    nb = n // tr

    def body(x_ref, o_ref, carry_ref):
        i = pl.program_id(0)

        @pl.when(i == 0)
        def _():
            carry_ref[...] = jnp.zeros_like(carry_ref)

        r, cc = _iota((tr, tr), 0), _iota((tr, tr), 1)
        tri = jnp.where((cc >= r) if reverse else (cc <= r), 1.0, 0.0).astype(F32)
        out = jnp.dot(tri, x_ref[...], precision=HI, preferred_element_type=F32) + carry_ref[...]
        o_ref[...] = out
        carry_ref[...] = out[0:1, :] if reverse else out[tr - 1:tr, :]

    idx = (lambda i: (nb - 1 - i, 0)) if reverse else (lambda i: (i, 0))
    return pl.pallas_call(
        body, name=name, grid=(nb,),
        in_specs=[pl.BlockSpec((tr, c), idx)], out_specs=pl.BlockSpec((tr, c), idx),
        out_shape=jax.ShapeDtypeStruct((n, c), F32), scratch_shapes=[pltpu.VMEM((1, c), F32)],
        compiler_params=_cparams("arbitrary"),
    )(x)


def _group_sum64(x):
    r, c = x.shape
    a, b = _iota((LANE, LANE), 0), _iota((LANE, LANE), 1)
    bd = jnp.where((a // 64) == (b // 64), 1.0, 0.0).astype(F32)
    parts = [jnp.dot(x[:, k * LANE:(k + 1) * LANE], bd, precision=HI, preferred_element_type=F32) for k in range(c // LANE)]
    return parts[0] if len(parts) == 1 else jnp.concatenate(parts, axis=1)


def attn_prep(proj, qg, kg, bf, *, hd, name):
    n = proj.shape[0]
    tr = _tile(n, 640, SUB)
    scale = ATT_HEAD_DIM ** -0.5
    nh = hd // LANE

    def body(q_ref, k_ref, v_ref, f_ref, qg_ref, kg_ref, bf_ref, qo_ref, ko_ref, vo_ref, lf_ref):
        def norm(x, g):
            ms = _group_sum64(x * x) * (1.0 / ATT_HEAD_DIM)
            return x * lax.rsqrt(ms + EPS) * g

        qo_ref[...] = (norm(q_ref[...], qg_ref[...]) * scale).astype(qo_ref.dtype)
        ko_ref[...] = norm(k_ref[...], kg_ref[...]).astype(ko_ref.dtype)
        vo_ref[...] = v_ref[...].astype(vo_ref.dtype)
        lf_ref[...] = -_softplus(-(f_ref[...] + bf_ref[...]))

    col = lambda c: pl.BlockSpec((tr, hd), lambda i: (i, c))
    vec = lambda w: pl.BlockSpec((1, w), lambda i: (0, 0))
    return pl.pallas_call(
        body, name=name, grid=(n // tr,),
        in_specs=[col(0), col(1), col(2), pl.BlockSpec((tr, LANE), lambda i: (i, 4 * nh)), vec(hd), vec(hd), vec(LANE)],
        out_specs=[col(0), col(0), col(0), pl.BlockSpec((tr, LANE), lambda i: (i, 0))],
        out_shape=[jax.ShapeDtypeStruct((n, hd), BF16)] * 3 + [jax.ShapeDtypeStruct((n, LANE), F32)],
        compiler_params=_cparams("parallel"),
    )(proj, proj, proj, proj, qg, kg, bf)


def _half_mask(shape):
    return _iota(shape, 1) < ATT_HEAD_DIM


def flash_fwd(qs, kn, v, proj, ct, *, pad, t, name):
    n, hd = qs.shape
    npair = hd // LANE
    nb = n // t
    gate0 = 3 * npair

    def body(q_ref, k_ref, v_ref, g_ref, c_ref, o_ref, og_ref, lse_ref, m_ref, acc_ref):
        i, j = pl.program_id(1), pl.program_id(2)

        @pl.when(j == 0)
        def _():
            m_ref[...] = jnp.full_like(m_ref, NEG)
            acc_ref[...] = jnp.zeros_like(acc_ref)

        def step(masked):
            q, k, vv = q_ref[...], k_ref[...], v_ref[...]
            half0 = _half_mask(q.shape)
            if masked:
                rowpos = i * t + _iota((t, t), 0)
                colpos = j * t + _iota((t, t), 1)
                mask = (colpos <= rowpos) & (colpos >= pad)

            def head(hh):
                sel = half0 if hh == 0 else jnp.logical_not(half0)
                qm = jnp.where(sel, q, jnp.zeros_like(q))
                v1 = jnp.where(sel, vv, jnp.ones_like(vv))
                s = lax.dot_general(qm, k, (((1,), (1,)), ((), ())), preferred_element_type=F32) - c_ref[0, hh:hh + 1, :]
                yield
                if masked:
                    s = jnp.where(mask, s, NEG)
                m_prev = m_ref[hh]
                m_new = jnp.maximum(m_prev, jnp.max(s, axis=1, keepdims=True))
                p = jnp.exp(s - m_new[:, 0:1])
                if masked:
                    p = jnp.where(mask, p, 0.0)
                yield
                acc_ref[hh] = jnp.exp(m_prev - m_new) * acc_ref[hh] + jnp.dot(p.astype(vv.dtype), v1, preferred_element_type=F32)
                m_ref[hh] = m_new

            _lockstep(head(hh) for hh in range(2))

        edge = (j == i) | (j == 0)

        @pl.when(edge & (j <= i))
        def _():
            step(True)

        @pl.when(jnp.logical_not(edge) & (j < i))
        def _():
            step(False)

        @pl.when(j == i)
        def _():
            half0 = _half_mask((t, LANE))
            a0, a1 = acc_ref[0], acc_ref[1]
            l = jnp.where(half0, a0[:, ATT_HEAD_DIM:ATT_HEAD_DIM + 1], a1[:, 0:1])
            acc = jnp.where(half0, a0, a1)
            m = jnp.where(half0, m_ref[0], m_ref[1])
            live = l > 0.0
            o = jnp.where(live, acc / jnp.where(live, l, 1.0), 0.0)
            o_ref[...] = o
            og_ref[...] = (o * _sigmoid(g_ref[...])).astype(og_ref.dtype)
            lse_ref[...] = jnp.where(live, m + jnp.log(jnp.where(live, l, 1.0)), 0.0)

    qspec = pl.BlockSpec((t, LANE), lambda p, i, j: (i, p))
    kspec = pl.BlockSpec((t, LANE), lambda p, i, j: (jnp.minimum(j, i), p))
    return pl.pallas_call(
        body, name=name, grid=(npair, nb, nb),
        in_specs=[qspec, kspec, kspec, pl.BlockSpec((t, LANE), lambda p, i, j: (i, gate0 + p)),
                  pl.BlockSpec((1, 2, t), lambda p, i, j: (p, 0, jnp.minimum(j, i)))],
        out_specs=[qspec, qspec, qspec],
        out_shape=[jax.ShapeDtypeStruct((n, hd), F32), jax.ShapeDtypeStruct((n, hd), BF16), jax.ShapeDtypeStruct((n, hd), F32)],
        scratch_shapes=[pltpu.VMEM((2, t, LANE), F32)] * 2,
        compiler_params=_cparams("parallel", "parallel", "arbitrary"),
    )(qs, kn, v, proj, ct)


def attn_bwd_prep(dgated, o, proj, *, hd, name):
    n = o.shape[0]
    tr = _tile(n, 640, SUB)
    gate0 = 3

    def body(dg_ref, o_ref, g_ref, do_ref, dl_ref, dgate_ref):
        dg, ov = dg_ref[...], o_ref[...]
        sg = _sigmoid(g_ref[...])
        do = dg * sg
        do_ref[...] = do.astype(do_ref.dtype)
        dl_ref[...] = _group_sum64(do * ov)
        dgate_ref[...] = dg * ov * sg * (1.0 - sg)

    row = pl.BlockSpec((tr, hd), lambda i: (i, 0))
    return pl.pallas_call(
        body, name=name, grid=(n // tr,),
        in_specs=[row, row, pl.BlockSpec((tr, hd), lambda i: (i, gate0))],
        out_specs=[row, row, row],
        out_shape=[jax.ShapeDtypeStruct((n, hd), BF16), jax.ShapeDtypeStruct((n, hd), F32), jax.ShapeDtypeStruct((n, hd), F32)],
        compiler_params=_cparams("parallel"),
    )(dgated, o, proj)


def flash_bwd(qs, kn, v, do, lse, delta, ct, *, pad, t, name):
    n, hd = qs.shape
    npair = hd // LANE
    nb = n // t

    def body(q_ref, k_ref, v_ref, do_ref, lse_ref, dl_ref, c_ref, dq_ref, dk_ref, dv_ref, dck_ref, dcr_ref, dk_acc, dv_acc, dck_acc):
        j, i = pl.program_id(1), pl.program_id(2)

        @pl.when((j == 0) & (i == 0))
        def _():
            dq_ref[...] = jnp.zeros_like(dq_ref)
            dcr_ref[...] = jnp.zeros_like(dcr_ref)

        @pl.when(i == j)
        def _():
            dk_acc[...] = jnp.zeros_like(dk_acc)
            dv_acc[...] = jnp.zeros_like(dv_acc)
            dck_acc[...] = jnp.zeros_like(dck_acc)

        def step(masked):
            q, k, vv, dov = q_ref[...], k_ref[...], v_ref[...], do_ref[...]
            half0 = _half_mask(q.shape)
            if masked:
                rowpos = i * t + _iota((t, t), 0)
                colpos = j * t + _iota((t, t), 1)
                mask = (colpos <= rowpos) & (colpos >= pad)
            nt = (((1,), (1,)), ((), ()))
            tn = (((0,), (0,)), ((), ()))
            dq_h, dk_h, dv_h = [], [], []
            for hh in range(2):
                sel = half0 if hh == 0 else jnp.logical_not(half0)
                qm = jnp.where(sel, q, jnp.zeros_like(q))
                dom = jnp.where(sel, dov, jnp.zeros_like(dov))
                q1 = jnp.where(sel, q, jnp.ones_like(q))
                k1 = jnp.where(sel, k, jnp.ones_like(k))
                x = lax.dot_general(qm, k, nt, preferred_element_type=F32) - c_ref[0, hh:hh + 1, :] - lse_ref[:, hh * 64:hh * 64 + 1]
                if masked:
                    p = jnp.where(mask, jnp.exp(jnp.where(mask, x, NEG)), 0.0)
                else:
                    p = jnp.exp(x)
                dp = lax.dot_general(dom, vv, nt, preferred_element_type=F32)
                ds = p * (dp - dl_ref[:, hh * 64:hh * 64 + 1])
                dsb, pb = ds.astype(k.dtype), p.astype(k.dtype)
                dq_h.append(jnp.dot(dsb, k1, preferred_element_type=F32))
                dk_h.append(lax.dot_general(dsb, q1, tn, preferred_element_type=F32))
                dv_h.append(lax.dot_general(pb, dov, tn, preferred_element_type=F32))
            rows = pl.ds(pl.multiple_of(i * t, t), t)
            dq_ref[rows, :] += jnp.where(half0, dq_h[0], dq_h[1])
            dcr_ref[rows, :] += jnp.where(half0, dq_h[1], dq_h[0])
            dk_acc[...] += jnp.where(half0, dk_h[0], dk_h[1])
            dck_acc[...] += jnp.where(half0, dk_h[1], dk_h[0])
            dv_acc[...] += jnp.where(half0, dv_h[0], dv_h[1])

        edge = (i == j) | (j == 0)

        @pl.when(edge & (i >= j))
        def _():
            step(True)

        @pl.when(jnp.logical_not(edge) & (i > j))
        def _():
            step(False)

        @pl.when(i == nb - 1)
        def _():
            dk_ref[...] = dk_acc[...]
            dv_ref[...] = dv_acc[...]
            dck_ref[...] = dck_acc[...]

    qspec = pl.BlockSpec((t, LANE), lambda p, j, i: (jnp.maximum(i, j), p))
    kspec = pl.BlockSpec((t, LANE), lambda p, j, i: (j, p))
    cspec = pl.BlockSpec((1, 2, t), lambda p, j, i: (p, 0, j))
    whole = pl.BlockSpec((n, LANE), lambda p, j, i: (0, p))
    return pl.pallas_call(
        body, name=name, grid=(npair, nb, nb),
        in_specs=[qspec, kspec, kspec, qspec, qspec, qspec, cspec],
        out_specs=[whole, kspec, kspec, kspec, whole],
        out_shape=[jax.ShapeDtypeStruct((n, hd), F32)] * 5,
        scratch_shapes=[pltpu.VMEM((t, LANE), F32)] * 3,
        compiler_params=_cparams("parallel", "arbitrary", "arbitrary"),
    )(qs, kn, v, do, lse, delta, ct)


def attn_in_bwd(dqs, dkn, proj, qg, kg, bf, dlogf, *, hd, pad, name):
    n = proj.shape[0]
    tr = _tile(n, 640, SUB)
    scale = ATT_HEAD_DIM ** -0.5
    nh = hd // LANE

    def body(dq_ref, dk_ref, q_ref, k_ref, f_ref, qg_ref, kg_ref, bf_ref, dl_ref, oq_ref, ok_ref, of_ref, gq_ref, gk_ref, gb_ref):
        i = pl.program_id(0)

        def back(x, g, dy):
            r = lax.rsqrt(_group_sum64(x * x) * (1.0 / ATT_HEAD_DIM) + EPS)
            xh = x * r
            gdy = dy * g
            dx = r * (gdy - xh * _group_sum64(xh * gdy) * (1.0 / ATT_HEAD_DIM))
            return dx, _colsum8(dy * xh)

        dxq, gq = back(q_ref[...], qg_ref[...], dq_ref[...] * scale)
        dxk, gk = back(k_ref[...], kg_ref[...], dk_ref[...])
        oq_ref[...] = dxq.astype(oq_ref.dtype)
        ok_ref[...] = dxk.astype(ok_ref.dtype)
        rows = i * tr + _iota((tr, 1), 0)
        dfl = jnp.where(rows >= pad, dl_ref[...] * _sigmoid(-(f_ref[...] + bf_ref[...])), 0.0)
        of_ref[...] = dfl.astype(of_ref.dtype)
        _accum(gq_ref, gq, i == 0)
        _accum(gk_ref, gk, i == 0)
        _accum(gb_ref, _colsum8(dfl), i == 0)

    row = pl.BlockSpec((tr, hd), lambda i: (i, 0))
    col = lambda c: pl.BlockSpec((tr, hd), lambda i: (i, c))
    nar = pl.BlockSpec((tr, LANE), lambda i: (i, 0))
    vec = lambda w: pl.BlockSpec((1, w), lambda i: (0, 0))
    acc = lambda w: pl.BlockSpec((SUB, w), lambda i: (0, 0))
    return pl.pallas_call(
        body, name=name, grid=(n // tr,),
        in_specs=[row, row, col(0), col(1), pl.BlockSpec((tr, LANE), lambda i: (i, 4 * nh)), vec(hd), vec(hd), vec(LANE), nar],
        out_specs=[row, row, nar, acc(hd), acc(hd), acc(LANE)],
        out_shape=[jax.ShapeDtypeStruct((n, hd), BF16)] * 2 + [jax.ShapeDtypeStruct((n, LANE), BF16),
                   jax.ShapeDtypeStruct((SUB, hd), F32), jax.ShapeDtypeStruct((SUB, hd), F32), jax.ShapeDtypeStruct((SUB, LANE), F32)],
        compiler_params=_cparams("arbitrary"),
    )(dqs, dkn, proj, proj, proj, qg, kg, bf, dlogf)


def _silu(x):
    return x * _sigmoid(x)


def _silu_grad(x):
    s = _sigmoid(x)
    return s * (1.0 + x * (1.0 - s))


def gdn_prep(proj, cw, *, hd, name):
    n = proj.shape[0]
    tr = _tile(n, 640, SUB)
    nh = hd // LANE
    r8 = tr // SUB
    qscale = DN_HEAD_DIM ** -0.5

    def body(x_ref, p_ref, w_ref, o_ref):
        i, c = pl.program_id(0), pl.program_id(1)
        prev = jnp.where(i == 0, 0.0, p_ref[...])
        s = _silu(_conv_rows(x_ref[...], prev, w_ref[...]))
        r = lax.rsqrt(jnp.sum(s * s, axis=-1, keepdims=True) + EPS)
        mult = jnp.where(c < nh, r * qscale, jnp.where(c < 2 * nh, r, 1.0))
        o_ref[...] = s * mult

    return pl.pallas_call(
        body, name=name, grid=(n // tr, 3 * nh),
        in_specs=[pl.BlockSpec((tr, LANE), lambda i, c: (i, c)),
                  pl.BlockSpec((SUB, LANE), lambda i, c: (jnp.maximum(i * r8 - 1, 0), c)),
                  pl.BlockSpec((DN_CONV, LANE), lambda i, c: (0, c))],
        out_specs=pl.BlockSpec((tr, LANE), lambda i, c: (i, c)),
        out_shape=jax.ShapeDtypeStruct((n, 3 * hd), F32),
        compiler_params=_cparams("parallel", "parallel"),
    )(proj, proj, cw)


def _chunk_tri(reverse):
    r, c = _iota((LANE, LANE), 0), _iota((LANE, LANE), 1)
    same = (r // DN_CHUNK) == (c // DN_CHUNK)
    return jnp.where(same & ((c >= r) if reverse else (c <= r)), 1.0, 0.0).astype(F32)


def gdn_gates(proj, alog, dtb, *, hd, name):
    n = proj.shape[0]
    gcol = 4 * (hd // LANE)

    def body(x_ref, a_ref, d_ref, o_ref):
        x = x_ref[...]
        lane = _iota(x.shape, 1)
        g = -jnp.exp(a_ref[...]) * _softplus(x + d_ref[...])
        gc = jnp.dot(_chunk_tri(False), jnp.where((lane >= DN_HEADS) & (lane < 2 * DN_HEADS), g, 0.0), precision=HI,
                     preferred_element_type=F32)
        o_ref[...] = jnp.where(lane < DN_HEADS, _sigmoid(x), gc)

    vec = pl.BlockSpec((1, LANE), lambda i: (0, 0))
    return pl.pallas_call(
        body, name=name, grid=(n // LANE,),
        in_specs=[pl.BlockSpec((LANE, LANE), lambda i: (i, gcol)), vec, vec],
        out_specs=pl.BlockSpec((LANE, LANE), lambda i: (i, 0)),
        out_shape=jax.ShapeDtypeStruct((n, LANE), F32),
        compiler_params=_cparams("parallel"),
    )(proj, alog, dtb)


def _mm(a, b, ca=1, cb=0):
    return lax.dot_general(a.astype(BF16), b.astype(BF16), (((ca,), (cb,)), ((), ())), preferred_element_type=F32)


def _mmh(a, b):
    return jnp.dot(a, b, precision=HI, preferred_element_type=F32)


def _gdn_common(q, k, v, beta, gc_c, gc_r):
    r, c = _iota((LANE, LANE), 0), _iota((LANE, LANE), 1)
    same = (r // DN_CHUNK) == (c // DN_CHUNK)
    incl, strict = same & (r >= c), same & (r > c)
    d = jnp.exp(jnp.where(incl, gc_c - gc_r, NEG))
    kk = _mm(k, k, 1, 1)
    qk = _mm(q, k, 1, 1)
    yield
    ahat = jnp.where(strict, kk * d, 0.0)
    a = ahat * beta
    eye = jnp.where(r == c, 1.0, 0.0).astype(F32)
    t = eye - a
    pw = _mmh(a, a)
    yield
    for step in range(5):
        t = t + _mmh(t, pw)
        if step < 4:
            pw = _mmh(pw, pw)
        yield
    row = _iota((LANE, 1), 0)
    gl0 = jnp.sum(jnp.where(row == DN_CHUNK - 1, gc_c, 0.0), axis=0, keepdims=True)
    gl1 = jnp.sum(jnp.where(row == LANE - 1, gc_c, 0.0), axis=0, keepdims=True)
    gam = jnp.exp(gc_c)
    lam = jnp.exp(jnp.where(row < DN_CHUNK, gl0, gl1) - gc_c)
    kb, vb = k * (beta * gam), v * beta
    cm = dict(incl=incl, strict=strict, d=d, kk=kk, ahat=ahat, t=t, gam=gam, lam=lam, kb=kb, vb=vb, w=_mm(t, kb), u0=_mm(t, vb),
              qk=qk, pm=jnp.where(incl, qk * d, 0.0), qg=q * gam, kl=k * lam, g0=jnp.exp(gl0), g1=jnp.exp(gl1))
    yield
    return cm


def _gdn_states(cm, s0):
    c = DN_CHUNK
    u_a = cm["u0"][:c] - _mm(cm["w"][:c], s0, 1, 1)
    yield
    s1 = cm["g0"] * s0 + _mm(u_a, cm["kl"][:c], 0, 0)
    yield
    u_b = cm["u0"][c:] - _mm(cm["w"][c:], s1, 1, 1)
    yield
    s2 = cm["g1"] * s1 + _mm(u_b, cm["kl"][c:], 0, 0)
    yield
    return u_a, s1, u_b, s2


def gdn_chunk_fwd(qkv, bg, bgt, proj, ogain, *, hd, name):
    n = qkv.shape[0]
    nb = n // LANE
    nh = hd // LANE
    c = DN_CHUNK

    def body(q_ref, k_ref, v_ref, bg_ref, bgt_ref, g_ref, gain_ref, o_ref, og_ref, hist_ref, s_ref):
        @pl.when(pl.program_id(0) == 0)
        def _():
            s_ref[...] = jnp.zeros_like(s_ref)

        hist_ref[0] = s_ref[...]

        def head(h):
            cols = slice(h * LANE, (h + 1) * LANE)
            cm = yield from _gdn_common(q_ref[:, cols], k_ref[:, cols], v_ref[:, cols], bg_ref[:, h:h + 1],
                                        bg_ref[:, nh + h:nh + h + 1], bgt_ref[nh + h:nh + h + 1, :])
            s0 = s_ref[h]
            u_a, s1, u_b, s2 = yield from _gdn_states(cm, s0)
            u_all = jnp.concatenate([u_a, u_b], axis=0)
            o = jnp.concatenate([_mm(cm["qg"][:c], s0, 1, 1), _mm(cm["qg"][c:], s1, 1, 1)], axis=0) + _mm(cm["pm"], u_all)
            s_ref[h] = s2
            o_ref[:, cols] = o
            rn = lax.rsqrt(jnp.mean(o * o, axis=-1, keepdims=True) + EPS)
            og_ref[:, cols] = (o * rn * gain_ref[...] * _silu(g_ref[:, cols])).astype(og_ref.dtype)

        _lockstep(head(h) for h in range(nh))

    col = lambda cc: pl.BlockSpec((LANE, hd), lambda b: (b, cc))
    return pl.pallas_call(
        body, name=name, grid=(nb,),
        in_specs=[col(0), col(1), col(2), pl.BlockSpec((LANE, LANE), lambda b: (b, 0)),
                  pl.BlockSpec((2 * nh, LANE), lambda b: (0, b)), pl.BlockSpec((LANE, hd), lambda b: (b, 3)),
                  pl.BlockSpec((1, LANE), lambda b: (0, 0))],
        out_specs=[col(0), col(0), pl.BlockSpec((1, nh, LANE, LANE), lambda b: (b, 0, 0, 0))],
        out_shape=[jax.ShapeDtypeStruct((n, hd), F32), jax.ShapeDtypeStruct((n, hd), BF16),
                   jax.ShapeDtypeStruct((nb, nh, LANE, LANE), F32)],
        scratch_shapes=[pltpu.VMEM((nh, LANE, LANE), F32)],
        compiler_params=_cparams("arbitrary"),
    )(qkv, qkv, qkv, bg, bgt, proj, ogain)


def gdn_chunk_bwd(qkv, bg, bgt, proj, ogain, o_raw, dog, hist, *, hd, name):
    n = qkv.shape[0]
    nb = n // LANE
    nh = hd // LANE
    c = DN_CHUNK

    def body(q_ref, k_ref, v_ref, bg_ref, bgt_ref, g_ref, gain_ref, o_ref, dog_ref, hist_ref,
             dq_ref, dk_ref, dv_ref, dgate_ref, dbg_ref, dgt_ref, dgain_ref, ds_ref):
        first = pl.program_id(0) == 0

        @pl.when(first)
        def _():
            ds_ref[...] = jnp.zeros_like(ds_ref)

        lane = _iota((LANE, LANE), 1)
        row = _iota((LANE, 1), 0)

        def head(h):
            cols = slice(h * LANE, (h + 1) * LANE)
            q, k, v = q_ref[:, cols], k_ref[:, cols], v_ref[:, cols]
            beta = bg_ref[:, h:h + 1]
            cm = yield from _gdn_common(q, k, v, beta, bg_ref[:, nh + h:nh + h + 1], bgt_ref[nh + h:nh + h + 1, :])
            s0 = hist_ref[0, h]
            u_a, s1, u_b, _ = yield from _gdn_states(cm, s0)
            u_all = jnp.concatenate([u_a, u_b], axis=0)
            o, gate, d_out, gain = o_ref[:, cols], g_ref[:, cols], dog_ref[:, cols], gain_ref[...]
            rn = lax.rsqrt(jnp.mean(o * o, axis=-1, keepdims=True) + EPS)
            xh = o * rn
            d_on = d_out * _silu(gate)
            dgate_ref[:, cols] = d_out * xh * gain * _silu_grad(gate)
            dgain = _colsum8(d_on * xh)
            gdy = d_on * gain
            d_o = rn * (gdy - xh * jnp.mean(xh * gdy, axis=-1, keepdims=True))
            pt_do = _mm(cm["pm"], d_o, 0, 0)
            ds_in = ds_ref[h]
            yield
            du_b = _mm(cm["kl"][c:], ds_in, 1, 1) + pt_do[c:]
            dkl_b = _mm(u_b, ds_in)
            dqg_b = _mm(d_o[c:], s1)
            dg1 = jnp.sum(jnp.sum(ds_in * s1, axis=1, keepdims=True), axis=0, keepdims=True)
            dw_b = -_mm(du_b, s1)
            yield
            ds_mid = cm["g1"] * ds_in + _mm(d_o[c:], cm["qg"][c:], 0, 0) - _mm(du_b, cm["w"][c:], 0, 0)
            yield
            du_a = _mm(cm["kl"][:c], ds_mid, 1, 1) + pt_do[:c]
            dkl_a = _mm(u_a, ds_mid)
            dqg_a = _mm(d_o[:c], s0)
            dg0 = jnp.sum(jnp.sum(ds_mid * s0, axis=1, keepdims=True), axis=0, keepdims=True)
            yield
            dw_a = -_mm(du_a, s0)
            ds_ref[h] = cm["g0"] * ds_mid + _mm(d_o[:c], cm["qg"][:c], 0, 0) - _mm(du_a, cm["w"][:c], 0, 0)
            du = jnp.concatenate([du_a, du_b], axis=0)
            dkl = jnp.concatenate([dkl_a, dkl_b], axis=0)
            dqg = jnp.concatenate([dqg_a, dqg_b], axis=0)
            dw = jnp.concatenate([dw_a, dw_b], axis=0)
            t, d, gam, lam = cm["t"], cm["d"], cm["gam"], cm["lam"]
            dp = jnp.where(cm["incl"], _mm(d_o, u_all, 1, 1), 0.0)
            dt = _mm(dw, cm["kb"], 1, 1) + _mm(du, cm["vb"], 1, 1)
            dkb = _mm(t, dw, 0, 0)
            dvb = _mm(t, du, 0, 0)
            yield
            x_t = _mm(t, dt, 0, 0)
            yield
            da = jnp.where(cm["strict"], -_mm(x_t, t, 1, 1), 0.0)
            yield
            kb_k = jnp.sum(dkb * k, axis=1, keepdims=True)
            dbeta = jnp.sum(da * cm["ahat"], axis=1, keepdims=True) + gam * kb_k + jnp.sum(dvb * v, axis=1, keepdims=True)
            dahat = da * beta
            dkk = dahat * d
            dqk = dp * d
            e = (dahat * cm["kk"] + dp * cm["qk"]) * d
            dk_ref[:, cols] = (_mm(dkk, k) + _mm(dkk, k, 0, 0) + _mm(dqk, q, 0, 0) + dkb * (beta * gam) + dkl * lam)
            dq_ref[:, cols] = _mm(dqk, k) + dqg * gam
            dv_ref[:, cols] = dvb * beta
            dgam = beta * kb_k + jnp.sum(dqg * q, axis=1, keepdims=True)
            dlam_lam = jnp.sum(dkl * k, axis=1, keepdims=True) * lam
            dgl0 = jnp.sum(jnp.where(row < c, dlam_lam, 0.0), axis=0, keepdims=True) + dg0 * cm["g0"]
            dgl1 = jnp.sum(jnp.where(row >= c, dlam_lam, 0.0), axis=0, keepdims=True) + dg1 * cm["g1"]
            dgc = (jnp.sum(e, axis=1, keepdims=True) + dgam * gam - dlam_lam
                   + jnp.where(row == c - 1, dgl0, 0.0) + jnp.where(row == LANE - 1, dgl1, 0.0))
            dgt_ref[h:h + 1, :] = -jnp.sum(e, axis=0, keepdims=True)
            return jnp.where(lane == h, dbeta, 0.0) + jnp.where(lane == nh + h, dgc, 0.0), dgain

        parts = _lockstep(head(h) for h in range(nh))
        dbg_ref[...] = sum(p[0] for p in parts)
        _accum(dgain_ref, sum(p[1] for p in parts), first)

    rev = lambda b: nb - 1 - b
    col = lambda cc: pl.BlockSpec((LANE, hd), lambda b: (rev(b), cc))
    return pl.pallas_call(
        body, name=name, grid=(nb,),
        in_specs=[col(0), col(1), col(2), pl.BlockSpec((LANE, LANE), lambda b: (rev(b), 0)),
                  pl.BlockSpec((2 * nh, LANE), lambda b: (0, rev(b))), pl.BlockSpec((LANE, hd), lambda b: (rev(b), 3)),
                  pl.BlockSpec((1, LANE), lambda b: (0, 0)), col(0), col(0),
                  pl.BlockSpec((1, nh, LANE, LANE), lambda b: (rev(b), 0, 0, 0))],
        out_specs=[col(0), col(0), col(0), col(0), pl.BlockSpec((LANE, LANE), lambda b: (rev(b), 0)),
                   pl.BlockSpec((nh, LANE), lambda b: (0, rev(b))), pl.BlockSpec((SUB, LANE), lambda b: (0, 0))],
        out_shape=[jax.ShapeDtypeStruct((n, hd), F32)] * 4 + [jax.ShapeDtypeStruct((n, LANE), F32),
                   jax.ShapeDtypeStruct((nh, n), F32), jax.ShapeDtypeStruct((SUB, LANE), F32)],
        scratch_shapes=[pltpu.VMEM((nh, LANE, LANE), F32)],
        compiler_params=_cparams("arbitrary"),
    )(qkv, qkv, qkv, bg, bgt, proj, ogain, o_raw, dog, hist)


def gdn_gates_bwd(proj, alog, dtb, dbg, *, hd, pad, name):
    n = proj.shape[0]
    gcol = 4 * (hd // LANE)

    def body(x_ref, a_ref, d_ref, dbg_ref, o_ref, da_ref, dd_ref):
        i = pl.program_id(0)
        x = x_ref[...]
        lane = _iota(x.shape, 1)
        rows = i * LANE + _iota((LANE, 1), 0)
        isg = (lane >= DN_HEADS) & (lane < 2 * DN_HEADS)
        dbgv = jnp.where(rows >= pad, dbg_ref[...], 0.0)
        dg = jnp.dot(_chunk_tri(True), jnp.where(isg, dbgv, 0.0), precision=HI, preferred_element_type=F32)
        ea = jnp.exp(a_ref[...])
        z = x + d_ref[...]
        dg = jnp.where(rows >= pad, dg, 0.0)
        dz = jnp.where(isg, dg * (-ea) * _sigmoid(z), 0.0)
        sb = _sigmoid(x)
        o_ref[...] = jnp.where(lane < DN_HEADS, dbgv * sb * (1.0 - sb), dz).astype(o_ref.dtype)
        _accum(da_ref, _colsum8(jnp.where(isg, dg * (-ea) * _softplus(z), 0.0)), i == 0)
        _accum(dd_ref, _colsum8(dz), i == 0)

    vec = pl.BlockSpec((1, LANE), lambda i: (0, 0))
    blk = pl.BlockSpec((LANE, LANE), lambda i: (i, 0))
    acc = pl.BlockSpec((SUB, LANE), lambda i: (0, 0))
    return pl.pallas_call(
        body, name=name, grid=(n // LANE,),
        in_specs=[pl.BlockSpec((LANE, LANE), lambda i: (i, gcol)), vec, vec, blk],
        out_specs=[blk, acc, acc],
        out_shape=[jax.ShapeDtypeStruct((n, LANE), BF16), jax.ShapeDtypeStruct((SUB, LANE), F32), jax.ShapeDtypeStruct((SUB, LANE), F32)],
        compiler_params=_cparams("arbitrary"),
    )(proj, alog, dtb, dbg)


def gdn_prep_bwd(proj, cw, dqkv, *, hd, name):
    n = proj.shape[0]
    tr = _tile(n, 640, SUB)
    nh = hd // LANE
    r8 = tr // SUB
    qscale = DN_HEAD_DIM ** -0.5

    def body(x_ref, p_ref, w_ref, dq_ref, dk_ref, dv_ref, o_ref, dw_ref):
        c, i = pl.program_id(0), pl.program_id(1)
        first = i == 0
        prev = jnp.where(first, 0.0, p_ref[...])
        cur = x_ref[...]
        cv = _conv_rows(cur, prev, w_ref[...])
        s = _silu(cv)
        r = lax.rsqrt(jnp.sum(s * s, axis=-1, keepdims=True) + EPS)
        y = s * r
        dy = jnp.where(c < nh, dq_ref[...] * qscale, dk_ref[...])
        ds_norm = r * (dy - y * jnp.sum(dy * y, axis=-1, keepdims=True))
        dcv = jnp.where(c < 2 * nh, ds_norm, dv_ref[...]) * _silu_grad(cv)
        o_ref[...] = dcv
        for k in range(DN_CONV):
            part = _colsum8(dcv * _shift_down(cur, prev, DN_CONV - 1 - k))

            @pl.when(first)
            def _():
                dw_ref[k] = part

            @pl.when(jnp.logical_not(first))
            def _():
                dw_ref[k] += part

    blk = lambda f: pl.BlockSpec((tr, LANE), f)
    return pl.pallas_call(
        body, name=name, grid=(3 * nh, n // tr),
        in_specs=[blk(lambda c, i: (i, c)), pl.BlockSpec((SUB, LANE), lambda c, i: (jnp.maximum(i * r8 - 1, 0), c)),
                  pl.BlockSpec((DN_CONV, LANE), lambda c, i: (0, c)),
                  blk(lambda c, i: (i, jnp.minimum(c, nh - 1))), blk(lambda c, i: (i, jnp.clip(c - nh, 0, nh - 1))),
                  blk(lambda c, i: (i, jnp.clip(c - 2 * nh, 0, nh - 1)))],
        out_specs=[blk(lambda c, i: (i, c)), pl.BlockSpec((DN_CONV, SUB, LANE), lambda c, i: (0, 0, c))],
        out_shape=[jax.ShapeDtypeStruct((n, 3 * hd), F32), jax.ShapeDtypeStruct((DN_CONV, SUB, 3 * hd), F32)],
        compiler_params=_cparams("parallel", "arbitrary"),
    )(proj, proj, cw, *dqkv)


def loss_head(h, target, *, x0, name):
    n, d = h.shape
    tr = LANE
    nb0 = x0 // tr

    def body(h_ref, t_ref, dh_ref, sq_ref):
        i = pl.program_id(0)
        live = i >= nb0
        err = jnp.where(live, h_ref[...] - t_ref[...], 0.0)
        dh_ref[...] = err * (1.0 / d)
        _accum(sq_ref, _colsum8(err * err), i == 0)

    row = pl.BlockSpec((tr, d), lambda i: (i, 0))
    return pl.pallas_call(
        body, name=name, grid=(n // tr,),
        in_specs=[row, pl.BlockSpec((tr, d), lambda i: (jnp.maximum(i - nb0, 0), 0))],
        out_specs=[row, pl.BlockSpec((SUB, d), lambda i: (0, 0))],
        out_shape=[jax.ShapeDtypeStruct((n, d), F32), jax.ShapeDtypeStruct((SUB, d), F32)],
        compiler_params=_cparams("arbitrary"),
    )(h, target)


def adamw(w, g, m, v, *, name):
    r, c = w.shape
    tr = _tile(r, 512, SUB) if r % SUB == 0 else r
    c1 = 1.0 / (1.0 - ADAM_B1 ** ADAM_STEP)
    c2 = 1.0 / (1.0 - ADAM_B2 ** ADAM_STEP)

    def body(w_ref, g_ref, m_ref, v_ref, d_ref, mo_ref, vo_ref):
        gv = g_ref[...]
        mn = ADAM_B1 * m_ref[...] + (1.0 - ADAM_B1) * gv
        vn = ADAM_B2 * v_ref[...] + (1.0 - ADAM_B2) * (gv * gv)
        d_ref[...] = -ADAM_LR * ((mn * c1) / (jnp.sqrt(vn * c2) + ADAM_EPS) + ADAM_WD * w_ref[...])
        mo_ref[...] = mn
        vo_ref[...] = vn

    blk = pl.BlockSpec((tr, c), lambda i: (i, 0))
    return pl.pallas_call(
        body, name=name, grid=(r // tr,), in_specs=[blk] * 4, out_specs=[blk] * 3,
        out_shape=[jax.ShapeDtypeStruct((r, c), F32)] * 3, compiler_params=_cparams("parallel"),
    )(w, g, m, v)


_BIG = ("attn_w_in", "attn_w_out", "dn_w_in", "dn_w_out", "ffn_w_up", "ffn_w_down")


def _row(v, width=None):
    v = v.astype(F32).reshape(1, -1)
    if width is not None and v.shape[1] < width:
        v = jnp.pad(v, ((0, 0), (0, width - v.shape[1])))
    return v


def _fold8(p):
    return jnp.sum(p, axis=-2)


def local_step(x, target, w):
    seq, d = x.shape
    pad = (-(N_META + seq)) % LANE
    x0 = pad + N_META
    n = x0 + seq
    depth = w["g_pre"].shape[0]
    hd_a = ATT_HEADS * ATT_HEAD_DIM
    hd_d = DN_HEADS * DN_HEAD_DIM
    t_att = _tile(n, 640)
    h = jnp.concatenate([jnp.zeros((pad, d), F32), w["meta"].astype(F32), x], axis=0)
    saved = []
    for i in range(depth):
        j = i // 2
        s = dict(h=h)
        s["a"] = rms_fwd(h, _row(w["g_pre"][i]), out_dtype=BF16, name="rms_pre")
        if i % 2 == 0:
            s["proj"] = proj = matmul(s["a"], w["attn_w_in"][j], name="mm_attn_in")
            qg, kg = _row(jnp.tile(w["attn_qg"][j], ATT_HEADS)), _row(jnp.tile(w["attn_kg"][j], ATT_HEADS))
            bf = _row(w["attn_b"][j], LANE)
            s["qs"], s["kn"], s["v"], logf = attn_prep(proj, qg, kg, bf, hd=hd_a, name="attn_prep")
            c = cumsum_rows(logf, reverse=False, name="cumsum_fwd")
            s["ct"] = c[:, :ATT_HEADS].T.reshape(ATT_HEADS // 2, 2, n)
            s["o"], s["og"], s["lse"] = flash_fwd(s["qs"], s["kn"], s["v"], proj, s["ct"], pad=pad, t=t_att, name="flash_fwd")
            s["m"] = matmul(s["og"], w["attn_w_out"][j], name="mm_attn_out")
        else:
            s["proj"] = proj = matmul(s["a"], w["dn_w_in"][j], name="mm_dn_in")
            s["qkv"] = gdn_prep(proj, w["dn_conv"][j], hd=hd_d, name="gdn_prep")
            alog = jnp.pad(_row(w["dn_alog"][j]), ((0, 0), (DN_HEADS, LANE - 2 * DN_HEADS)))
            dtb = jnp.pad(_row(w["dn_dtb"][j]), ((0, 0), (DN_HEADS, LANE - 2 * DN_HEADS)))
            s["bg"] = gdn_gates(proj, alog, dtb, hd=hd_d, name="gdn_gates")
            s["bgt"] = s["bg"][:, :2 * DN_HEADS].T
            s["o"], s["og"], s["hist"] = gdn_chunk_fwd(s["qkv"], s["bg"], s["bgt"], proj, _row(w["dn_og"][j]), hd=hd_d, name="gdn_fwd")
            s["m"] = matmul(s["og"], w["dn_w_out"][j], name="mm_dn_out")
        s["h_mid"] = rms_fwd(s["m"], _row(w["g_post"][i]), res=h, out_dtype=F32, name="rms_post")
        s["b"] = rms_fwd(s["h_mid"], _row(w["g_fpre"][i]), out_dtype=BF16, name="rms_fpre")
        s["u"] = matmul(s["b"], w["ffn_w_up"][i], out_dtype=BF16, name="mm_ffn_up")
        s["act"] = conv_glu_fwd(s["u"], w["ffn_conv"][i], name="ffn_glu")
        s["f"] = matmul(s["act"], w["ffn_w_down"][i], name="mm_ffn_down")
        h = rms_fwd(s["f"], _row(w["g_fpost"][i]), res=s["h_mid"], out_dtype=F32, name="rms_fpost")
        saved.append(s)

    dh, sq = loss_head(h, target, x0=x0, name="loss_head")
    loss = 0.5 * jnp.sum(sq) / d

    g = {k: [None] * depth for k in ("g_pre", "g_post", "g_fpre", "g_fpost", "ffn_w_up", "ffn_conv", "ffn_w_down")}
    for k in ("attn_w_in", "attn_b", "attn_qg", "attn_kg", "attn_w_out", "dn_w_in", "dn_conv", "dn_alog", "dn_dtb", "dn_og", "dn_w_out"):
        g[k] = [None] * (depth // 2)
    for i in reversed(range(depth)):
        j = i // 2
        s = saved[i]
        proj = s["proj"]
        df, p8 = rms_bwd(s["f"], _row(w["g_fpost"][i]), dh, pad=pad, name="rms_fpost_bwd")
        g["g_fpost"][i] = _fold8(p8)
        dact = matmul(df, w["ffn_w_down"][i], trans_b=True, out_dtype=BF16, name="mm_ffn_down_dx")
        g["ffn_w_down"][i] = matmul(s["act"], df, trans_a=True, out_dtype=BF16, name="mm_ffn_down_dw")
        duc, p8 = conv_glu_bwd(s["u"], w["ffn_conv"][i], dact, name="ffn_glu_bwd")
        g["ffn_conv"][i] = _fold8(p8)
        du = conv_transpose(duc, w["ffn_conv"][i], out_dtype=BF16, name="ffn_conv_t")
        db = matmul(du, w["ffn_w_up"][i], trans_b=True, name="mm_ffn_up_dx")
        g["ffn_w_up"][i] = matmul(s["b"], du, trans_a=True, out_dtype=BF16, name="mm_ffn_up_dw")
        dh_mid, p8 = rms_bwd(s["h_mid"], _row(w["g_fpre"][i]), db, res=dh, pad=pad, name="rms_fpre_bwd")
        g["g_fpre"][i] = _fold8(p8)
        dm, p8 = rms_bwd(s["m"], _row(w["g_post"][i]), dh_mid, pad=pad, name="rms_post_bwd")
        g["g_post"][i] = _fold8(p8)
        if i % 2 == 0:
            g["attn_w_out"][j] = matmul(s["og"], dm, trans_a=True, out_dtype=BF16, name="mm_attn_out_dw")
            dgated = matmul(dm, w["attn_w_out"][j], trans_b=True, name="mm_attn_out_dx")
            do, delta, dgate = attn_bwd_prep(dgated, s["o"], proj, hd=hd_a, name="attn_bwd_prep")
            dqs, dkn, dv, dck, dcr = flash_bwd(s["qs"], s["kn"], s["v"], do, s["lse"], delta, s["ct"], pad=pad, t=t_att, name="flash_bwd")
            dc = (dcr - dck)[:, ::ATT_HEAD_DIM].reshape(n, ATT_HEADS // 2, 2)[:, :, ::-1].reshape(n, ATT_HEADS)
            dc = jnp.pad(dc, ((0, 0), (0, LANE - ATT_HEADS)))
            dlogf = cumsum_rows(dc, reverse=True, name="cumsum_bwd")
            qg, kg = _row(jnp.tile(w["attn_qg"][j], ATT_HEADS)), _row(jnp.tile(w["attn_kg"][j], ATT_HEADS))
            bf = _row(w["attn_b"][j], LANE)
            dq_raw, dk_raw, dfl, gq8, gk8, gb8 = attn_in_bwd(dqs, dkn, proj, qg, kg, bf, dlogf, hd=hd_a, pad=pad, name="attn_in_bwd")
            g["attn_qg"][j] = _fold8(gq8).reshape(ATT_HEADS, ATT_HEAD_DIM).sum(axis=0)
            g["attn_kg"][j] = _fold8(gk8).reshape(ATT_HEADS, ATT_HEAD_DIM).sum(axis=0)
            g["attn_b"][j] = _fold8(gb8)[:ATT_HEADS]
            dproj = jnp.concatenate([dq_raw, dk_raw, dv.astype(BF16), dgate.astype(BF16), dfl], axis=1)
            w_in, key = w["attn_w_in"][j], "attn_w_in"
        else:
            g["dn_w_out"][j] = matmul(s["og"], dm, trans_a=True, out_dtype=BF16, name="mm_dn_out_dw")
            dgated = matmul(dm, w["dn_w_out"][j], trans_b=True, name="mm_dn_out_dx")
            alog = jnp.pad(_row(w["dn_alog"][j]), ((0, 0), (DN_HEADS, LANE - 2 * DN_HEADS)))
            dtb = jnp.pad(_row(w["dn_dtb"][j]), ((0, 0), (DN_HEADS, LANE - 2 * DN_HEADS)))
            dq, dk, dv, dgate, dbg, dgt, gain8 = gdn_chunk_bwd(s["qkv"], s["bg"], s["bgt"], proj, _row(w["dn_og"][j]), s["o"], dgated,
                                                              s["hist"], hd=hd_d, name="gdn_bwd")
            g["dn_og"][j] = _fold8(gain8)
            dbg = dbg + jnp.pad(dgt.T, ((0, 0), (DN_HEADS, LANE - 2 * DN_HEADS)))
            dgl, da8, dd8 = gdn_gates_bwd(proj, alog, dtb, dbg, hd=hd_d, pad=pad, name="gdn_gates_bwd")
            g["dn_alog"][j] = _fold8(da8)[DN_HEADS:2 * DN_HEADS]
            g["dn_dtb"][j] = _fold8(dd8)[DN_HEADS:2 * DN_HEADS]
            dcv, p8 = gdn_prep_bwd(proj, w["dn_conv"][j], (dq, dk, dv), hd=hd_d, name="gdn_prep_bwd")
            g["dn_conv"][j] = _fold8(p8)
            dqkv = conv_transpose(dcv, w["dn_conv"][j], out_dtype=BF16, name="gdn_conv_t")
            dproj = jnp.concatenate([dqkv, dgate.astype(BF16), dgl], axis=1)
            w_in, key = w["dn_w_in"][j], "dn_w_in"
        da = matmul(dproj, w_in, trans_b=True, name="mm_in_dx")
        g[key][j] = matmul(s["a"], dproj, trans_a=True, out_dtype=BF16, name="mm_in_dw")
        dh, p8 = rms_bwd(s["h"], _row(w["g_pre"][i]), da, res=dh_mid, pad=pad, name="rms_pre_bwd")
        g["g_pre"][i] = _fold8(p8)

    grads = {k: (v if k in _BIG else jnp.stack(v)) for k, v in g.items()}
    grads["meta"] = dh[pad:x0]
    return loss, dh[x0:], grads


_ANY = pl.BlockSpec(memory_space=pl.ANY)


def _mesh_place():
    x, y, c = lax.axis_index("x"), lax.axis_index("y"), lax.axis_index("c")
    return x, y, c, 4 * x + 2 * y + c


def _peer(x, y, c, k):
    px, py, pc = (1 - x if k & 4 else x), (1 - y if k & 2 else y), (1 - c if k & 1 else c)
    return (px, py, pc), 4 * px + 2 * py + pc


def _window_blocks(shard):
    return max(-(-(shard * (d + 1)) // LANE) - (shard * d) // LANE for d in range(N_DEV))


def _sds(shape, dtype):
    return jax.ShapeDtypeStruct(tuple(shape), dtype)


def _plan_gather(buf):
    return _sds((N_DEV,) + buf.shape, buf.dtype), (lambda r, i: r), (lambda o, i: o.at[i])


def _plan_scatter(buf):
    return _sds(buf.shape, buf.dtype), (lambda r, i: r.at[i]), (lambda o, i: o.at[i])


def _plan_gather_rows(buf):
    l, r, c = buf.shape
    return _sds((l, N_DEV * r, c), buf.dtype), (lambda ref, i: ref), (lambda o, i: o.at[:, pl.ds(pl.multiple_of(i * r, SUB), r), :])


def _plan_scatter_rows(buf):
    r, c = buf.shape[0] // N_DEV, buf.shape[1]
    return _sds((N_DEV, r, c), buf.dtype), (lambda ref, i: ref.at[pl.ds(pl.multiple_of(i * r, SUB), r), :]), (lambda o, i: o.at[i])


def _plan_scatter_cols(buf, shard):
    ww = _window_blocks(shard) * LANE
    src = lambda ref, i: ref.at[:, pl.ds(pl.multiple_of((shard * i) // LANE * LANE, LANE), ww)]
    return _sds((N_DEV, buf.shape[0], ww), buf.dtype), src, (lambda o, i: o.at[i])


def exchange(bufs, plans, *, name):
    nbuf = len(bufs)

    def body(*refs):
        ins, outs = refs[:nbuf], refs[nbuf:2 * nbuf]
        send_sems, recv_sems, loc_sems = refs[2 * nbuf:]
        x, y, c, me = _mesh_place()
        local = [pltpu.make_async_copy(plans[b][1](ins[b], me), plans[b][2](outs[b], me), loc_sems.at[b]) for b in range(nbuf)]
        for cp in local:
            cp.start()
        sends, recvs = [], []
        for k in range(1, N_DEV):
            peer, pidx = _peer(x, y, c, k)
            for b in range(nbuf):
                sems = dict(send_sem=send_sems.at[b, k - 1], recv_sem=recv_sems.at[b, k - 1], device_id=peer,
                            device_id_type=pl.DeviceIdType.MESH)
                cp = pltpu.make_async_remote_copy(src_ref=plans[b][1](ins[b], pidx), dst_ref=plans[b][2](outs[b], me), **sems)
                cp.start()
                sends.append(cp)
                recvs.append(pltpu.make_async_remote_copy(src_ref=plans[b][1](ins[b], pidx), dst_ref=plans[b][2](outs[b], pidx), **sems))
        for cp in recvs:
            cp.wait_recv()
        for cp in sends:
            cp.wait_send()
        for cp in local:
            cp.wait()

    return pl.pallas_call(
        body, name=name, in_specs=[_ANY] * nbuf, out_specs=[_ANY] * nbuf, out_shape=[p[0] for p in plans],
        scratch_shapes=[pltpu.SemaphoreType.DMA((nbuf, N_DEV - 1)), pltpu.SemaphoreType.DMA((nbuf, N_DEV - 1)),
                        pltpu.SemaphoreType.DMA((nbuf,))],
        compiler_params=pltpu.CompilerParams(has_side_effects=True),
    )(*bufs)


def slot_sum(x, *, name):
    _, r, c = x.shape
    tr = _tile(r, 512, 16)

    def body(x_ref, o_ref):
        acc = x_ref[0].astype(F32)
        for d in range(1, N_DEV):
            acc = acc + x_ref[d].astype(F32)
        o_ref[...] = acc

    return pl.pallas_call(
        body, name=name, grid=(r // tr,), in_specs=[pl.BlockSpec((N_DEV, tr, c), lambda i: (0, i, 0))],
        out_specs=pl.BlockSpec((tr, c), lambda i: (i, 0)), out_shape=jax.ShapeDtypeStruct((r, c), F32),
        compiler_params=_cparams("parallel"),
    )(x)


def assemble_cols(win, shard, *, name):
    _, r, ww = win.shape
    wb = ww // LANE
    nbo = -(-(N_DEV * shard) // LANE)
    tab = []
    for b in range(nbo):
        hits = [(d, b - (shard * d) // LANE) for d in range(N_DEV) if 0 <= b - (shard * d) // LANE < wb]
        assert 1 <= len(hits) <= 2, (b, hits)
        tab.append([hits[0][0], hits[0][1], hits[-1][0], hits[-1][1], len(hits) - 1])
    tab = jnp.array(tab, jnp.int32).T
    tr = _tile(r, 512, 16)

    def body(tab_ref, a_ref, b_ref, o_ref):
        two = tab_ref[4, pl.program_id(1)] > 0
        o_ref[...] = a_ref[0] + jnp.where(two, b_ref[0], jnp.zeros_like(b_ref[0]))

    grid_spec = pltpu.PrefetchScalarGridSpec(
        num_scalar_prefetch=1, grid=(r // tr, nbo),
        in_specs=[pl.BlockSpec((1, tr, LANE), lambda i, b, t: (t[0, b], i, t[1, b])),
                  pl.BlockSpec((1, tr, LANE), lambda i, b, t: (t[2, b], i, t[3, b]))],
        out_specs=pl.BlockSpec((tr, LANE), lambda i, b, t: (i, b)))
    return pl.pallas_call(body, name=name, grid_spec=grid_spec, out_shape=jax.ShapeDtypeStruct((r, nbo * LANE), win.dtype),
                          compiler_params=_cparams("parallel", "parallel"))(tab, win, win)


def _pack(parts, dtype, lead=()):
    nl = len(lead)
    flat = jnp.concatenate([p.astype(dtype).reshape(lead + (-1,)) for p in parts], axis=nl)
    tot = flat.shape[nl]
    rows = -(-tot // (16 * LANE)) * 16
    flat = jnp.pad(flat, [(0, 0)] * nl + [(0, rows * LANE - tot)])
    return flat.reshape(lead + (rows, LANE))


def _unpack(buf, shapes, lead=()):
    nl = len(lead)
    flat = buf.reshape(lead + (-1,))
    out, off = [], 0
    for shp in shapes:
        size = 1
        for s in shp:
            size *= s
        out.append(lax.slice_in_dim(flat, off, off + size, axis=nl).reshape(lead + tuple(shp)))
        off += size
    return out


def _whole(g8, axis):
    t = jnp.moveaxis(g8, 0, axis)
    shp = t.shape
    return t.reshape(shp[:axis] + (shp[axis] * shp[axis + 1],) + shp[axis + 2:])


def _slots(full, axis):
    shp = full.shape
    t = full.reshape(shp[:axis] + (N_DEV, shp[axis] // N_DEV) + shp[axis + 1:])
    return jnp.moveaxis(t, axis, 0)


_PARAMS = (("meta_tokens", 1), ("norm_mix_pre", None), ("norm_mix_post", None), ("norm_ffn_pre", None), ("norm_ffn_post", None),
           ("attn_w_in", 2), ("attn_b_forget", None), ("attn_q_norm", None), ("attn_k_norm", None), ("attn_w_out", 1), ("dn_w_in", 2),
           ("dn_conv", 2), ("dn_a_log", None), ("dn_dt_bias", None), ("dn_o_norm", None), ("dn_w_out", 1), ("ffn_w_up", 2),
           ("ffn_conv", 2), ("ffn_w_down", 1))
_LOCAL_KEY = dict(meta_tokens="meta", norm_mix_pre="g_pre", norm_mix_post="g_post", norm_ffn_pre="g_fpre", norm_ffn_post="g_fpost",
                  attn_w_in="attn_w_in", attn_b_forget="attn_b", attn_q_norm="attn_qg", attn_k_norm="attn_kg", attn_w_out="attn_w_out",
                  dn_w_in="dn_w_in", dn_conv="dn_conv", dn_a_log="dn_alog", dn_dt_bias="dn_dtb", dn_o_norm="dn_og", dn_w_out="dn_w_out",
                  ffn_w_up="ffn_w_up", ffn_conv="ffn_conv", ffn_w_down="ffn_w_down")
_COL_CUT = ("attn_w_in", "dn_w_in", "ffn_w_up")
_ROW_CUT = ("attn_w_out", "dn_w_out", "ffn_w_down")


def kernel(x, meta_tokens, norm_mix_pre, norm_mix_post, norm_ffn_pre, norm_ffn_post, attn_w_in, attn_b_forget, attn_q_norm, attn_k_norm, attn_w_out, dn_w_in, dn_conv, dn_a_log, dn_dt_bias, dn_o_norm, dn_w_out, ffn_w_up, ffn_conv, ffn_w_down, loss_target, m_meta_tokens, m_norm_mix_pre, m_norm_mix_post, m_norm_ffn_pre, m_norm_ffn_post, m_attn_w_in, m_attn_b_forget, m_attn_q_norm, m_attn_k_norm, m_attn_w_out, m_dn_w_in, m_dn_conv, m_dn_a_log, m_dn_dt_bias, m_dn_o_norm, m_dn_w_out, m_ffn_w_up, m_ffn_conv, m_ffn_w_down, v_meta_tokens, v_norm_mix_pre, v_norm_mix_post, v_norm_ffn_pre, v_norm_ffn_post, v_attn_w_in, v_attn_b_forget, v_attn_q_norm, v_attn_k_norm, v_attn_w_out, v_dn_w_in, v_dn_conv, v_dn_a_log, v_dn_dt_bias, v_dn_o_norm, v_dn_w_out, v_ffn_w_up, v_ffn_conv, v_ffn_w_down):
    given = dict(locals())
    names = [p[0] for p in _PARAMS]
    cut = [p for p in _PARAMS if p[1] is not None and p[0] not in _BIG]
    rep = [p for p in _PARAMS if p[1] is None]
    me = 4 * lax.axis_index("x") + 2 * lax.axis_index("y") + lax.axis_index("c")

    bufs, plans = [], []
    for n in _COL_CUT:
        layers, d, shard = given[n].shape
        win = jnp.zeros((layers * d, _window_blocks(shard) * LANE), BF16)
        win = lax.dynamic_update_slice(win, given[n].astype(BF16).reshape(layers * d, shard), (0, (shard * me) % LANE))
        bufs.append(win)
        plans.append(_plan_gather(win))
    for n in _ROW_CUT:
        bufs.append(given[n].astype(BF16))
        plans.append(_plan_gather_rows(bufs[-1]))
    bufs.append(_pack([given[n] for n, _ in cut], F32))
    plans.append(_plan_gather(bufs[-1]))
    got = exchange(bufs, plans, name="gather_weights")
    w = {}
    for n, g8 in zip(_COL_CUT, got[:3]):
        layers, d, shard = given[n].shape
        w[n] = assemble_cols(g8, shard, name="assemble_" + n).reshape(layers, d, -1)
    for n, full in zip(_ROW_CUT, got[3:6]):
        w[n] = full
    for (n, axis), g8 in zip(cut, _unpack(got[6], [given[n].shape for n, _ in cut], lead=(N_DEV,))):
        w[_LOCAL_KEY[n]] = _whole(g8, axis)
    for n, _ in rep:
        w[_LOCAL_KEY[n]] = given[n]

    loss, grad_x, g = local_step(x[0], loss_target[0], w)

    bufs, plans, what = [], [], []
    for n in _COL_CUT:
        for layer, gl in enumerate(g[n]):
            bufs.append(gl)
            plans.append(_plan_scatter_cols(gl, given[n].shape[2]))
            what.append((n, layer))
    for n in _ROW_CUT:
        for layer, gl in enumerate(g[n]):
            bufs.append(gl)
            plans.append(_plan_scatter_rows(gl))
            what.append((n, layer))
    nbig = len(bufs)
    bufs.append(_pack([_slots(g[_LOCAL_KEY[n]], axis) for n, axis in cut], F32, lead=(N_DEV,)))
    plans.append(_plan_scatter(bufs[-1]))
    bufs.append(_pack([g[_LOCAL_KEY[n]] for n, _ in rep] + [loss.reshape(1)], F32))
    plans.append(_plan_gather(bufs[-1]))
    got = exchange(bufs, plans, name="reduce_grads")
    per_layer = {n: [] for n in _BIG}
    for (n, layer), r8 in zip(what, got[:nbig]):
        tot = slot_sum(r8, name="sum_" + n)
        if n in _COL_CUT:
            shard = given[n].shape[2]
            tot = lax.dynamic_slice_in_dim(tot, (shard * me) % LANE, shard, axis=1)
        per_layer[n].append(tot)
    grads = {n: jnp.stack(v) for n, v in per_layer.items()}
    for (n, _), gv in zip(cut, _unpack(slot_sum(got[nbig], name="sum_cut"), [given[n].shape for n, _ in cut])):
        grads[n] = gv
    rep_sum = _unpack(slot_sum(got[nbig + 1], name="sum_rep"), [given[n].shape for n, _ in rep] + [(1,)])
    for (n, _), gv in zip(rep, rep_sum):
        grads[n] = gv
    loss_all = rep_sum[-1].reshape(())

    deltas, new_m, new_v = {}, {}, {}
    for n in names:
        shp = given[n].shape
        two_d = (-1, shp[-1])
        d, mn, vn = adamw(given[n].reshape(two_d), grads[n].reshape(two_d), given["m_" + n].reshape(two_d), given["v_" + n].reshape(two_d),
                          name="adamw_" + n)
        deltas[n], new_m[n], new_v[n] = d.reshape(shp), mn.reshape(shp), vn.reshape(shp)
    return (loss_all, grad_x[None], *[grads[n] for n in names], *[deltas[n] for n in names], *[new_m[n] for n in names],
            *[new_v[n] for n in names])
```

```python
import functools

import jax
import jax.numpy as jnp
from jax import lax
from jax.experimental import pallas as pl
from jax.experimental.pallas import tpu as pltpu

F32 = jnp.float32
BF16 = jnp.bfloat16
LANE = 128
SUB = 8
N_DEV = 8
N_META = 16
ATT_HEADS, ATT_HEAD_DIM = 16, 64
DN_HEADS, DN_HEAD_DIM, DN_CHUNK, DN_CONV = 8, 128, 64, 4
FFN_CONV = 3
EPS = 1e-6
NEG = -1e30
ADAM_LR, ADAM_B1, ADAM_B2, ADAM_EPS, ADAM_WD, ADAM_STEP = 0.001, 0.9, 0.999, 1e-08, 0.01, 10
HI = lax.Precision.HIGHEST
VMEM_LIMIT = 56 * 1024 * 1024


def _tile(n, target, align=LANE):
    if n <= target:
        return n
    best = None
    for t in range(align, target + 1, align):
        if n % t == 0:
            best = t
    assert best is not None, (n, target, align)
    return best


def _iota(shape, dim):
    return lax.broadcasted_iota(jnp.int32, shape, dim)


def _colsum8(x):
    r, c = x.shape
    return x.reshape(r // SUB, SUB, c).sum(axis=0)


def _cparams(*sem):
    return pltpu.CompilerParams(dimension_semantics=sem, vmem_limit_bytes=VMEM_LIMIT)


def _sigmoid(x):
    return 1.0 / (1.0 + jnp.exp(-x))


def _softplus(x):
    return jnp.maximum(x, 0.0) + jnp.log(1.0 + jnp.exp(-jnp.abs(x)))


def _accum(ref, part, first):
    @pl.when(first)
    def _():
        ref[...] = part

    @pl.when(jnp.logical_not(first))
    def _():
        ref[...] += part


def _lockstep(gens):
    gens = list(gens)
    out = [None] * len(gens)
    live = list(range(len(gens)))
    while live:
        nxt = []
        for i in live:
            try:
                next(gens[i])
                nxt.append(i)
            except StopIteration as stop:
                out[i] = stop.value
        live = nxt
    return out


def matmul(a, b, *, trans_a=False, trans_b=False, out_dtype=F32, tm=1664, tn=1408, tk=1664, name="matmul"):
    if trans_a:
        kdim, m = a.shape
    else:
        m, kdim = a.shape
    if trans_b:
        n, kb = b.shape
    else:
        kb, n = b.shape
    assert kb == kdim, (a.shape, b.shape, trans_a, trans_b)
    tm, tn, tk = _tile(m, tm), _tile(n, tn), _tile(kdim, tk)
    nk = kdim // tk
    dims = (((0 if trans_a else 1,), (1 if trans_b else 0,)), ((), ()))
    cdt = BF16

    def body(a_ref, b_ref, o_ref, *acc):
        part = lax.dot_general(a_ref[...].astype(cdt), b_ref[...].astype(cdt), dims, preferred_element_type=F32)
        if nk == 1:
            o_ref[...] = part.astype(o_ref.dtype)
            return
        k = pl.program_id(2)
        _accum(acc[0], part, k == 0)

        @pl.when(k == nk - 1)
        def _():
            o_ref[...] = acc[0][...].astype(o_ref.dtype)

    a_spec = pl.BlockSpec((tk, tm), lambda i, j, k: (k, i)) if trans_a else pl.BlockSpec((tm, tk), lambda i, j, k: (i, k))
    b_spec = pl.BlockSpec((tn, tk), lambda i, j, k: (j, k)) if trans_b else pl.BlockSpec((tk, tn), lambda i, j, k: (k, j))
    return pl.pallas_call(
        body,
        name=name,
        grid=(m // tm, n // tn, nk),
        in_specs=[a_spec, b_spec],
        out_specs=pl.BlockSpec((tm, tn), lambda i, j, k: (i, j)),
        out_shape=jax.ShapeDtypeStruct((m, n), out_dtype),
        scratch_shapes=[] if nk == 1 else [pltpu.VMEM((tm, tn), F32)],
        compiler_params=_cparams("parallel", "parallel", "arbitrary"),
    )(a, b)


def rms_fwd(x, g, *, res=None, out_dtype, name):
    n, d = x.shape
    tr = _tile(n, 640, SUB)

    def body(*refs):
        x_ref, g_ref = refs[0], refs[1]
        o_ref = refs[-1]
        xv = x_ref[...]
        y = xv * lax.rsqrt(jnp.mean(xv * xv, axis=-1, keepdims=True) + EPS) * g_ref[...]
        if res is not None:
            y = y + refs[2][...]
        o_ref[...] = y.astype(o_ref.dtype)

    row = pl.BlockSpec((tr, d), lambda i: (i, 0))
    ins = [x, g] + ([res] if res is not None else [])
    return pl.pallas_call(
        body, name=name, grid=(n // tr,),
        in_specs=[row, pl.BlockSpec((1, d), lambda i: (0, 0))] + ([row] if res is not None else []),
        out_specs=row, out_shape=jax.ShapeDtypeStruct((n, d), out_dtype),
        compiler_params=_cparams("parallel"),
    )(*ins)


def rms_bwd(x, g, dy, *, res=None, pad, name):
    n, d = x.shape
    tr = _tile(n, 640, SUB)

    def body(*refs):
        x_ref, g_ref, dy_ref = refs[:3]
        dx_ref, dg_ref = refs[-2:]
        i = pl.program_id(0)
        xv = x_ref[...]
        r = lax.rsqrt(jnp.mean(xv * xv, axis=-1, keepdims=True) + EPS)
        xh = xv * r
        dyv = dy_ref[...].astype(F32)
        gdy = dyv * g_ref[...]
        dx = r * (gdy - xh * jnp.mean(xh * gdy, axis=-1, keepdims=True))
        if res is not None:
            dx = dx + refs[3][...]
        rows = i * tr + _iota((tr, 1), 0)
        dx_ref[...] = jnp.where(rows >= pad, dx, 0.0)
        _accum(dg_ref, _colsum8(dyv * xh), i == 0)

    row = pl.BlockSpec((tr, d), lambda i: (i, 0))
    ins = [x, g, dy] + ([res] if res is not None else [])
    return pl.pallas_call(
        body, name=name, grid=(n // tr,),
        in_specs=[row, pl.BlockSpec((1, d), lambda i: (0, 0)), row] + ([row] if res is not None else []),
        out_specs=[row, pl.BlockSpec((SUB, d), lambda i: (0, 0))],
        out_shape=[jax.ShapeDtypeStruct((n, d), F32), jax.ShapeDtypeStruct((SUB, d), F32)],
        compiler_params=_cparams("arbitrary"),
    )(*ins)


def _halo_rows(dtype):
    return SUB * 4 // jnp.dtype(dtype).itemsize


def _shift_down(cur, prev, s):
    if s == 0:
        return cur
    hb = prev.shape[0]
    out = pltpu.roll(cur, s, 0)
    row = _iota(cur.shape, 0)
    for r in range(s):
        out = jnp.where(row == r, prev[hb - s + r:hb - s + r + 1, :], out)
    return out


def _shift_up(cur, next8, s):
    if s == 0:
        return cur
    tr = cur.shape[0]
    out = pltpu.roll(cur, tr - s, 0)
    row = _iota(cur.shape, 0)
    for r in range(s):
        out = jnp.where(row == tr - s + r, next8[r:r + 1, :], out)
    return out


def _conv_rows(cur, prev8, w):
    kw = w.shape[0]
    acc = w[kw - 1:kw, :] * cur
    for k in range(kw - 1):
        acc = acc + w[k:k + 1, :] * _shift_down(cur, prev8, kw - 1 - k)
    return acc


def conv_transpose(dy, w, *, dy_hi=None, out_dtype, name):
    n, c_in = dy.shape
    c = c_in if dy_hi is None else 2 * c_in
    kw = w.shape[0]
    tr = _tile(n, 640, SUB)
    tc = _tile(c_in, 512)
    nlo = c_in // tc
    hb = _halo_rows(dy.dtype)
    nbh = n // hb

    def body(*refs):
        w_ref, o_ref = refs[-2:]
        i, j = pl.program_id(0), pl.program_id(1)
        if dy_hi is None:
            cur, nxt = refs[0][...], refs[1][...]
        else:
            cur = jnp.where(j < nlo, refs[0][...], refs[2][...])
            nxt = jnp.where(j < nlo, refs[1][...], refs[3][...])
        cur = cur.astype(F32)
        nxt = jnp.where(i == pl.num_programs(0) - 1, 0.0, nxt.astype(F32))
        wv = w_ref[...]
        acc = wv[kw - 1:kw, :] * cur
        for k in range(kw - 1):
            acc = acc + wv[k:k + 1, :] * _shift_up(cur, nxt, kw - 1 - k)
        o_ref[...] = acc.astype(o_ref.dtype)

    def pair(col):
        return [pl.BlockSpec((tr, tc), lambda i, j: (i, col(j))),
                pl.BlockSpec((hb, tc), lambda i, j: (jnp.minimum((i + 1) * (tr // hb), nbh - 1), col(j)))]

    if dy_hi is None:
        srcs, specs = [dy, dy], pair(lambda j: j)
    else:
        srcs = [dy, dy, dy_hi, dy_hi]
        specs = pair(lambda j: jnp.minimum(j, nlo - 1)) + pair(lambda j: jnp.maximum(j - nlo, 0))
    return pl.pallas_call(
        body, name=name, grid=(n // tr, c // tc),
        in_specs=specs + [pl.BlockSpec((kw, tc), lambda i, j: (0, j))],
        out_specs=pl.BlockSpec((tr, tc), lambda i, j: (i, j)),
        out_shape=jax.ShapeDtypeStruct((n, c), out_dtype),
        compiler_params=_cparams("parallel", "parallel"),
    )(*srcs, w)


_GELU_C = 0.7978845608028654
_GELU_A = 0.044715


def _gelu(x):
    return 0.5 * x * (1.0 + jnp.tanh(_GELU_C * (x + _GELU_A * x * x * x)))


def _gelu_grad(x):
    th = jnp.tanh(_GELU_C * (x + _GELU_A * x * x * x))
    return 0.5 * (1.0 + th) + 0.5 * x * (1.0 - th * th) * _GELU_C * (1.0 + 3.0 * _GELU_A * x * x)


def conv_glu_fwd(u, cw, *, name):
    n, f2 = u.shape
    f = f2 // 2
    tr = _tile(n, 640, SUB)
    tc = _tile(f, 512)
    nf = f // tc
    hb = _halo_rows(u.dtype)
    r8 = tr // hb

    def body(ug_ref, uu_ref, pg_ref, pu_ref, wg_ref, wu_ref, o_ref):
        first = pl.program_id(0) == 0
        pg = jnp.where(first, 0.0, pg_ref[...].astype(F32))
        pu = jnp.where(first, 0.0, pu_ref[...].astype(F32))
        gate = _conv_rows(ug_ref[...].astype(F32), pg, wg_ref[...])
        up = _conv_rows(uu_ref[...].astype(F32), pu, wu_ref[...])
        o_ref[...] = (_gelu(gate) * up).astype(o_ref.dtype)

    prev = lambda off: pl.BlockSpec((hb, tc), lambda i, j: (jnp.maximum(i * r8 - 1, 0), j + off))
    return pl.pallas_call(
        body, name=name, grid=(n // tr, nf),
        in_specs=[pl.BlockSpec((tr, tc), lambda i, j: (i, j)), pl.BlockSpec((tr, tc), lambda i, j: (i, j + nf)),
                  prev(0), prev(nf),
                  pl.BlockSpec((FFN_CONV, tc), lambda i, j: (0, j)), pl.BlockSpec((FFN_CONV, tc), lambda i, j: (0, j + nf))],
        out_specs=pl.BlockSpec((tr, tc), lambda i, j: (i, j)),
        out_shape=jax.ShapeDtypeStruct((n, f), BF16),
        compiler_params=_cparams("parallel", "parallel"),
    )(u, u, u, u, cw, cw)


def _gelu_both(x):
    th = jnp.tanh(_GELU_C * (x + _GELU_A * x * x * x))
    return 0.5 * x * (1.0 + th), 0.5 * (1.0 + th) + 0.5 * x * (1.0 - th * th) * _GELU_C * (1.0 + 3.0 * _GELU_A * x * x)


def conv_glu_bwd(u, cw, dact, *, name):
    n, f2 = u.shape
    f = f2 // 2
    tr = _tile(n, 640, SUB)
    tc = _tile(f, 512)
    nf = f // tc
    hb = _halo_rows(u.dtype)
    r8 = tr // hb

    def body(ug_ref, uu_ref, pg_ref, pu_ref, wg_ref, wu_ref, da_ref, og_ref, ou_ref, dwg_ref, dwu_ref):
        first = pl.program_id(1) == 0
        pg = jnp.where(first, 0.0, pg_ref[...].astype(F32))
        pu = jnp.where(first, 0.0, pu_ref[...].astype(F32))
        curg, curu = ug_ref[...].astype(F32), uu_ref[...].astype(F32)
        gate = _conv_rows(curg, pg, wg_ref[...])
        up = _conv_rows(curu, pu, wu_ref[...])
        da = da_ref[...].astype(F32)
        gel, gel_grad = _gelu_both(gate)
        d_gate, d_up = da * up * gel_grad, da * gel
        og_ref[...] = d_gate.astype(og_ref.dtype)
        ou_ref[...] = d_up.astype(ou_ref.dtype)
        for k in range(FFN_CONV):
            part_g = _colsum8(d_gate * _shift_down(curg, pg, FFN_CONV - 1 - k))
            part_u = _colsum8(d_up * _shift_down(curu, pu, FFN_CONV - 1 - k))

            @pl.when(first)
            def _():
                dwg_ref[k] = part_g
                dwu_ref[k] = part_u

            @pl.when(jnp.logical_not(first))
            def _():
                dwg_ref[k] += part_g
                dwu_ref[k] += part_u

    tile = lambda off: pl.BlockSpec((tr, tc), lambda j, i: (i, j + off))
    prev = lambda off: pl.BlockSpec((hb, tc), lambda j, i: (jnp.maximum(i * r8 - 1, 0), j + off))
    wsp = lambda off: pl.BlockSpec((FFN_CONV, tc), lambda j, i: (0, j + off))
    acc = pl.BlockSpec((FFN_CONV, SUB, tc), lambda j, i: (0, 0, j))
    return pl.pallas_call(
        body, name=name, grid=(nf, n // tr),
        in_specs=[tile(0), tile(nf), prev(0), prev(nf), wsp(0), wsp(nf), tile(0)],
        out_specs=[tile(0), tile(0), acc, acc],
        out_shape=[jax.ShapeDtypeStruct((n, f), BF16)] * 2 + [jax.ShapeDtypeStruct((FFN_CONV, SUB, f), F32)] * 2,
        compiler_params=_cparams("parallel", "arbitrary"),
    )(u, u, u, u, cw, cw, dact)


def cumsum_rows(x, *, reverse, name):
    n, c = x.shape
    tr = LANE
    nb = n // tr

    def body(x_ref, o_ref, carry_ref):
        i = pl.program_id(0)

        @pl.when(i == 0)
        def _():
            carry_ref[...] = jnp.zeros_like(carry_ref)

        r, cc = _iota((tr, tr), 0), _iota((tr, tr), 1)
        tri = jnp.where((cc >= r) if reverse else (cc <= r), 1.0, 0.0).astype(F32)
        out = jnp.dot(tri, x_ref[...], precision=HI, preferred_element_type=F32) + carry_ref[...]
        o_ref[...] = out
        carry_ref[...] = out[0:1, :] if reverse else out[tr - 1:tr, :]

    idx = (lambda i: (nb - 1 - i, 0)) if reverse else (lambda i: (i, 0))
    return pl.pallas_call(
        body, name=name, grid=(nb,),
        in_specs=[pl.BlockSpec((tr, c), idx)], out_specs=pl.BlockSpec((tr, c), idx),
        out_shape=jax.ShapeDtypeStruct((n, c), F32), scratch_shapes=[pltpu.VMEM((1, c), F32)],
        compiler_params=_cparams("arbitrary"),
    )(x)


def _group_sum64(x):
    r, c = x.shape
    a, b = _iota((LANE, LANE), 0), _iota((LANE, LANE), 1)
    bd = jnp.where((a // 64) == (b // 64), 1.0, 0.0).astype(F32)
    parts = [jnp.dot(x[:, k * LANE:(k + 1) * LANE], bd, precision=HI, preferred_element_type=F32) for k in range(c // LANE)]
    return parts[0] if len(parts) == 1 else jnp.concatenate(parts, axis=1)


def attn_prep(proj, qg, kg, bf, *, hd, name):
    n = proj.shape[0]
    tr = _tile(n, 640, SUB)
    scale = ATT_HEAD_DIM ** -0.5
    nh = hd // LANE

    def body(q_ref, k_ref, v_ref, f_ref, qg_ref, kg_ref, bf_ref, qo_ref, ko_ref, vo_ref, lf_ref):
        def norm(x, g):
            ms = _group_sum64(x * x) * (1.0 / ATT_HEAD_DIM)
            return x * lax.rsqrt(ms + EPS) * g

        qo_ref[...] = (norm(q_ref[...], qg_ref[...]) * scale).astype(qo_ref.dtype)
        ko_ref[...] = norm(k_ref[...], kg_ref[...]).astype(ko_ref.dtype)
        vo_ref[...] = v_ref[...].astype(vo_ref.dtype)
        lf_ref[...] = -_softplus(-(f_ref[...] + bf_ref[...]))

    col = lambda c: pl.BlockSpec((tr, hd), lambda i: (i, c))
    vec = lambda w: pl.BlockSpec((1, w), lambda i: (0, 0))
    return pl.pallas_call(
        body, name=name, grid=(n // tr,),
        in_specs=[col(0), col(1), col(2), pl.BlockSpec((tr, LANE), lambda i: (i, 4 * nh)), vec(hd), vec(hd), vec(LANE)],
        out_specs=[col(0), col(0), col(0), pl.BlockSpec((tr, LANE), lambda i: (i, 0))],
        out_shape=[jax.ShapeDtypeStruct((n, hd), BF16)] * 3 + [jax.ShapeDtypeStruct((n, LANE), F32)],
        compiler_params=_cparams("parallel"),
    )(proj, proj, proj, proj, qg, kg, bf)


def _half_mask(shape):
    return _iota(shape, 1) < ATT_HEAD_DIM


def flash_fwd(qs, kn, v, proj, ct, *, pad, t, name):
    n, hd = qs.shape
    npair = hd // LANE
    nb = n // t
    gate0 = 3 * npair

    def body(q_ref, k_ref, v_ref, g_ref, c_ref, o_ref, og_ref, lse_ref, m_ref, acc_ref):
        i, j = pl.program_id(1), pl.program_id(2)

        @pl.when(j == 0)
        def _():
            m_ref[...] = jnp.full_like(m_ref, NEG)
            acc_ref[...] = jnp.zeros_like(acc_ref)

        def step(masked):
            q, k, vv = q_ref[...], k_ref[...], v_ref[...]
            half0 = _half_mask(q.shape)
            if masked:
                rowpos = i * t + _iota((t, t), 0)
                colpos = j * t + _iota((t, t), 1)
                mask = (colpos <= rowpos) & (colpos >= pad)

            def head(hh):
                sel = half0 if hh == 0 else jnp.logical_not(half0)
                qm = jnp.where(sel, q, jnp.zeros_like(q))
                v1 = jnp.where(sel, vv, jnp.ones_like(vv))
                s = lax.dot_general(qm, k, (((1,), (1,)), ((), ())), preferred_element_type=F32) - c_ref[0, hh:hh + 1, :]
                yield
                if masked:
                    s = jnp.where(mask, s, NEG)
                m_prev = m_ref[hh]
                m_new = jnp.maximum(m_prev, jnp.max(s, axis=1, keepdims=True))
                p = jnp.exp(s - m_new[:, 0:1])
                if masked:
                    p = jnp.where(mask, p, 0.0)
                yield
                acc_ref[hh] = jnp.exp(m_prev - m_new) * acc_ref[hh] + jnp.dot(p.astype(vv.dtype), v1, preferred_element_type=F32)
                m_ref[hh] = m_new

            _lockstep(head(hh) for hh in range(2))

        edge = (j == i) | (j == 0)

        @pl.when(edge & (j <= i))
        def _():
            step(True)

        @pl.when(jnp.logical_not(edge) & (j < i))
        def _():
            step(False)

        @pl.when(j == i)
        def _():
            half0 = _half_mask((t, LANE))
            a0, a1 = acc_ref[0], acc_ref[1]
            l = jnp.where(half0, a0[:, ATT_HEAD_DIM:ATT_HEAD_DIM + 1], a1[:, 0:1])
            acc = jnp.where(half0, a0, a1)
            m = jnp.where(half0, m_ref[0], m_ref[1])
            live = l > 0.0
            o = jnp.where(live, acc / jnp.where(live, l, 1.0), 0.0)
            o_ref[...] = o
            og_ref[...] = (o * _sigmoid(g_ref[...])).astype(og_ref.dtype)
            lse_ref[...] = jnp.where(live, m + jnp.log(jnp.where(live, l, 1.0)), 0.0)

    qspec = pl.BlockSpec((t, LANE), lambda p, i, j: (i, p))
    kspec = pl.BlockSpec((t, LANE), lambda p, i, j: (jnp.minimum(j, i), p))
    return pl.pallas_call(
        body, name=name, grid=(npair, nb, nb),
        in_specs=[qspec, kspec, kspec, pl.BlockSpec((t, LANE), lambda p, i, j: (i, gate0 + p)),
                  pl.BlockSpec((1, 2, t), lambda p, i, j: (p, 0, jnp.minimum(j, i)))],
        out_specs=[qspec, qspec, qspec],
        out_shape=[jax.ShapeDtypeStruct((n, hd), F32), jax.ShapeDtypeStruct((n, hd), BF16), jax.ShapeDtypeStruct((n, hd), F32)],
        scratch_shapes=[pltpu.VMEM((2, t, LANE), F32)] * 2,
        compiler_params=_cparams("parallel", "parallel", "arbitrary"),
    )(qs, kn, v, proj, ct)


def attn_bwd_prep(dgated, o, proj, *, hd, name):
    n = o.shape[0]
    tr = _tile(n, 640, SUB)
    gate0 = 3

    def body(dg_ref, o_ref, g_ref, do_ref, dl_ref, dgate_ref):
        dg, ov = dg_ref[...], o_ref[...]
        sg = _sigmoid(g_ref[...])
        do = dg * sg
        do_ref[...] = do.astype(do_ref.dtype)
        dl_ref[...] = _group_sum64(do * ov)
        dgate_ref[...] = dg * ov * sg * (1.0 - sg)

    row = pl.BlockSpec((tr, hd), lambda i: (i, 0))
    return pl.pallas_call(
        body, name=name, grid=(n // tr,),
        in_specs=[row, row, pl.BlockSpec((tr, hd), lambda i: (i, gate0))],
        out_specs=[row, row, row],
        out_shape=[jax.ShapeDtypeStruct((n, hd), BF16), jax.ShapeDtypeStruct((n, hd), F32), jax.ShapeDtypeStruct((n, hd), F32)],
        compiler_params=_cparams("parallel"),
    )(dgated, o, proj)


def flash_bwd(qs, kn, v, do, lse, delta, ct, *, pad, t, name):
    n, hd = qs.shape
    npair = hd // LANE
    nb = n // t

    def body(q_ref, k_ref, v_ref, do_ref, lse_ref, dl_ref, c_ref, dq_ref, dk_ref, dv_ref, dck_ref, dcr_ref, dk_acc, dv_acc, dck_acc):
        j, i = pl.program_id(1), pl.program_id(2)

        @pl.when((j == 0) & (i == 0))
        def _():
            dq_ref[...] = jnp.zeros_like(dq_ref)
            dcr_ref[...] = jnp.zeros_like(dcr_ref)

        @pl.when(i == j)
        def _():
            dk_acc[...] = jnp.zeros_like(dk_acc)
            dv_acc[...] = jnp.zeros_like(dv_acc)
            dck_acc[...] = jnp.zeros_like(dck_acc)

        def step(masked):
            q, k, vv, dov = q_ref[...], k_ref[...], v_ref[...], do_ref[...]
            half0 = _half_mask(q.shape)
            if masked:
                rowpos = i * t + _iota((t, t), 0)
                colpos = j * t + _iota((t, t), 1)
                mask = (colpos <= rowpos) & (colpos >= pad)
            nt = (((1,), (1,)), ((), ()))
            tn = (((0,), (0,)), ((), ()))
            dq_h, dk_h, dv_h = [], [], []
            for hh in range(2):
                sel = half0 if hh == 0 else jnp.logical_not(half0)
                qm = jnp.where(sel, q, jnp.zeros_like(q))
                dom = jnp.where(sel, dov, jnp.zeros_like(dov))
                q1 = jnp.where(sel, q, jnp.ones_like(q))
                k1 = jnp.where(sel, k, jnp.ones_like(k))
                x = lax.dot_general(qm, k, nt, preferred_element_type=F32) - c_ref[0, hh:hh + 1, :] - lse_ref[:, hh * 64:hh * 64 + 1]
                if masked:
                    p = jnp.where(mask, jnp.exp(jnp.where(mask, x, NEG)), 0.0)
                else:
                    p = jnp.exp(x)
                dp = lax.dot_general(dom, vv, nt, preferred_element_type=F32)
                ds = p * (dp - dl_ref[:, hh * 64:hh * 64 + 1])
                dsb, pb = ds.astype(k.dtype), p.astype(k.dtype)
                dq_h.append(jnp.dot(dsb, k1, preferred_element_type=F32))
                dk_h.append(lax.dot_general(dsb, q1, tn, preferred_element_type=F32))
                dv_h.append(lax.dot_general(pb, dov, tn, preferred_element_type=F32))
            rows = pl.ds(pl.multiple_of(i * t, t), t)
            dq_ref[rows, :] += jnp.where(half0, dq_h[0], dq_h[1])
            dcr_ref[rows, :] += jnp.where(half0, dq_h[1], dq_h[0])
            dk_acc[...] += jnp.where(half0, dk_h[0], dk_h[1])
            dck_acc[...] += jnp.where(half0, dk_h[1], dk_h[0])
            dv_acc[...] += jnp.where(half0, dv_h[0], dv_h[1])

        edge = (i == j) | (j == 0)

        @pl.when(edge & (i >= j))
        def _():
            step(True)

        @pl.when(jnp.logical_not(edge) & (i > j))
        def _():
            step(False)

        @pl.when(i == nb - 1)
        def _():
            dk_ref[...] = dk_acc[...]
            dv_ref[...] = dv_acc[...]
            dck_ref[...] = dck_acc[...]

    qspec = pl.BlockSpec((t, LANE), lambda p, j, i: (jnp.maximum(i, j), p))
    kspec = pl.BlockSpec((t, LANE), lambda p, j, i: (j, p))
    cspec = pl.BlockSpec((1, 2, t), lambda p, j, i: (p, 0, j))
    whole = pl.BlockSpec((n, LANE), lambda p, j, i: (0, p))
    return pl.pallas_call(
        body, name=name, grid=(npair, nb, nb),
        in_specs=[qspec, kspec, kspec, qspec, qspec, qspec, cspec],
        out_specs=[whole, kspec, kspec, kspec, whole],
        out_shape=[jax.ShapeDtypeStruct((n, hd), F32)] * 5,
        scratch_shapes=[pltpu.VMEM((t, LANE), F32)] * 3,
        compiler_params=_cparams("parallel", "arbitrary", "arbitrary"),
    )(qs, kn, v, do, lse, delta, ct)


def attn_in_bwd(dqs, dkn, proj, qg, kg, bf, dlogf, *, hd, pad, name):
    n = proj.shape[0]
    tr = _tile(n, 640, SUB)
    scale = ATT_HEAD_DIM ** -0.5
    nh = hd // LANE

    def body(dq_ref, dk_ref, q_ref, k_ref, f_ref, qg_ref, kg_ref, bf_ref, dl_ref, oq_ref, ok_ref, of_ref, gq_ref, gk_ref, gb_ref):
        i = pl.program_id(0)

        def back(x, g, dy):
            r = lax.rsqrt(_group_sum64(x * x) * (1.0 / ATT_HEAD_DIM) + EPS)
            xh = x * r
            gdy = dy * g
            dx = r * (gdy - xh * _group_sum64(xh * gdy) * (1.0 / ATT_HEAD_DIM))
            return dx, _colsum8(dy * xh)

        dxq, gq = back(q_ref[...], qg_ref[...], dq_ref[...] * scale)
        dxk, gk = back(k_ref[...], kg_ref[...], dk_ref[...])
        oq_ref[...] = dxq.astype(oq_ref.dtype)
        ok_ref[...] = dxk.astype(ok_ref.dtype)
        rows = i * tr + _iota((tr, 1), 0)
        dfl = jnp.where(rows >= pad, dl_ref[...] * _sigmoid(-(f_ref[...] + bf_ref[...])), 0.0)
        of_ref[...] = dfl.astype(of_ref.dtype)
        _accum(gq_ref, gq, i == 0)
        _accum(gk_ref, gk, i == 0)
        _accum(gb_ref, _colsum8(dfl), i == 0)

    row = pl.BlockSpec((tr, hd), lambda i: (i, 0))
    col = lambda c: pl.BlockSpec((tr, hd), lambda i: (i, c))
    nar = pl.BlockSpec((tr, LANE), lambda i: (i, 0))
    vec = lambda w: pl.BlockSpec((1, w), lambda i: (0, 0))
    acc = lambda w: pl.BlockSpec((SUB, w), lambda i: (0, 0))
    return pl.pallas_call(
        body, name=name, grid=(n // tr,),
        in_specs=[row, row, col(0), col(1), pl.BlockSpec((tr, LANE), lambda i: (i, 4 * nh)), vec(hd), vec(hd), vec(LANE), nar],
        out_specs=[row, row, nar, acc(hd), acc(hd), acc(LANE)],
        out_shape=[jax.ShapeDtypeStruct((n, hd), BF16)] * 2 + [jax.ShapeDtypeStruct((n, LANE), BF16),
                   jax.ShapeDtypeStruct((SUB, hd), F32), jax.ShapeDtypeStruct((SUB, hd), F32), jax.ShapeDtypeStruct((SUB, LANE), F32)],
        compiler_params=_cparams("arbitrary"),
    )(dqs, dkn, proj, proj, proj, qg, kg, bf, dlogf)


def _silu(x):
    return x * _sigmoid(x)


def _silu_grad(x):
    s = _sigmoid(x)
    return s * (1.0 + x * (1.0 - s))


def gdn_prep(proj, cw, *, hd, name):
    n = proj.shape[0]
    tr = _tile(n, 640, SUB)
    nh = hd // LANE
    r8 = tr // SUB
    qscale = DN_HEAD_DIM ** -0.5

    def body(x_ref, p_ref, w_ref, o_ref):
        i, c = pl.program_id(0), pl.program_id(1)
        prev = jnp.where(i == 0, 0.0, p_ref[...])
        s = _silu(_conv_rows(x_ref[...], prev, w_ref[...]))
        r = lax.rsqrt(jnp.sum(s * s, axis=-1, keepdims=True) + EPS)
        mult = jnp.where(c < nh, r * qscale, jnp.where(c < 2 * nh, r, 1.0))
        o_ref[...] = s * mult

    return pl.pallas_call(
        body, name=name, grid=(n // tr, 3 * nh),
        in_specs=[pl.BlockSpec((tr, LANE), lambda i, c: (i, c)),
                  pl.BlockSpec((SUB, LANE), lambda i, c: (jnp.maximum(i * r8 - 1, 0), c)),
                  pl.BlockSpec((DN_CONV, LANE), lambda i, c: (0, c))],
        out_specs=pl.BlockSpec((tr, LANE), lambda i, c: (i, c)),
        out_shape=jax.ShapeDtypeStruct((n, 3 * hd), F32),
        compiler_params=_cparams("parallel", "parallel"),
    )(proj, proj, cw)


def _chunk_tri(reverse):
    r, c = _iota((LANE, LANE), 0), _iota((LANE, LANE), 1)
    same = (r // DN_CHUNK) == (c // DN_CHUNK)
    return jnp.where(same & ((c >= r) if reverse else (c <= r)), 1.0, 0.0).astype(F32)


def gdn_gates(proj, alog, dtb, *, hd, name):
    n = proj.shape[0]
    gcol = 4 * (hd // LANE)

    def body(x_ref, a_ref, d_ref, o_ref):
        x = x_ref[...]
        lane = _iota(x.shape, 1)
        g = -jnp.exp(a_ref[...]) * _softplus(x + d_ref[...])
        gc = jnp.dot(_chunk_tri(False), jnp.where((lane >= DN_HEADS) & (lane < 2 * DN_HEADS), g, 0.0), precision=HI,
                     preferred_element_type=F32)
        o_ref[...] = jnp.where(lane < DN_HEADS, _sigmoid(x), gc)

    vec = pl.BlockSpec((1, LANE), lambda i: (0, 0))
    return pl.pallas_call(
        body, name=name, grid=(n // LANE,),
        in_specs=[pl.BlockSpec((LANE, LANE), lambda i: (i, gcol)), vec, vec],
        out_specs=pl.BlockSpec((LANE, LANE), lambda i: (i, 0)),
        out_shape=jax.ShapeDtypeStruct((n, LANE), F32),
        compiler_params=_cparams("parallel"),
    )(proj, alog, dtb)


def _mm(a, b, ca=1, cb=0):
    return lax.dot_general(a.astype(BF16), b.astype(BF16), (((ca,), (cb,)), ((), ())), preferred_element_type=F32)


def _mmh(a, b):
    return jnp.dot(a, b, precision=HI, preferred_element_type=F32)


def _gdn_common(q, k, v, beta, gc_c, gc_r):
    r, c = _iota((LANE, LANE), 0), _iota((LANE, LANE), 1)
    same = (r // DN_CHUNK) == (c // DN_CHUNK)
    incl, strict = same & (r >= c), same & (r > c)
    d = jnp.exp(jnp.where(incl, gc_c - gc_r, NEG))
    kk = _mm(k, k, 1, 1)
    qk = _mm(q, k, 1, 1)
    yield
    ahat = jnp.where(strict, kk * d, 0.0)
    a = ahat * beta
    eye = jnp.where(r == c, 1.0, 0.0).astype(F32)
    t = eye - a
    pw = _mmh(a, a)
    yield
    for step in range(5):
        t = t + _mmh(t, pw)
        if step < 4:
            pw = _mmh(pw, pw)
        yield
    row = _iota((LANE, 1), 0)
    gl0 = jnp.sum(jnp.where(row == DN_CHUNK - 1, gc_c, 0.0), axis=0, keepdims=True)
    gl1 = jnp.sum(jnp.where(row == LANE - 1, gc_c, 0.0), axis=0, keepdims=True)
    gam = jnp.exp(gc_c)
    lam = jnp.exp(jnp.where(row < DN_CHUNK, gl0, gl1) - gc_c)
    kb, vb = k * (beta * gam), v * beta
    cm = dict(incl=incl, strict=strict, d=d, kk=kk, ahat=ahat, t=t, gam=gam, lam=lam, kb=kb, vb=vb, w=_mm(t, kb), u0=_mm(t, vb),
              qk=qk, pm=jnp.where(incl, qk * d, 0.0), qg=q * gam, kl=k * lam, g0=jnp.exp(gl0), g1=jnp.exp(gl1))
    yield
    return cm


def _gdn_states(cm, s0):
    c = DN_CHUNK
    u_a = cm["u0"][:c] - _mm(cm["w"][:c], s0, 1, 1)
    yield
    s1 = cm["g0"] * s0 + _mm(u_a, cm["kl"][:c], 0, 0)
    yield
    u_b = cm["u0"][c:] - _mm(cm["w"][c:], s1, 1, 1)
    yield
    s2 = cm["g1"] * s1 + _mm(u_b, cm["kl"][c:], 0, 0)
    yield
    return u_a, s1, u_b, s2


def gdn_chunk_fwd(qkv, bg, bgt, proj, ogain, *, hd, name):
    n = qkv.shape[0]
    nb = n // LANE
    nh = hd // LANE
    c = DN_CHUNK

    def body(q_ref, k_ref, v_ref, bg_ref, bgt_ref, g_ref, gain_ref, o_ref, og_ref, hist_ref, s_ref):
        @pl.when(pl.program_id(0) == 0)
        def _():
            s_ref[...] = jnp.zeros_like(s_ref)

        hist_ref[0] = s_ref[...]

        def head(h):
            cols = slice(h * LANE, (h + 1) * LANE)
            cm = yield from _gdn_common(q_ref[:, cols], k_ref[:, cols], v_ref[:, cols], bg_ref[:, h:h + 1],
                                        bg_ref[:, nh + h:nh + h + 1], bgt_ref[nh + h:nh + h + 1, :])
            s0 = s_ref[h]
            u_a, s1, u_b, s2 = yield from _gdn_states(cm, s0)
            u_all = jnp.concatenate([u_a, u_b], axis=0)
            o = jnp.concatenate([_mm(cm["qg"][:c], s0, 1, 1), _mm(cm["qg"][c:], s1, 1, 1)], axis=0) + _mm(cm["pm"], u_all)
            s_ref[h] = s2
            o_ref[:, cols] = o
            rn = lax.rsqrt(jnp.mean(o * o, axis=-1, keepdims=True) + EPS)
            og_ref[:, cols] = (o * rn * gain_ref[...] * _silu(g_ref[:, cols])).astype(og_ref.dtype)

        _lockstep(head(h) for h in range(nh))

    col = lambda cc: pl.BlockSpec((LANE, hd), lambda b: (b, cc))
    return pl.pallas_call(
        body, name=name, grid=(nb,),
        in_specs=[col(0), col(1), col(2), pl.BlockSpec((LANE, LANE), lambda b: (b, 0)),
                  pl.BlockSpec((2 * nh, LANE), lambda b: (0, b)), pl.BlockSpec((LANE, hd), lambda b: (b, 3)),
                  pl.BlockSpec((1, LANE), lambda b: (0, 0))],
        out_specs=[col(0), col(0), pl.BlockSpec((1, nh, LANE, LANE), lambda b: (b, 0, 0, 0))],
        out_shape=[jax.ShapeDtypeStruct((n, hd), F32), jax.ShapeDtypeStruct((n, hd), BF16),
                   jax.ShapeDtypeStruct((nb, nh, LANE, LANE), F32)],
        scratch_shapes=[pltpu.VMEM((nh, LANE, LANE), F32)],
        compiler_params=_cparams("arbitrary"),
    )(qkv, qkv, qkv, bg, bgt, proj, ogain)


def gdn_chunk_bwd(qkv, bg, bgt, proj, ogain, o_raw, dog, hist, *, hd, name):
    n = qkv.shape[0]
    nb = n // LANE
    nh = hd // LANE
    c = DN_CHUNK

    def body(q_ref, k_ref, v_ref, bg_ref, bgt_ref, g_ref, gain_ref, o_ref, dog_ref, hist_ref,
             dq_ref, dk_ref, dv_ref, dgate_ref, dbg_ref, dgt_ref, dgain_ref, ds_ref):
        first = pl.program_id(0) == 0

        @pl.when(first)
        def _():
            ds_ref[...] = jnp.zeros_like(ds_ref)

        lane = _iota((LANE, LANE), 1)
        row = _iota((LANE, 1), 0)

        def head(h):
            cols = slice(h * LANE, (h + 1) * LANE)
            q, k, v = q_ref[:, cols], k_ref[:, cols], v_ref[:, cols]
            beta = bg_ref[:, h:h + 1]
            cm = yield from _gdn_common(q, k, v, beta, bg_ref[:, nh + h:nh + h + 1], bgt_ref[nh + h:nh + h + 1, :])
            s0 = hist_ref[0, h]
            u_a, s1, u_b, _ = yield from _gdn_states(cm, s0)
            u_all = jnp.concatenate([u_a, u_b], axis=0)
            o, gate, d_out, gain = o_ref[:, cols], g_ref[:, cols], dog_ref[:, cols], gain_ref[...]
            rn = lax.rsqrt(jnp.mean(o * o, axis=-1, keepdims=True) + EPS)
            xh = o * rn
            d_on = d_out * _silu(gate)
            dgate_ref[:, cols] = d_out * xh * gain * _silu_grad(gate)
            dgain = _colsum8(d_on * xh)
            gdy = d_on * gain
            d_o = rn * (gdy - xh * jnp.mean(xh * gdy, axis=-1, keepdims=True))
            pt_do = _mm(cm["pm"], d_o, 0, 0)
            ds_in = ds_ref[h]
            yield
            du_b = _mm(cm["kl"][c:], ds_in, 1, 1) + pt_do[c:]
            dkl_b = _mm(u_b, ds_in)
            dqg_b = _mm(d_o[c:], s1)
            dg1 = jnp.sum(jnp.sum(ds_in * s1, axis=1, keepdims=True), axis=0, keepdims=True)
            dw_b = -_mm(du_b, s1)
            yield
            ds_mid = cm["g1"] * ds_in + _mm(d_o[c:], cm["qg"][c:], 0, 0) - _mm(du_b, cm["w"][c:], 0, 0)
            yield
            du_a = _mm(cm["kl"][:c], ds_mid, 1, 1) + pt_do[:c]
            dkl_a = _mm(u_a, ds_mid)
            dqg_a = _mm(d_o[:c], s0)
            dg0 = jnp.sum(jnp.sum(ds_mid * s0, axis=1, keepdims=True), axis=0, keepdims=True)
            yield
            dw_a = -_mm(du_a, s0)
            ds_ref[h] = cm["g0"] * ds_mid + _mm(d_o[:c], cm["qg"][:c], 0, 0) - _mm(du_a, cm["w"][:c], 0, 0)
            du = jnp.concatenate([du_a, du_b], axis=0)
            dkl = jnp.concatenate([dkl_a, dkl_b], axis=0)
            dqg = jnp.concatenate([dqg_a, dqg_b], axis=0)
            dw = jnp.concatenate([dw_a, dw_b], axis=0)
            t, d, gam, lam = cm["t"], cm["d"], cm["gam"], cm["lam"]
            dp = jnp.where(cm["incl"], _mm(d_o, u_all, 1, 1), 0.0)
            dt = _mm(dw, cm["kb"], 1, 1) + _mm(du, cm["vb"], 1, 1)
            dkb = _mm(t, dw, 0, 0)
            dvb = _mm(t, du, 0, 0)
            yield
            x_t = _mm(t, dt, 0, 0)
            yield
            da = jnp.where(cm["strict"], -_mm(x_t, t, 1, 1), 0.0)
            yield
            kb_k = jnp.sum(dkb * k, axis=1, keepdims=True)
            dbeta = jnp.sum(da * cm["ahat"], axis=1, keepdims=True) + gam * kb_k + jnp.sum(dvb * v, axis=1, keepdims=True)
            dahat = da * beta
            dkk = dahat * d
            dqk = dp * d
            e = (dahat * cm["kk"] + dp * cm["qk"]) * d
            dk_ref[:, cols] = (_mm(dkk, k) + _mm(dkk, k, 0, 0) + _mm(dqk, q, 0, 0) + dkb * (beta * gam) + dkl * lam)
            dq_ref[:, cols] = _mm(dqk, k) + dqg * gam
            dv_ref[:, cols] = dvb * beta
            dgam = beta * kb_k + jnp.sum(dqg * q, axis=1, keepdims=True)
            dlam_lam = jnp.sum(dkl * k, axis=1, keepdims=True) * lam
            dgl0 = jnp.sum(jnp.where(row < c, dlam_lam, 0.0), axis=0, keepdims=True) + dg0 * cm["g0"]
            dgl1 = jnp.sum(jnp.where(row >= c, dlam_lam, 0.0), axis=0, keepdims=True) + dg1 * cm["g1"]
            dgc = (jnp.sum(e, axis=1, keepdims=True) + dgam * gam - dlam_lam
                   + jnp.where(row == c - 1, dgl0, 0.0) + jnp.where(row == LANE - 1, dgl1, 0.0))
            dgt_ref[h:h + 1, :] = -jnp.sum(e, axis=0, keepdims=True)
            return jnp.where(lane == h, dbeta, 0.0) + jnp.where(lane == nh + h, dgc, 0.0), dgain

        parts = _lockstep(head(h) for h in range(nh))
        dbg_ref[...] = sum(p[0] for p in parts)
        _accum(dgain_ref, sum(p[1] for p in parts), first)

    rev = lambda b: nb - 1 - b
    col = lambda cc: pl.BlockSpec((LANE, hd), lambda b: (rev(b), cc))
    return pl.pallas_call(
        body, name=name, grid=(nb,),
        in_specs=[col(0), col(1), col(2), pl.BlockSpec((LANE, LANE), lambda b: (rev(b), 0)),
                  pl.BlockSpec((2 * nh, LANE), lambda b: (0, rev(b))), pl.BlockSpec((LANE, hd), lambda b: (rev(b), 3)),
                  pl.BlockSpec((1, LANE), lambda b: (0, 0)), col(0), col(0),
                  pl.BlockSpec((1, nh, LANE, LANE), lambda b: (rev(b), 0, 0, 0))],
        out_specs=[col(0), col(0), col(0), col(0), pl.BlockSpec((LANE, LANE), lambda b: (rev(b), 0)),
                   pl.BlockSpec((nh, LANE), lambda b: (0, rev(b))), pl.BlockSpec((SUB, LANE), lambda b: (0, 0))],
        out_shape=[jax.ShapeDtypeStruct((n, hd), F32)] * 4 + [jax.ShapeDtypeStruct((n, LANE), F32),
                   jax.ShapeDtypeStruct((nh, n), F32), jax.ShapeDtypeStruct((SUB, LANE), F32)],
        scratch_shapes=[pltpu.VMEM((nh, LANE, LANE), F32)],
        compiler_params=_cparams("arbitrary"),
    )(qkv, qkv, qkv, bg, bgt, proj, ogain, o_raw, dog, hist)


def gdn_gates_bwd(proj, alog, dtb, dbg, *, hd, pad, name):
    n = proj.shape[0]
    gcol = 4 * (hd // LANE)

    def body(x_ref, a_ref, d_ref, dbg_ref, o_ref, da_ref, dd_ref):
        i = pl.program_id(0)
        x = x_ref[...]
        lane = _iota(x.shape, 1)
        rows = i * LANE + _iota((LANE, 1), 0)
        isg = (lane >= DN_HEADS) & (lane < 2 * DN_HEADS)
        dbgv = jnp.where(rows >= pad, dbg_ref[...], 0.0)
        dg = jnp.dot(_chunk_tri(True), jnp.where(isg, dbgv, 0.0), precision=HI, preferred_element_type=F32)
        ea = jnp.exp(a_ref[...])
        z = x + d_ref[...]
        dg = jnp.where(rows >= pad, dg, 0.0)
        dz = jnp.where(isg, dg * (-ea) * _sigmoid(z), 0.0)
        sb = _sigmoid(x)
        o_ref[...] = jnp.where(lane < DN_HEADS, dbgv * sb * (1.0 - sb), dz).astype(o_ref.dtype)
        _accum(da_ref, _colsum8(jnp.where(isg, dg * (-ea) * _softplus(z), 0.0)), i == 0)
        _accum(dd_ref, _colsum8(dz), i == 0)

    vec = pl.BlockSpec((1, LANE), lambda i: (0, 0))
    blk = pl.BlockSpec((LANE, LANE), lambda i: (i, 0))
    acc = pl.BlockSpec((SUB, LANE), lambda i: (0, 0))
    return pl.pallas_call(
        body, name=name, grid=(n // LANE,),
        in_specs=[pl.BlockSpec((LANE, LANE), lambda i: (i, gcol)), vec, vec, blk],
        out_specs=[blk, acc, acc],
        out_shape=[jax.ShapeDtypeStruct((n, LANE), BF16), jax.ShapeDtypeStruct((SUB, LANE), F32), jax.ShapeDtypeStruct((SUB, LANE), F32)],
        compiler_params=_cparams("arbitrary"),
    )(proj, alog, dtb, dbg)


def gdn_prep_bwd(proj, cw, dqkv, *, hd, name):
    n = proj.shape[0]
    tr = _tile(n, 640, SUB)
    nh = hd // LANE
    r8 = tr // SUB
    qscale = DN_HEAD_DIM ** -0.5

    def body(x_ref, p_ref, w_ref, dq_ref, dk_ref, dv_ref, o_ref, dw_ref):
        c, i = pl.program_id(0), pl.program_id(1)
        first = i == 0
        prev = jnp.where(first, 0.0, p_ref[...])
        cur = x_ref[...]
        cv = _conv_rows(cur, prev, w_ref[...])
        s = _silu(cv)
        r = lax.rsqrt(jnp.sum(s * s, axis=-1, keepdims=True) + EPS)
        y = s * r
        dy = jnp.where(c < nh, dq_ref[...] * qscale, dk_ref[...])
        ds_norm = r * (dy - y * jnp.sum(dy * y, axis=-1, keepdims=True))
        dcv = jnp.where(c < 2 * nh, ds_norm, dv_ref[...]) * _silu_grad(cv)
        o_ref[...] = dcv
        for k in range(DN_CONV):
            part = _colsum8(dcv * _shift_down(cur, prev, DN_CONV - 1 - k))

            @pl.when(first)
            def _():
                dw_ref[k] = part

            @pl.when(jnp.logical_not(first))
            def _():
                dw_ref[k] += part

    blk = lambda f: pl.BlockSpec((tr, LANE), f)
    return pl.pallas_call(
        body, name=name, grid=(3 * nh, n // tr),
        in_specs=[blk(lambda c, i: (i, c)), pl.BlockSpec((SUB, LANE), lambda c, i: (jnp.maximum(i * r8 - 1, 0), c)),
                  pl.BlockSpec((DN_CONV, LANE), lambda c, i: (0, c)),
                  blk(lambda c, i: (i, jnp.minimum(c, nh - 1))), blk(lambda c, i: (i, jnp.clip(c - nh, 0, nh - 1))),
                  blk(lambda c, i: (i, jnp.clip(c - 2 * nh, 0, nh - 1)))],
        out_specs=[blk(lambda c, i: (i, c)), pl.BlockSpec((DN_CONV, SUB, LANE), lambda c, i: (0, 0, c))],
        out_shape=[jax.ShapeDtypeStruct((n, 3 * hd), F32), jax.ShapeDtypeStruct((DN_CONV, SUB, 3 * hd), F32)],
        compiler_params=_cparams("parallel", "arbitrary"),
    )(proj, proj, cw, *dqkv)


def loss_head(h, target, *, x0, name):
    n, d = h.shape
    tr = LANE
    nb0 = x0 // tr

    def body(h_ref, t_ref, dh_ref, sq_ref):
        i = pl.program_id(0)
        live = i >= nb0
        err = jnp.where(live, h_ref[...] - t_ref[...], 0.0)
        dh_ref[...] = err * (1.0 / d)
        _accum(sq_ref, _colsum8(err * err), i == 0)

    row = pl.BlockSpec((tr, d), lambda i: (i, 0))
    return pl.pallas_call(
        body, name=name, grid=(n // tr,),
        in_specs=[row, pl.BlockSpec((tr, d), lambda i: (jnp.maximum(i - nb0, 0), 0))],
        out_specs=[row, pl.BlockSpec((SUB, d), lambda i: (0, 0))],
        out_shape=[jax.ShapeDtypeStruct((n, d), F32), jax.ShapeDtypeStruct((SUB, d), F32)],
        compiler_params=_cparams("arbitrary"),
    )(h, target)


def adamw(w, g, m, v, *, name):
    r, c = w.shape
    tr = _tile(r, 512, SUB) if r % SUB == 0 else r
    c1 = 1.0 / (1.0 - ADAM_B1 ** ADAM_STEP)
    c2 = 1.0 / (1.0 - ADAM_B2 ** ADAM_STEP)

    def body(w_ref, g_ref, m_ref, v_ref, d_ref, mo_ref, vo_ref):
        gv = g_ref[...]
        mn = ADAM_B1 * m_ref[...] + (1.0 - ADAM_B1) * gv
        vn = ADAM_B2 * v_ref[...] + (1.0 - ADAM_B2) * (gv * gv)
        d_ref[...] = -ADAM_LR * ((mn * c1) / (jnp.sqrt(vn * c2) + ADAM_EPS) + ADAM_WD * w_ref[...])
        mo_ref[...] = mn
        vo_ref[...] = vn

    blk = pl.BlockSpec((tr, c), lambda i: (i, 0))
    return pl.pallas_call(
        body, name=name, grid=(r // tr,), in_specs=[blk] * 4, out_specs=[blk] * 3,
        out_shape=[jax.ShapeDtypeStruct((r, c), F32)] * 3, compiler_params=_cparams("parallel"),
    )(w, g, m, v)


_BIG = ("attn_w_in", "attn_w_out", "dn_w_in", "dn_w_out", "ffn_w_up", "ffn_w_down")


def _row(v, width=None):
    v = v.astype(F32).reshape(1, -1)
    if width is not None and v.shape[1] < width:
        v = jnp.pad(v, ((0, 0), (0, width - v.shape[1])))
    return v


def _fold8(p):
    return jnp.sum(p, axis=-2)


def local_step(x, target, w):
    seq, d = x.shape
    pad = (-(N_META + seq)) % LANE
    x0 = pad + N_META
    n = x0 + seq
    depth = w["g_pre"].shape[0]
    hd_a = ATT_HEADS * ATT_HEAD_DIM
    hd_d = DN_HEADS * DN_HEAD_DIM
    t_att = _tile(n, 640)
    h = jnp.concatenate([jnp.zeros((pad, d), F32), w["meta"].astype(F32), x], axis=0)
    saved = []
    for i in range(depth):
        j = i // 2
        s = dict(h=h)
        s["a"] = rms_fwd(h, _row(w["g_pre"][i]), out_dtype=BF16, name="rms_pre")
        if i % 2 == 0:
            s["proj"] = proj = matmul(s["a"], w["attn_w_in"][j], name="mm_attn_in")
            qg, kg = _row(jnp.tile(w["attn_qg"][j], ATT_HEADS)), _row(jnp.tile(w["attn_kg"][j], ATT_HEADS))
            bf = _row(w["attn_b"][j], LANE)
            s["qs"], s["kn"], s["v"], logf = attn_prep(proj, qg, kg, bf, hd=hd_a, name="attn_prep")
            c = cumsum_rows(logf, reverse=False, name="cumsum_fwd")
            s["ct"] = c[:, :ATT_HEADS].T.reshape(ATT_HEADS // 2, 2, n)
            s["o"], s["og"], s["lse"] = flash_fwd(s["qs"], s["kn"], s["v"], proj, s["ct"], pad=pad, t=t_att, name="flash_fwd")
            s["m"] = matmul(s["og"], w["attn_w_out"][j], name="mm_attn_out")
        else:
            s["proj"] = proj = matmul(s["a"], w["dn_w_in"][j], name="mm_dn_in")
            s["qkv"] = gdn_prep(proj, w["dn_conv"][j], hd=hd_d, name="gdn_prep")
            alog = jnp.pad(_row(w["dn_alog"][j]), ((0, 0), (DN_HEADS, LANE - 2 * DN_HEADS)))
            dtb = jnp.pad(_row(w["dn_dtb"][j]), ((0, 0), (DN_HEADS, LANE - 2 * DN_HEADS)))
            s["bg"] = gdn_gates(proj, alog, dtb, hd=hd_d, name="gdn_gates")
            s["bgt"] = s["bg"][:, :2 * DN_HEADS].T
            s["o"], s["og"], s["hist"] = gdn_chunk_fwd(s["qkv"], s["bg"], s["bgt"], proj, _row(w["dn_og"][j]), hd=hd_d, name="gdn_fwd")
            s["m"] = matmul(s["og"], w["dn_w_out"][j], name="mm_dn_out")
        s["h_mid"] = rms_fwd(s["m"], _row(w["g_post"][i]), res=h, out_dtype=F32, name="rms_post")
        s["b"] = rms_fwd(s["h_mid"], _row(w["g_fpre"][i]), out_dtype=BF16, name="rms_fpre")
        s["u"] = matmul(s["b"], w["ffn_w_up"][i], out_dtype=BF16, name="mm_ffn_up")
        s["act"] = conv_glu_fwd(s["u"], w["ffn_conv"][i], name="ffn_glu")
        s["f"] = matmul(s["act"], w["ffn_w_down"][i], name="mm_ffn_down")
        h = rms_fwd(s["f"], _row(w["g_fpost"][i]), res=s["h_mid"], out_dtype=F32, name="rms_fpost")
        saved.append(s)

    dh, sq = loss_head(h, target, x0=x0, name="loss_head")
    loss = 0.5 * jnp.sum(sq) / d

    g = {k: [None] * depth for k in ("g_pre", "g_post", "g_fpre", "g_fpost", "ffn_w_up", "ffn_conv", "ffn_w_down")}
    for k in ("attn_w_in", "attn_b", "attn_qg", "attn_kg", "attn_w_out", "dn_w_in", "dn_conv", "dn_alog", "dn_dtb", "dn_og", "dn_w_out"):
        g[k] = [None] * (depth // 2)
    for i in reversed(range(depth)):
        j = i // 2
        s = saved[i]
        proj = s["proj"]
        df, p8 = rms_bwd(s["f"], _row(w["g_fpost"][i]), dh, pad=pad, name="rms_fpost_bwd")
        g["g_fpost"][i] = _fold8(p8)
        dact = matmul(df, w["ffn_w_down"][i], trans_b=True, out_dtype=BF16, name="mm_ffn_down_dx")
        g["ffn_w_down"][i] = matmul(s["act"], df, trans_a=True, out_dtype=BF16, name="mm_ffn_down_dw")
        d_gate, d_up, pg8, pu8 = conv_glu_bwd(s["u"], w["ffn_conv"][i], dact, name="ffn_glu_bwd")
        g["ffn_conv"][i] = jnp.concatenate([_fold8(pg8), _fold8(pu8)], axis=-1)
        du = conv_transpose(d_gate, w["ffn_conv"][i], dy_hi=d_up, out_dtype=BF16, name="ffn_conv_t")
        db = matmul(du, w["ffn_w_up"][i], trans_b=True, name="mm_ffn_up_dx")
        g["ffn_w_up"][i] = matmul(s["b"], du, trans_a=True, out_dtype=BF16, name="mm_ffn_up_dw")
        dh_mid, p8 = rms_bwd(s["h_mid"], _row(w["g_fpre"][i]), db, res=dh, pad=pad, name="rms_fpre_bwd")
        g["g_fpre"][i] = _fold8(p8)
        dm, p8 = rms_bwd(s["m"], _row(w["g_post"][i]), dh_mid, pad=pad, name="rms_post_bwd")
        g["g_post"][i] = _fold8(p8)
        if i % 2 == 0:
            g["attn_w_out"][j] = matmul(s["og"], dm, trans_a=True, out_dtype=BF16, name="mm_attn_out_dw")
            dgated = matmul(dm, w["attn_w_out"][j], trans_b=True, name="mm_attn_out_dx")
            do, delta, dgate = attn_bwd_prep(dgated, s["o"], proj, hd=hd_a, name="attn_bwd_prep")
            dqs, dkn, dv, dck, dcr = flash_bwd(s["qs"], s["kn"], s["v"], do, s["lse"], delta, s["ct"], pad=pad, t=t_att, name="flash_bwd")
            dc = (dcr - dck)[:, ::ATT_HEAD_DIM].reshape(n, ATT_HEADS // 2, 2)[:, :, ::-1].reshape(n, ATT_HEADS)
            dc = jnp.pad(dc, ((0, 0), (0, LANE - ATT_HEADS)))
            dlogf = cumsum_rows(dc, reverse=True, name="cumsum_bwd")
            qg, kg = _row(jnp.tile(w["attn_qg"][j], ATT_HEADS)), _row(jnp.tile(w["attn_kg"][j], ATT_HEADS))
            bf = _row(w["attn_b"][j], LANE)
            dq_raw, dk_raw, dfl, gq8, gk8, gb8 = attn_in_bwd(dqs, dkn, proj, qg, kg, bf, dlogf, hd=hd_a, pad=pad, name="attn_in_bwd")
            g["attn_qg"][j] = _fold8(gq8).reshape(ATT_HEADS, ATT_HEAD_DIM).sum(axis=0)
            g["attn_kg"][j] = _fold8(gk8).reshape(ATT_HEADS, ATT_HEAD_DIM).sum(axis=0)
            g["attn_b"][j] = _fold8(gb8)[:ATT_HEADS]
            dproj = jnp.concatenate([dq_raw, dk_raw, dv.astype(BF16), dgate.astype(BF16), dfl], axis=1)
            w_in, key = w["attn_w_in"][j], "attn_w_in"
        else:
            g["dn_w_out"][j] = matmul(s["og"], dm, trans_a=True, out_dtype=BF16, name="mm_dn_out_dw")
            dgated = matmul(dm, w["dn_w_out"][j], trans_b=True, name="mm_dn_out_dx")
            alog = jnp.pad(_row(w["dn_alog"][j]), ((0, 0), (DN_HEADS, LANE - 2 * DN_HEADS)))
            dtb = jnp.pad(_row(w["dn_dtb"][j]), ((0, 0), (DN_HEADS, LANE - 2 * DN_HEADS)))
            dq, dk, dv, dgate, dbg, dgt, gain8 = gdn_chunk_bwd(s["qkv"], s["bg"], s["bgt"], proj, _row(w["dn_og"][j]), s["o"], dgated,
                                                              s["hist"], hd=hd_d, name="gdn_bwd")
            g["dn_og"][j] = _fold8(gain8)
            dbg = dbg + jnp.pad(dgt.T, ((0, 0), (DN_HEADS, LANE - 2 * DN_HEADS)))
            dgl, da8, dd8 = gdn_gates_bwd(proj, alog, dtb, dbg, hd=hd_d, pad=pad, name="gdn_gates_bwd")
            g["dn_alog"][j] = _fold8(da8)[DN_HEADS:2 * DN_HEADS]
            g["dn_dtb"][j] = _fold8(dd8)[DN_HEADS:2 * DN_HEADS]
            dcv, p8 = gdn_prep_bwd(proj, w["dn_conv"][j], (dq, dk, dv), hd=hd_d, name="gdn_prep_bwd")
            g["dn_conv"][j] = _fold8(p8)
            dqkv = conv_transpose(dcv, w["dn_conv"][j], out_dtype=BF16, name="gdn_conv_t")
            dproj = jnp.concatenate([dqkv, dgate.astype(BF16), dgl], axis=1)
            w_in, key = w["dn_w_in"][j], "dn_w_in"
        da = matmul(dproj, w_in, trans_b=True, name="mm_in_dx")
        g[key][j] = matmul(s["a"], dproj, trans_a=True, out_dtype=BF16, name="mm_in_dw")
        dh, p8 = rms_bwd(s["h"], _row(w["g_pre"][i]), da, res=dh_mid, pad=pad, name="rms_pre_bwd")
        g["g_pre"][i] = _fold8(p8)

    grads = {k: (v if k in _BIG else jnp.stack(v)) for k, v in g.items()}
    grads["meta"] = dh[pad:x0]
    return loss, dh[x0:], grads


_ANY = pl.BlockSpec(memory_space=pl.ANY)


def _mesh_place():
    x, y, c = lax.axis_index("x"), lax.axis_index("y"), lax.axis_index("c")
    return x, y, c, 4 * x + 2 * y + c


def _peer(x, y, c, k):
    px, py, pc = (1 - x if k & 4 else x), (1 - y if k & 2 else y), (1 - c if k & 1 else c)
    return (px, py, pc), 4 * px + 2 * py + pc


def _window_blocks(shard):
    return max(-(-(shard * (d + 1)) // LANE) - (shard * d) // LANE for d in range(N_DEV))


def _sds(shape, dtype):
    return jax.ShapeDtypeStruct(tuple(shape), dtype)


def _plan_gather(buf):
    return _sds((N_DEV,) + buf.shape, buf.dtype), (lambda r, i: r), (lambda o, i: o.at[i])


def _plan_scatter(buf):
    return _sds(buf.shape, buf.dtype), (lambda r, i: r.at[i]), (lambda o, i: o.at[i])


def _plan_gather_rows(buf):
    l, r, c = buf.shape
    return _sds((l, N_DEV * r, c), buf.dtype), (lambda ref, i: ref), (lambda o, i: o.at[:, pl.ds(pl.multiple_of(i * r, SUB), r), :])


def _plan_scatter_rows(buf):
    r, c = buf.shape[0] // N_DEV, buf.shape[1]
    return _sds((N_DEV, r, c), buf.dtype), (lambda ref, i: ref.at[pl.ds(pl.multiple_of(i * r, SUB), r), :]), (lambda o, i: o.at[i])


def _plan_scatter_cols(buf, shard):
    ww = _window_blocks(shard) * LANE
    src = lambda ref, i: ref.at[:, pl.ds(pl.multiple_of((shard * i) // LANE * LANE, LANE), ww)]
    return _sds((N_DEV, buf.shape[0], ww), buf.dtype), src, (lambda o, i: o.at[i])


def exchange(bufs, plans, *, name, relay=False):
    nbuf = len(bufs)
    far = (2, 4, 6)

    def body(*refs):
        ins, outs = refs[:nbuf], refs[nbuf:2 * nbuf]
        send_sems, recv_sems, loc_sems = refs[2 * nbuf:2 * nbuf + 3]
        x, y, c, me = _mesh_place()
        local = [pltpu.make_async_copy(plans[b][1](ins[b], me), plans[b][2](outs[b], me), loc_sems.at[b]) for b in range(nbuf)]
        for cp in local:
            cp.start()
        sends, recvs = [], {}
        for k in ((1,) + far if relay else range(1, N_DEV)):
            peer, pidx = _peer(x, y, c, k)
            for b in range(nbuf):
                sems = dict(send_sem=send_sems.at[b, k - 1], recv_sem=recv_sems.at[b, k - 1], device_id=peer,
                            device_id_type=pl.DeviceIdType.MESH)
                cp = pltpu.make_async_remote_copy(src_ref=plans[b][1](ins[b], pidx), dst_ref=plans[b][2](outs[b], me), **sems)
                cp.start()
                sends.append(cp)
                recvs[b, k] = pltpu.make_async_remote_copy(src_ref=plans[b][1](ins[b], pidx), dst_ref=plans[b][2](outs[b], pidx), **sems)
        if relay:
            fwd_send, fwd_recv = refs[2 * nbuf + 3:]
            sibling, _ = _peer(x, y, c, 1)
            for j, k in enumerate(far):
                _, pidx = _peer(x, y, c, k)
                _, qidx = _peer(x, y, c, k + 1)
                for b in range(nbuf):
                    recvs.pop((b, k)).wait_recv()
                    sems = dict(send_sem=fwd_send.at[b, j], recv_sem=fwd_recv.at[b, j], device_id=sibling,
                                device_id_type=pl.DeviceIdType.MESH)
                    landed = plans[b][2](outs[b], pidx)
                    cp = pltpu.make_async_remote_copy(src_ref=landed, dst_ref=landed, **sems)
                    cp.start()
                    sends.append(cp)
                    there = plans[b][2](outs[b], qidx)
                    recvs[b, -k] = pltpu.make_async_remote_copy(src_ref=there, dst_ref=there, **sems)
        for cp in recvs.values():
            cp.wait_recv()
        for cp in sends:
            cp.wait_send()
        for cp in local:
            cp.wait()

    sem_shapes = [pltpu.SemaphoreType.DMA((nbuf, N_DEV - 1)), pltpu.SemaphoreType.DMA((nbuf, N_DEV - 1)), pltpu.SemaphoreType.DMA((nbuf,))]
    if relay:
        sem_shapes += [pltpu.SemaphoreType.DMA((nbuf, len(far))), pltpu.SemaphoreType.DMA((nbuf, len(far)))]
    return pl.pallas_call(
        body, name=name, in_specs=[_ANY] * nbuf, out_specs=[_ANY] * nbuf, out_shape=[p[0] for p in plans],
        scratch_shapes=sem_shapes,
        compiler_params=pltpu.CompilerParams(has_side_effects=True),
    )(*bufs)


def slot_sum(x, *, name):
    _, r, c = x.shape
    tr = _tile(r, 512, 16)

    def body(x_ref, o_ref):
        acc = x_ref[0].astype(F32)
        for d in range(1, N_DEV):
            acc = acc + x_ref[d].astype(F32)
        o_ref[...] = acc

    return pl.pallas_call(
        body, name=name, grid=(r // tr,), in_specs=[pl.BlockSpec((N_DEV, tr, c), lambda i: (0, i, 0))],
        out_specs=pl.BlockSpec((tr, c), lambda i: (i, 0)), out_shape=jax.ShapeDtypeStruct((r, c), F32),
        compiler_params=_cparams("parallel"),
    )(x)


def assemble_cols(win, shard, *, name):
    _, r, ww = win.shape
    wb = ww // LANE
    nbo = -(-(N_DEV * shard) // LANE)
    tab = []
    for b in range(nbo):
        hits = [(d, b - (shard * d) // LANE) for d in range(N_DEV) if 0 <= b - (shard * d) // LANE < wb]
        assert 1 <= len(hits) <= 2, (b, hits)
        tab.append([hits[0][0], hits[0][1], hits[-1][0], hits[-1][1], len(hits) - 1])
    tab = jnp.array(tab, jnp.int32).T
    tr = _tile(r, 512, 16)

    def body(tab_ref, a_ref, b_ref, o_ref):
        two = tab_ref[4, pl.program_id(1)] > 0
        o_ref[...] = a_ref[0] + jnp.where(two, b_ref[0], jnp.zeros_like(b_ref[0]))

    grid_spec = pltpu.PrefetchScalarGridSpec(
        num_scalar_prefetch=1, grid=(r // tr, nbo),
        in_specs=[pl.BlockSpec((1, tr, LANE), lambda i, b, t: (t[0, b], i, t[1, b])),
                  pl.BlockSpec((1, tr, LANE), lambda i, b, t: (t[2, b], i, t[3, b]))],
        out_specs=pl.BlockSpec((tr, LANE), lambda i, b, t: (i, b)))
    return pl.pallas_call(body, name=name, grid_spec=grid_spec, out_shape=jax.ShapeDtypeStruct((r, nbo * LANE), win.dtype),
                          compiler_params=_cparams("parallel", "parallel"))(tab, win, win)


def _pack(parts, dtype, lead=()):
    nl = len(lead)
    flat = jnp.concatenate([p.astype(dtype).reshape(lead + (-1,)) for p in parts], axis=nl)
    tot = flat.shape[nl]
    rows = -(-tot // (16 * LANE)) * 16
    flat = jnp.pad(flat, [(0, 0)] * nl + [(0, rows * LANE - tot)])
    return flat.reshape(lead + (rows, LANE))


def _unpack(buf, shapes, lead=()):
    nl = len(lead)
    flat = buf.reshape(lead + (-1,))
    out, off = [], 0
    for shp in shapes:
        size = 1
        for s in shp:
            size *= s
        out.append(lax.slice_in_dim(flat, off, off + size, axis=nl).reshape(lead + tuple(shp)))
        off += size
    return out


def _whole(g8, axis):
    t = jnp.moveaxis(g8, 0, axis)
    shp = t.shape
    return t.reshape(shp[:axis] + (shp[axis] * shp[axis + 1],) + shp[axis + 2:])


def _slots(full, axis):
    shp = full.shape
    t = full.reshape(shp[:axis] + (N_DEV, shp[axis] // N_DEV) + shp[axis + 1:])
    return jnp.moveaxis(t, axis, 0)


_PARAMS = (("meta_tokens", 1), ("norm_mix_pre", None), ("norm_mix_post", None), ("norm_ffn_pre", None), ("norm_ffn_post", None),
           ("attn_w_in", 2), ("attn_b_forget", None), ("attn_q_norm", None), ("attn_k_norm", None), ("attn_w_out", 1), ("dn_w_in", 2),
           ("dn_conv", 2), ("dn_a_log", None), ("dn_dt_bias", None), ("dn_o_norm", None), ("dn_w_out", 1), ("ffn_w_up", 2),
           ("ffn_conv", 2), ("ffn_w_down", 1))
_LOCAL_KEY = dict(meta_tokens="meta", norm_mix_pre="g_pre", norm_mix_post="g_post", norm_ffn_pre="g_fpre", norm_ffn_post="g_fpost",
                  attn_w_in="attn_w_in", attn_b_forget="attn_b", attn_q_norm="attn_qg", attn_k_norm="attn_kg", attn_w_out="attn_w_out",
                  dn_w_in="dn_w_in", dn_conv="dn_conv", dn_a_log="dn_alog", dn_dt_bias="dn_dtb", dn_o_norm="dn_og", dn_w_out="dn_w_out",
                  ffn_w_up="ffn_w_up", ffn_conv="ffn_conv", ffn_w_down="ffn_w_down")
_COL_CUT = ("attn_w_in", "dn_w_in", "ffn_w_up")
_ROW_CUT = ("attn_w_out", "dn_w_out", "ffn_w_down")


def kernel(x, meta_tokens, norm_mix_pre, norm_mix_post, norm_ffn_pre, norm_ffn_post, attn_w_in, attn_b_forget, attn_q_norm, attn_k_norm, attn_w_out, dn_w_in, dn_conv, dn_a_log, dn_dt_bias, dn_o_norm, dn_w_out, ffn_w_up, ffn_conv, ffn_w_down, loss_target, m_meta_tokens, m_norm_mix_pre, m_norm_mix_post, m_norm_ffn_pre, m_norm_ffn_post, m_attn_w_in, m_attn_b_forget, m_attn_q_norm, m_attn_k_norm, m_attn_w_out, m_dn_w_in, m_dn_conv, m_dn_a_log, m_dn_dt_bias, m_dn_o_norm, m_dn_w_out, m_ffn_w_up, m_ffn_conv, m_ffn_w_down, v_meta_tokens, v_norm_mix_pre, v_norm_mix_post, v_norm_ffn_pre, v_norm_ffn_post, v_attn_w_in, v_attn_b_forget, v_attn_q_norm, v_attn_k_norm, v_attn_w_out, v_dn_w_in, v_dn_conv, v_dn_a_log, v_dn_dt_bias, v_dn_o_norm, v_dn_w_out, v_ffn_w_up, v_ffn_conv, v_ffn_w_down):
    given = dict(locals())
    names = [p[0] for p in _PARAMS]
    cut = [p for p in _PARAMS if p[1] is not None and p[0] not in _BIG]
    rep = [p for p in _PARAMS if p[1] is None]
    me = 4 * lax.axis_index("x") + 2 * lax.axis_index("y") + lax.axis_index("c")

    bufs, plans = [], []
    for n in _COL_CUT:
        layers, d, shard = given[n].shape
        win = jnp.zeros((layers * d, _window_blocks(shard) * LANE), BF16)
        win = lax.dynamic_update_slice(win, given[n].astype(BF16).reshape(layers * d, shard), (0, (shard * me) % LANE))
        bufs.append(win)
        plans.append(_plan_gather(win))
    for n in _ROW_CUT:
        bufs.append(given[n].astype(BF16))
        plans.append(_plan_gather_rows(bufs[-1]))
    bufs.append(_pack([given[n] for n, _ in cut], F32))
    plans.append(_plan_gather(bufs[-1]))
    got = exchange(bufs, plans, name="gather_weights", relay=True)
    w = {}
    for n, g8 in zip(_COL_CUT, got[:3]):
        layers, d, shard = given[n].shape
        w[n] = assemble_cols(g8, shard, name="assemble_" + n).reshape(layers, d, -1)
    for n, full in zip(_ROW_CUT, got[3:6]):
        w[n] = full
    for (n, axis), g8 in zip(cut, _unpack(got[6], [given[n].shape for n, _ in cut], lead=(N_DEV,))):
        w[_LOCAL_KEY[n]] = _whole(g8, axis)
    for n, _ in rep:
        w[_LOCAL_KEY[n]] = given[n]

    loss, grad_x, g = local_step(x[0], loss_target[0], w)

    bufs, plans, what = [], [], []
    for n in _COL_CUT:
        for layer, gl in enumerate(g[n]):
            bufs.append(gl)
            plans.append(_plan_scatter_cols(gl, given[n].shape[2]))
            what.append((n, layer))
    for n in _ROW_CUT:
        for layer, gl in enumerate(g[n]):
            bufs.append(gl)
            plans.append(_plan_scatter_rows(gl))
            what.append((n, layer))
    nbig = len(bufs)
    bufs.append(_pack([_slots(g[_LOCAL_KEY[n]], axis) for n, axis in cut], F32, lead=(N_DEV,)))
    plans.append(_plan_scatter(bufs[-1]))
    bufs.append(_pack([g[_LOCAL_KEY[n]] for n, _ in rep] + [loss.reshape(1)], F32))
    plans.append(_plan_gather(bufs[-1]))
    got = exchange(bufs, plans, name="reduce_grads")
    per_layer = {n: [] for n in _BIG}
    for (n, layer), r8 in zip(what, got[:nbig]):
        tot = slot_sum(r8, name="sum_" + n)
        if n in _COL_CUT:
            shard = given[n].shape[2]
            tot = lax.dynamic_slice_in_dim(tot, (shard * me) % LANE, shard, axis=1)
        per_layer[n].append(tot)
    grads = {n: jnp.stack(v) for n, v in per_layer.items()}
    for (n, _), gv in zip(cut, _unpack(slot_sum(got[nbig], name="sum_cut"), [given[n].shape for n, _ in cut])):
        grads[n] = gv
    rep_sum = _unpack(slot_sum(got[nbig + 1], name="sum_rep"), [given[n].shape for n, _ in rep] + [(1,)])
    for (n, _), gv in zip(rep, rep_sum):
        grads[n] = gv
    loss_all = rep_sum[-1].reshape(())

    deltas, new_m, new_v = {}, {}, {}
    for n in names:
        shp = given[n].shape
        two_d = (-1, shp[-1])
        d, mn, vn = adamw(given[n].reshape(two_d), grads[n].reshape(two_d), given["m_" + n].reshape(two_d), given["v_" + n].reshape(two_d),
                          name="adamw_" + n)
        deltas[n], new_m[n], new_v[n] = d.reshape(shp), mn.reshape(shp), vn.reshape(shp)
    return (loss_all, grad_x[None], *[grads[n] for n in names], *[deltas[n] for n in names], *[new_m[n] for n in names],
            *[new_v[n] for n in names])
```

```python
import functools

import jax
import jax.numpy as jnp
from jax import lax
from jax.experimental import pallas as pl
from jax.experimental.pallas import tpu as pltpu

F32 = jnp.float32
BF16 = jnp.bfloat16
LANE = 128
SUB = 8
N_DEV = 8
N_META = 16
ATT_HEADS, ATT_HEAD_DIM = 16, 64
DN_HEADS, DN_HEAD_DIM, DN_CHUNK, DN_CONV = 8, 128, 64, 4
FFN_CONV = 3
EPS = 1e-6
NEG = -1e30
ADAM_LR, ADAM_B1, ADAM_B2, ADAM_EPS, ADAM_WD, ADAM_STEP = 0.001, 0.9, 0.999, 1e-08, 0.01, 10
HI = lax.Precision.HIGHEST
VMEM_LIMIT = 56 * 1024 * 1024


def _tile(n, target, align=LANE):
    if n <= target:
        return n
    best = None
    for t in range(align, target + 1, align):
        if n % t == 0:
            best = t
    assert best is not None, (n, target, align)
    return best


def _iota(shape, dim):
    return lax.broadcasted_iota(jnp.int32, shape, dim)


def _colsum8(x):
    r, c = x.shape
    return x.reshape(r // SUB, SUB, c).sum(axis=0)


def _cparams(*sem):
    return pltpu.CompilerParams(dimension_semantics=sem, vmem_limit_bytes=VMEM_LIMIT)


def _sigmoid(x):
    return 1.0 / (1.0 + jnp.exp(-x))


def _softplus(x):
    return jnp.maximum(x, 0.0) + jnp.log(1.0 + jnp.exp(-jnp.abs(x)))


def _accum(ref, part, first):
    @pl.when(first)
    def _():
        ref[...] = part

    @pl.when(jnp.logical_not(first))
    def _():
        ref[...] += part


def _lockstep(gens):
    gens = list(gens)
    out = [None] * len(gens)
    live = list(range(len(gens)))
    while live:
        nxt = []
        for i in live:
            try:
                next(gens[i])
                nxt.append(i)
            except StopIteration as stop:
                out[i] = stop.value
        live = nxt
    return out


def matmul(a, b, *, trans_a=False, trans_b=False, out_dtype=F32, tm=1664, tn=1408, tk=1664, name="matmul"):
    if trans_a:
        kdim, m = a.shape
    else:
        m, kdim = a.shape
    if trans_b:
        n, kb = b.shape
    else:
        kb, n = b.shape
    assert kb == kdim, (a.shape, b.shape, trans_a, trans_b)
    tm, tn, tk = _tile(m, tm), _tile(n, tn), _tile(kdim, tk)
    nk = kdim // tk
    dims = (((0 if trans_a else 1,), (1 if trans_b else 0,)), ((), ()))
    cdt = BF16

    def body(a_ref, b_ref, o_ref, *acc):
        part = lax.dot_general(a_ref[...].astype(cdt), b_ref[...].astype(cdt), dims, preferred_element_type=F32)
        if nk == 1:
            o_ref[...] = part.astype(o_ref.dtype)
            return
        k = pl.program_id(2)
        _accum(acc[0], part, k == 0)

        @pl.when(k == nk - 1)
        def _():
            o_ref[...] = acc[0][...].astype(o_ref.dtype)

    a_spec = pl.BlockSpec((tk, tm), lambda i, j, k: (k, i)) if trans_a else pl.BlockSpec((tm, tk), lambda i, j, k: (i, k))
    b_spec = pl.BlockSpec((tn, tk), lambda i, j, k: (j, k)) if trans_b else pl.BlockSpec((tk, tn), lambda i, j, k: (k, j))
    return pl.pallas_call(
        body,
        name=name,
        grid=(m // tm, n // tn, nk),
        in_specs=[a_spec, b_spec],
        out_specs=pl.BlockSpec((tm, tn), lambda i, j, k: (i, j)),
        out_shape=jax.ShapeDtypeStruct((m, n), out_dtype),
        scratch_shapes=[] if nk == 1 else [pltpu.VMEM((tm, tn), F32)],
        compiler_params=_cparams("parallel", "parallel", "arbitrary"),
    )(a, b)


def rms_fwd(x, g, *, res=None, out_dtype, name):
    n, d = x.shape
    tr = _tile(n, 640, SUB)

    def body(*refs):
        x_ref, g_ref = refs[0], refs[1]
        o_ref = refs[-1]
        xv = x_ref[...]
        y = xv * lax.rsqrt(jnp.mean(xv * xv, axis=-1, keepdims=True) + EPS) * g_ref[...]
        if res is not None:
            y = y + refs[2][...]
        o_ref[...] = y.astype(o_ref.dtype)

    row = pl.BlockSpec((tr, d), lambda i: (i, 0))
    ins = [x, g] + ([res] if res is not None else [])
    return pl.pallas_call(
        body, name=name, grid=(n // tr,),
        in_specs=[row, pl.BlockSpec((1, d), lambda i: (0, 0))] + ([row] if res is not None else []),
        out_specs=row, out_shape=jax.ShapeDtypeStruct((n, d), out_dtype),
        compiler_params=_cparams("parallel"),
    )(*ins)


def rms_bwd(x, g, dy, *, res=None, pad, name):
    n, d = x.shape
    tr = _tile(n, 640, SUB)

    def body(*refs):
        x_ref, g_ref, dy_ref = refs[:3]
        dx_ref, dg_ref = refs[-2:]
        i = pl.program_id(0)
        xv = x_ref[...]
        r = lax.rsqrt(jnp.mean(xv * xv, axis=-1, keepdims=True) + EPS)
        xh = xv * r
        dyv = dy_ref[...].astype(F32)
        gdy = dyv * g_ref[...]
        dx = r * (gdy - xh * jnp.mean(xh * gdy, axis=-1, keepdims=True))
        if res is not None:
            dx = dx + refs[3][...]
        rows = i * tr + _iota((tr, 1), 0)
        dx_ref[...] = jnp.where(rows >= pad, dx, 0.0)
        _accum(dg_ref, _colsum8(dyv * xh), i == 0)

    row = pl.BlockSpec((tr, d), lambda i: (i, 0))
    ins = [x, g, dy] + ([res] if res is not None else [])
    return pl.pallas_call(
        body, name=name, grid=(n // tr,),
        in_specs=[row, pl.BlockSpec((1, d), lambda i: (0, 0)), row] + ([row] if res is not None else []),
        out_specs=[row, pl.BlockSpec((SUB, d), lambda i: (0, 0))],
        out_shape=[jax.ShapeDtypeStruct((n, d), F32), jax.ShapeDtypeStruct((SUB, d), F32)],
        compiler_params=_cparams("arbitrary"),
    )(*ins)


def _halo_rows(dtype):
    return SUB * 4 // jnp.dtype(dtype).itemsize


def _shift_down(cur, prev, s):
    if s == 0:
        return cur
    hb = prev.shape[0]
    out = pltpu.roll(cur, s, 0)
    row = _iota(cur.shape, 0)
    for r in range(s):
        out = jnp.where(row == r, prev[hb - s + r:hb - s + r + 1, :], out)
    return out


def _shift_up(cur, next8, s):
    if s == 0:
        return cur
    tr = cur.shape[0]
    out = pltpu.roll(cur, tr - s, 0)
    row = _iota(cur.shape, 0)
    for r in range(s):
        out = jnp.where(row == tr - s + r, next8[r:r + 1, :], out)
    return out


def _conv_rows(cur, prev8, w):
    kw = w.shape[0]
    acc = w[kw - 1:kw, :] * cur
    for k in range(kw - 1):
        acc = acc + w[k:k + 1, :] * _shift_down(cur, prev8, kw - 1 - k)
    return acc


def conv_transpose(dy, w, *, dy_hi=None, out_dtype, name):
    n, c_in = dy.shape
    c = c_in if dy_hi is None else 2 * c_in
    kw = w.shape[0]
    tr = _tile(n, 640, SUB)
    tc = _tile(c_in, 512)
    nlo = c_in // tc
    hb = _halo_rows(dy.dtype)
    nbh = n // hb

    def body(*refs):
        w_ref, o_ref = refs[-2:]
        i, j = pl.program_id(0), pl.program_id(1)
        if dy_hi is None:
            cur, nxt = refs[0][...], refs[1][...]
        else:
            cur = jnp.where(j < nlo, refs[0][...], refs[2][...])
            nxt = jnp.where(j < nlo, refs[1][...], refs[3][...])
        cur = cur.astype(F32)
        nxt = jnp.where(i == pl.num_programs(0) - 1, 0.0, nxt.astype(F32))
        wv = w_ref[...]
        acc = wv[kw - 1:kw, :] * cur
        for k in range(kw - 1):
            acc = acc + wv[k:k + 1, :] * _shift_up(cur, nxt, kw - 1 - k)
        o_ref[...] = acc.astype(o_ref.dtype)

    def pair(col):
        return [pl.BlockSpec((tr, tc), lambda i, j: (i, col(j))),
                pl.BlockSpec((hb, tc), lambda i, j: (jnp.minimum((i + 1) * (tr // hb), nbh - 1), col(j)))]

    if dy_hi is None:
        srcs, specs = [dy, dy], pair(lambda j: j)
    else:
        srcs = [dy, dy, dy_hi, dy_hi]
        specs = pair(lambda j: jnp.minimum(j, nlo - 1)) + pair(lambda j: jnp.maximum(j - nlo, 0))
    return pl.pallas_call(
        body, name=name, grid=(n // tr, c // tc),
        in_specs=specs + [pl.BlockSpec((kw, tc), lambda i, j: (0, j))],
        out_specs=pl.BlockSpec((tr, tc), lambda i, j: (i, j)),
        out_shape=jax.ShapeDtypeStruct((n, c), out_dtype),
        compiler_params=_cparams("parallel", "parallel"),
    )(*srcs, w)


_GELU_C = 0.7978845608028654
_GELU_A = 0.044715


def _gelu(x):
    return 0.5 * x * (1.0 + jnp.tanh(_GELU_C * (x + _GELU_A * x * x * x)))


def _gelu_grad(x):
    th = jnp.tanh(_GELU_C * (x + _GELU_A * x * x * x))
    return 0.5 * (1.0 + th) + 0.5 * x * (1.0 - th * th) * _GELU_C * (1.0 + 3.0 * _GELU_A * x * x)


def conv_glu_fwd(u, cw, *, name):
    n, f2 = u.shape
    f = f2 // 2
    tr = _tile(n, 640, SUB)
    tc = _tile(f, 512)
    nf = f // tc
    hb = _halo_rows(u.dtype)
    r8 = tr // hb

    def body(ug_ref, uu_ref, pg_ref, pu_ref, wg_ref, wu_ref, o_ref):
        first = pl.program_id(0) == 0
        pg = jnp.where(first, 0.0, pg_ref[...].astype(F32))
        pu = jnp.where(first, 0.0, pu_ref[...].astype(F32))
        gate = _conv_rows(ug_ref[...].astype(F32), pg, wg_ref[...])
        up = _conv_rows(uu_ref[...].astype(F32), pu, wu_ref[...])
        o_ref[...] = (_gelu(gate) * up).astype(o_ref.dtype)

    prev = lambda off: pl.BlockSpec((hb, tc), lambda i, j: (jnp.maximum(i * r8 - 1, 0), j + off))
    return pl.pallas_call(
        body, name=name, grid=(n // tr, nf),
        in_specs=[pl.BlockSpec((tr, tc), lambda i, j: (i, j)), pl.BlockSpec((tr, tc), lambda i, j: (i, j + nf)),
                  prev(0), prev(nf),
                  pl.BlockSpec((FFN_CONV, tc), lambda i, j: (0, j)), pl.BlockSpec((FFN_CONV, tc), lambda i, j: (0, j + nf))],
        out_specs=pl.BlockSpec((tr, tc), lambda i, j: (i, j)),
        out_shape=jax.ShapeDtypeStruct((n, f), BF16),
        compiler_params=_cparams("parallel", "parallel"),
    )(u, u, u, u, cw, cw)


def _gelu_both(x):
    th = jnp.tanh(_GELU_C * (x + _GELU_A * x * x * x))
    return 0.5 * x * (1.0 + th), 0.5 * (1.0 + th) + 0.5 * x * (1.0 - th * th) * _GELU_C * (1.0 + 3.0 * _GELU_A * x * x)


def conv_glu_bwd(u, cw, dact, *, name):
    n, f2 = u.shape
    f = f2 // 2
    tr = _tile(n, 640, SUB)
    tc = _tile(f, 512)
    nf = f // tc
    hb = _halo_rows(u.dtype)
    r8 = tr // hb

    def body(ug_ref, uu_ref, pg_ref, pu_ref, wg_ref, wu_ref, da_ref, og_ref, ou_ref, dwg_ref, dwu_ref):
        first = pl.program_id(1) == 0
        pg = jnp.where(first, 0.0, pg_ref[...].astype(F32))
        pu = jnp.where(first, 0.0, pu_ref[...].astype(F32))
        curg, curu = ug_ref[...].astype(F32), uu_ref[...].astype(F32)
        gate = _conv_rows(curg, pg, wg_ref[...])
        up = _conv_rows(curu, pu, wu_ref[...])
        da = da_ref[...].astype(F32)
        gel, gel_grad = _gelu_both(gate)
        d_gate, d_up = da * up * gel_grad, da * gel
        og_ref[...] = d_gate.astype(og_ref.dtype)
        ou_ref[...] = d_up.astype(ou_ref.dtype)
        for k in range(FFN_CONV):
            part_g = _colsum8(d_gate * _shift_down(curg, pg, FFN_CONV - 1 - k))
            part_u = _colsum8(d_up * _shift_down(curu, pu, FFN_CONV - 1 - k))

            @pl.when(first)
            def _():
                dwg_ref[k] = part_g
                dwu_ref[k] = part_u

            @pl.when(jnp.logical_not(first))
            def _():
                dwg_ref[k] += part_g
                dwu_ref[k] += part_u

    tile = lambda off: pl.BlockSpec((tr, tc), lambda j, i: (i, j + off))
    prev = lambda off: pl.BlockSpec((hb, tc), lambda j, i: (jnp.maximum(i * r8 - 1, 0), j + off))
    wsp = lambda off: pl.BlockSpec((FFN_CONV, tc), lambda j, i: (0, j + off))
    acc = pl.BlockSpec((FFN_CONV, SUB, tc), lambda j, i: (0, 0, j))
    return pl.pallas_call(
        body, name=name, grid=(nf, n // tr),
        in_specs=[tile(0), tile(nf), prev(0), prev(nf), wsp(0), wsp(nf), tile(0)],
        out_specs=[tile(0), tile(0), acc, acc],
        out_shape=[jax.ShapeDtypeStruct((n, f), BF16)] * 2 + [jax.ShapeDtypeStruct((FFN_CONV, SUB, f), F32)] * 2,
        compiler_params=_cparams("parallel", "arbitrary"),
    )(u, u, u, u, cw, cw, dact)


def cumsum_rows(x, *, reverse, name):
    n, c = x.shape
    tr = LANE
    nb = n // tr

    def body(x_ref, o_ref, carry_ref):
        i = pl.program_id(0)

        @pl.when(i == 0)
        def _():
            carry_ref[...] = jnp.zeros_like(carry_ref)

        r, cc = _iota((tr, tr), 0), _iota((tr, tr), 1)
        tri = jnp.where((cc >= r) if reverse else (cc <= r), 1.0, 0.0).astype(F32)
        out = jnp.dot(tri, x_ref[...], precision=HI, preferred_element_type=F32) + carry_ref[...]
        o_ref[...] = out
        carry_ref[...] = out[0:1, :] if reverse else out[tr - 1:tr, :]

    idx = (lambda i: (nb - 1 - i, 0)) if reverse else (lambda i: (i, 0))
    return pl.pallas_call(
        body, name=name, grid=(nb,),
        in_specs=[pl.BlockSpec((tr, c), idx)], out_specs=pl.BlockSpec((tr, c), idx),
        out_shape=jax.ShapeDtypeStruct((n, c), F32), scratch_shapes=[pltpu.VMEM((1, c), F32)],
        compiler_params=_cparams("arbitrary"),
    )(x)


def _group_sum64(x):
    r, c = x.shape
    a, b = _iota((LANE, LANE), 0), _iota((LANE, LANE), 1)
    bd = jnp.where((a // 64) == (b // 64), 1.0, 0.0).astype(F32)
    parts = [jnp.dot(x[:, k * LANE:(k + 1) * LANE], bd, precision=HI, preferred_element_type=F32) for k in range(c // LANE)]
    return parts[0] if len(parts) == 1 else jnp.concatenate(parts, axis=1)


def attn_prep(proj, qg, kg, bf, *, hd, name):
    n = proj.shape[0]
    tr = _tile(n, 640, SUB)
    scale = ATT_HEAD_DIM ** -0.5
    nh = hd // LANE

    def body(q_ref, k_ref, v_ref, f_ref, qg_ref, kg_ref, bf_ref, qo_ref, ko_ref, vo_ref, lf_ref):
        def norm(x, g):
            ms = _group_sum64(x * x) * (1.0 / ATT_HEAD_DIM)
            return x * lax.rsqrt(ms + EPS) * g

        qo_ref[...] = (norm(q_ref[...], qg_ref[...]) * scale).astype(qo_ref.dtype)
        ko_ref[...] = norm(k_ref[...], kg_ref[...]).astype(ko_ref.dtype)
        vo_ref[...] = v_ref[...].astype(vo_ref.dtype)
        lf_ref[...] = -_softplus(-(f_ref[...] + bf_ref[...]))

    col = lambda c: pl.BlockSpec((tr, hd), lambda i: (i, c))
    vec = lambda w: pl.BlockSpec((1, w), lambda i: (0, 0))
    return pl.pallas_call(
        body, name=name, grid=(n // tr,),
        in_specs=[col(0), col(1), col(2), pl.BlockSpec((tr, LANE), lambda i: (i, 4 * nh)), vec(hd), vec(hd), vec(LANE)],
        out_specs=[col(0), col(0), col(0), pl.BlockSpec((tr, LANE), lambda i: (i, 0))],
        out_shape=[jax.ShapeDtypeStruct((n, hd), BF16)] * 3 + [jax.ShapeDtypeStruct((n, LANE), F32)],
        compiler_params=_cparams("parallel"),
    )(proj, proj, proj, proj, qg, kg, bf)


def _half_mask(shape):
    return _iota(shape, 1) < ATT_HEAD_DIM


def flash_fwd(qs, kn, v, proj, ct, *, pad, t, name):
    n, hd = qs.shape
    npair = hd // LANE
    nb = n // t
    gate0 = 3 * npair

    def body(q_ref, k_ref, v_ref, g_ref, c_ref, o_ref, og_ref, lse_ref, m_ref, acc_ref):
        i, j = pl.program_id(1), pl.program_id(2)

        @pl.when(j == 0)
        def _():
            m_ref[...] = jnp.full_like(m_ref, NEG)
            acc_ref[...] = jnp.zeros_like(acc_ref)

        def step(masked):
            q, k, vv = q_ref[...], k_ref[...], v_ref[...]
            half0 = _half_mask(q.shape)
            if masked:
                rowpos = i * t + _iota((t, t), 0)
                colpos = j * t + _iota((t, t), 1)
                mask = (colpos <= rowpos) & (colpos >= pad)

            def head(hh):
                sel = half0 if hh == 0 else jnp.logical_not(half0)
                qm = jnp.where(sel, q, jnp.zeros_like(q))
                v1 = jnp.where(sel, vv, jnp.ones_like(vv))
                s = lax.dot_general(qm, k, (((1,), (1,)), ((), ())), preferred_element_type=F32) - c_ref[0, hh:hh + 1, :]
                yield
                if masked:
                    s = jnp.where(mask, s, NEG)
                m_prev = m_ref[hh]
                m_new = jnp.maximum(m_prev, jnp.max(s, axis=1, keepdims=True))
                p = jnp.exp(s - m_new[:, 0:1])
                if masked:
                    p = jnp.where(mask, p, 0.0)
                yield
                acc_ref[hh] = jnp.exp(m_prev - m_new) * acc_ref[hh] + jnp.dot(p.astype(vv.dtype), v1, preferred_element_type=F32)
                m_ref[hh] = m_new

            _lockstep(head(hh) for hh in range(2))

        edge = (j == i) | (j == 0)

        @pl.when(edge & (j <= i))
        def _():
            step(True)

        @pl.when(jnp.logical_not(edge) & (j < i))
        def _():
            step(False)

        @pl.when(j == i)
        def _():
            half0 = _half_mask((t, LANE))
            a0, a1 = acc_ref[0], acc_ref[1]
            l = jnp.where(half0, a0[:, ATT_HEAD_DIM:ATT_HEAD_DIM + 1], a1[:, 0:1])
            acc = jnp.where(half0, a0, a1)
            m = jnp.where(half0, m_ref[0], m_ref[1])
            live = l > 0.0
            o = jnp.where(live, acc / jnp.where(live, l, 1.0), 0.0)
            o_ref[...] = o
            og_ref[...] = (o * _sigmoid(g_ref[...])).astype(og_ref.dtype)
            lse_ref[...] = jnp.where(live, m + jnp.log(jnp.where(live, l, 1.0)), 0.0)

    qspec = pl.BlockSpec((t, LANE), lambda p, i, j: (i, p))
    kspec = pl.BlockSpec((t, LANE), lambda p, i, j: (jnp.minimum(j, i), p))
    return pl.pallas_call(
        body, name=name, grid=(npair, nb, nb),
        in_specs=[qspec, kspec, kspec, pl.BlockSpec((t, LANE), lambda p, i, j: (i, gate0 + p)),
                  pl.BlockSpec((1, 2, t), lambda p, i, j: (p, 0, jnp.minimum(j, i)))],
        out_specs=[qspec, qspec, qspec],
        out_shape=[jax.ShapeDtypeStruct((n, hd), F32), jax.ShapeDtypeStruct((n, hd), BF16), jax.ShapeDtypeStruct((n, hd), F32)],
        scratch_shapes=[pltpu.VMEM((2, t, LANE), F32)] * 2,
        compiler_params=_cparams("parallel", "parallel", "arbitrary"),
    )(qs, kn, v, proj, ct)


def attn_bwd_prep(dgated, o, proj, *, hd, name):
    n = o.shape[0]
    tr = _tile(n, 640, SUB)
    gate0 = 3

    def body(dg_ref, o_ref, g_ref, do_ref, dl_ref, dgate_ref):
        dg, ov = dg_ref[...], o_ref[...]
        sg = _sigmoid(g_ref[...])
        do = dg * sg
        do_ref[...] = do.astype(do_ref.dtype)
        dl_ref[...] = _group_sum64(do * ov)
        dgate_ref[...] = dg * ov * sg * (1.0 - sg)

    row = pl.BlockSpec((tr, hd), lambda i: (i, 0))
    return pl.pallas_call(
        body, name=name, grid=(n // tr,),
        in_specs=[row, row, pl.BlockSpec((tr, hd), lambda i: (i, gate0))],
        out_specs=[row, row, row],
        out_shape=[jax.ShapeDtypeStruct((n, hd), BF16), jax.ShapeDtypeStruct((n, hd), F32), jax.ShapeDtypeStruct((n, hd), F32)],
        compiler_params=_cparams("parallel"),
    )(dgated, o, proj)


def flash_bwd(qs, kn, v, do, lse, delta, ct, *, pad, t, name):
    n, hd = qs.shape
    npair = hd // LANE
    nb = n // t

    def body(q_ref, k_ref, v_ref, do_ref, lse_ref, dl_ref, c_ref, dq_ref, dk_ref, dv_ref, dck_ref, dcr_ref, dk_acc, dv_acc, dck_acc):
        j, i = pl.program_id(1), pl.program_id(2)

        @pl.when((j == 0) & (i == 0))
        def _():
            dq_ref[...] = jnp.zeros_like(dq_ref)
            dcr_ref[...] = jnp.zeros_like(dcr_ref)

        @pl.when(i == j)
        def _():
            dk_acc[...] = jnp.zeros_like(dk_acc)
            dv_acc[...] = jnp.zeros_like(dv_acc)
            dck_acc[...] = jnp.zeros_like(dck_acc)

        def step(masked):
            q, k, vv, dov = q_ref[...], k_ref[...], v_ref[...], do_ref[...]
            half0 = _half_mask(q.shape)
            if masked:
                rowpos = i * t + _iota((t, t), 0)
                colpos = j * t + _iota((t, t), 1)
                mask = (colpos <= rowpos) & (colpos >= pad)
            nt = (((1,), (1,)), ((), ()))
            tn = (((0,), (0,)), ((), ()))
            dq_h, dk_h, dv_h = [], [], []
            for hh in range(2):
                sel = half0 if hh == 0 else jnp.logical_not(half0)
                qm = jnp.where(sel, q, jnp.zeros_like(q))
                dom = jnp.where(sel, dov, jnp.zeros_like(dov))
                q1 = jnp.where(sel, q, jnp.ones_like(q))
                k1 = jnp.where(sel, k, jnp.ones_like(k))
                x = lax.dot_general(qm, k, nt, preferred_element_type=F32) - c_ref[0, hh:hh + 1, :] - lse_ref[:, hh * 64:hh * 64 + 1]
                if masked:
                    p = jnp.where(mask, jnp.exp(jnp.where(mask, x, NEG)), 0.0)
                else:
                    p = jnp.exp(x)
                dp = lax.dot_general(dom, vv, nt, preferred_element_type=F32)
                ds = p * (dp - dl_ref[:, hh * 64:hh * 64 + 1])
                dsb, pb = ds.astype(k.dtype), p.astype(k.dtype)
                dq_h.append(jnp.dot(dsb, k1, preferred_element_type=F32))
                dk_h.append(lax.dot_general(dsb, q1, tn, preferred_element_type=F32))
                dv_h.append(lax.dot_general(pb, dov, tn, preferred_element_type=F32))
            rows = pl.ds(pl.multiple_of(i * t, t), t)
            dq_ref[rows, :] += jnp.where(half0, dq_h[0], dq_h[1])
            dcr_ref[rows, :] += jnp.where(half0, dq_h[1], dq_h[0])
            dk_acc[...] += jnp.where(half0, dk_h[0], dk_h[1])
            dck_acc[...] += jnp.where(half0, dk_h[1], dk_h[0])
            dv_acc[...] += jnp.where(half0, dv_h[0], dv_h[1])

        edge = (i == j) | (j == 0)

        @pl.when(edge & (i >= j))
        def _():
            step(True)

        @pl.when(jnp.logical_not(edge) & (i > j))
        def _():
            step(False)

        @pl.when(i == nb - 1)
        def _():
            dk_ref[...] = dk_acc[...]
            dv_ref[...] = dv_acc[...]
            dck_ref[...] = dck_acc[...]

    qspec = pl.BlockSpec((t, LANE), lambda p, j, i: (jnp.maximum(i, j), p))
    kspec = pl.BlockSpec((t, LANE), lambda p, j, i: (j, p))
    cspec = pl.BlockSpec((1, 2, t), lambda p, j, i: (p, 0, j))
    whole = pl.BlockSpec((n, LANE), lambda p, j, i: (0, p))
    return pl.pallas_call(
        body, name=name, grid=(npair, nb, nb),
        in_specs=[qspec, kspec, kspec, qspec, qspec, qspec, cspec],
        out_specs=[whole, kspec, kspec, kspec, whole],
        out_shape=[jax.ShapeDtypeStruct((n, hd), F32)] * 5,
        scratch_shapes=[pltpu.VMEM((t, LANE), F32)] * 3,
        compiler_params=_cparams("parallel", "arbitrary", "arbitrary"),
    )(qs, kn, v, do, lse, delta, ct)


def attn_in_bwd(dqs, dkn, proj, qg, kg, bf, dlogf, *, hd, pad, name):
    n = proj.shape[0]
    tr = _tile(n, 640, SUB)
    scale = ATT_HEAD_DIM ** -0.5
    nh = hd // LANE

    def body(dq_ref, dk_ref, q_ref, k_ref, f_ref, qg_ref, kg_ref, bf_ref, dl_ref, oq_ref, ok_ref, of_ref, gq_ref, gk_ref, gb_ref):
        i = pl.program_id(0)

        def back(x, g, dy):
            r = lax.rsqrt(_group_sum64(x * x) * (1.0 / ATT_HEAD_DIM) + EPS)
            xh = x * r
            gdy = dy * g
            dx = r * (gdy - xh * _group_sum64(xh * gdy) * (1.0 / ATT_HEAD_DIM))
            return dx, _colsum8(dy * xh)

        dxq, gq = back(q_ref[...], qg_ref[...], dq_ref[...] * scale)
        dxk, gk = back(k_ref[...], kg_ref[...], dk_ref[...])
        oq_ref[...] = dxq.astype(oq_ref.dtype)
        ok_ref[...] = dxk.astype(ok_ref.dtype)
        rows = i * tr + _iota((tr, 1), 0)
        dfl = jnp.where(rows >= pad, dl_ref[...] * _sigmoid(-(f_ref[...] + bf_ref[...])), 0.0)
        of_ref[...] = dfl.astype(of_ref.dtype)
        _accum(gq_ref, gq, i == 0)
        _accum(gk_ref, gk, i == 0)
        _accum(gb_ref, _colsum8(dfl), i == 0)

    row = pl.BlockSpec((tr, hd), lambda i: (i, 0))
    col = lambda c: pl.BlockSpec((tr, hd), lambda i: (i, c))
    nar = pl.BlockSpec((tr, LANE), lambda i: (i, 0))
    vec = lambda w: pl.BlockSpec((1, w), lambda i: (0, 0))
    acc = lambda w: pl.BlockSpec((SUB, w), lambda i: (0, 0))
    return pl.pallas_call(
        body, name=name, grid=(n // tr,),
        in_specs=[row, row, col(0), col(1), pl.BlockSpec((tr, LANE), lambda i: (i, 4 * nh)), vec(hd), vec(hd), vec(LANE), nar],
        out_specs=[row, row, nar, acc(hd), acc(hd), acc(LANE)],
        out_shape=[jax.ShapeDtypeStruct((n, hd), BF16)] * 2 + [jax.ShapeDtypeStruct((n, LANE), BF16),
                   jax.ShapeDtypeStruct((SUB, hd), F32), jax.ShapeDtypeStruct((SUB, hd), F32), jax.ShapeDtypeStruct((SUB, LANE), F32)],
        compiler_params=_cparams("arbitrary"),
    )(dqs, dkn, proj, proj, proj, qg, kg, bf, dlogf)


def _silu(x):
    return x * _sigmoid(x)


def _silu_grad(x):
    s = _sigmoid(x)
    return s * (1.0 + x * (1.0 - s))


def gdn_prep(proj, cw, *, hd, name):
    n = proj.shape[0]
    tr = _tile(n, 640, SUB)
    nh = hd // LANE
    r8 = tr // SUB
    qscale = DN_HEAD_DIM ** -0.5

    def body(x_ref, p_ref, w_ref, o_ref):
        i, c = pl.program_id(0), pl.program_id(1)
        prev = jnp.where(i == 0, 0.0, p_ref[...])
        s = _silu(_conv_rows(x_ref[...], prev, w_ref[...]))
        r = lax.rsqrt(jnp.sum(s * s, axis=-1, keepdims=True) + EPS)
        mult = jnp.where(c < nh, r * qscale, jnp.where(c < 2 * nh, r, 1.0))
        o_ref[...] = s * mult

    return pl.pallas_call(
        body, name=name, grid=(n // tr, 3 * nh),
        in_specs=[pl.BlockSpec((tr, LANE), lambda i, c: (i, c)),
                  pl.BlockSpec((SUB, LANE), lambda i, c: (jnp.maximum(i * r8 - 1, 0), c)),
                  pl.BlockSpec((DN_CONV, LANE), lambda i, c: (0, c))],
        out_specs=pl.BlockSpec((tr, LANE), lambda i, c: (i, c)),
        out_shape=jax.ShapeDtypeStruct((n, 3 * hd), F32),
        compiler_params=_cparams("parallel", "parallel"),
    )(proj, proj, cw)


def _chunk_tri(reverse):
    r, c = _iota((LANE, LANE), 0), _iota((LANE, LANE), 1)
    same = (r // DN_CHUNK) == (c // DN_CHUNK)
    return jnp.where(same & ((c >= r) if reverse else (c <= r)), 1.0, 0.0).astype(F32)


def gdn_gates(proj, alog, dtb, *, hd, name):
    n = proj.shape[0]
    gcol = 4 * (hd // LANE)

    def body(x_ref, a_ref, d_ref, o_ref):
        x = x_ref[...]
        lane = _iota(x.shape, 1)
        g = -jnp.exp(a_ref[...]) * _softplus(x + d_ref[...])
        gc = jnp.dot(_chunk_tri(False), jnp.where((lane >= DN_HEADS) & (lane < 2 * DN_HEADS), g, 0.0), precision=HI,
                     preferred_element_type=F32)
        o_ref[...] = jnp.where(lane < DN_HEADS, _sigmoid(x), gc)

    vec = pl.BlockSpec((1, LANE), lambda i: (0, 0))
    return pl.pallas_call(
        body, name=name, grid=(n // LANE,),
        in_specs=[pl.BlockSpec((LANE, LANE), lambda i: (i, gcol)), vec, vec],
        out_specs=pl.BlockSpec((LANE, LANE), lambda i: (i, 0)),
        out_shape=jax.ShapeDtypeStruct((n, LANE), F32),
        compiler_params=_cparams("parallel"),
    )(proj, alog, dtb)


def _mm(a, b, ca=1, cb=0):
    return lax.dot_general(a.astype(BF16), b.astype(BF16), (((ca,), (cb,)), ((), ())), preferred_element_type=F32)


def _mmh(a, b):
    return jnp.dot(a, b, precision=HI, preferred_element_type=F32)


def _gdn_common(q, k, v, beta, gc_c, gc_r):
    r, c = _iota((LANE, LANE), 0), _iota((LANE, LANE), 1)
    same = (r // DN_CHUNK) == (c // DN_CHUNK)
    incl, strict = same & (r >= c), same & (r > c)
    d = jnp.exp(jnp.where(incl, gc_c - gc_r, NEG))
    kk = _mm(k, k, 1, 1)
    qk = _mm(q, k, 1, 1)
    yield
    ahat = jnp.where(strict, kk * d, 0.0)
    a = ahat * beta
    eye = jnp.where(r == c, 1.0, 0.0).astype(F32)
    t = eye - a
    pw = _mmh(a, a)
    yield
    for step in range(5):
        t = t + _mmh(t, pw)
        if step < 4:
            pw = _mmh(pw, pw)
        yield
    row = _iota((LANE, 1), 0)
    gl0 = jnp.sum(jnp.where(row == DN_CHUNK - 1, gc_c, 0.0), axis=0, keepdims=True)
    gl1 = jnp.sum(jnp.where(row == LANE - 1, gc_c, 0.0), axis=0, keepdims=True)
    gam = jnp.exp(gc_c)
    lam = jnp.exp(jnp.where(row < DN_CHUNK, gl0, gl1) - gc_c)
    kb, vb = k * (beta * gam), v * beta
    cm = dict(incl=incl, strict=strict, d=d, kk=kk, ahat=ahat, t=t, gam=gam, lam=lam, kb=kb, vb=vb, w=_mm(t, kb), u0=_mm(t, vb),
              qk=qk, pm=jnp.where(incl, qk * d, 0.0), qg=q * gam, kl=k * lam, g0=jnp.exp(gl0), g1=jnp.exp(gl1))
    yield
    return cm


def _gdn_states(cm, s0):
    c = DN_CHUNK
    u_a = cm["u0"][:c] - _mm(cm["w"][:c], s0, 1, 1)
    yield
    s1 = cm["g0"] * s0 + _mm(u_a, cm["kl"][:c], 0, 0)
    yield
    u_b = cm["u0"][c:] - _mm(cm["w"][c:], s1, 1, 1)
    yield
    s2 = cm["g1"] * s1 + _mm(u_b, cm["kl"][c:], 0, 0)
    yield
    return u_a, s1, u_b, s2


def gdn_chunk_fwd(qkv, bg, bgt, proj, ogain, *, hd, name):
    n = qkv.shape[0]
    nb = n // LANE
    nh = hd // LANE
    c = DN_CHUNK

    def body(q_ref, k_ref, v_ref, bg_ref, bgt_ref, g_ref, gain_ref, o_ref, og_ref, hist_ref, s_ref):
        @pl.when(pl.program_id(0) == 0)
        def _():
            s_ref[...] = jnp.zeros_like(s_ref)

        hist_ref[0] = s_ref[...]

        def head(h):
            cols = slice(h * LANE, (h + 1) * LANE)
            cm = yield from _gdn_common(q_ref[:, cols], k_ref[:, cols], v_ref[:, cols], bg_ref[:, h:h + 1],
                                        bg_ref[:, nh + h:nh + h + 1], bgt_ref[nh + h:nh + h + 1, :])
            s0 = s_ref[h]
            u_a, s1, u_b, s2 = yield from _gdn_states(cm, s0)
            u_all = jnp.concatenate([u_a, u_b], axis=0)
            o = jnp.concatenate([_mm(cm["qg"][:c], s0, 1, 1), _mm(cm["qg"][c:], s1, 1, 1)], axis=0) + _mm(cm["pm"], u_all)
            s_ref[h] = s2
            o_ref[:, cols] = o
            rn = lax.rsqrt(jnp.mean(o * o, axis=-1, keepdims=True) + EPS)
            og_ref[:, cols] = (o * rn * gain_ref[...] * _silu(g_ref[:, cols])).astype(og_ref.dtype)

        _lockstep(head(h) for h in range(nh))

    col = lambda cc: pl.BlockSpec((LANE, hd), lambda b: (b, cc))
    return pl.pallas_call(
        body, name=name, grid=(nb,),
        in_specs=[col(0), col(1), col(2), pl.BlockSpec((LANE, LANE), lambda b: (b, 0)),
                  pl.BlockSpec((2 * nh, LANE), lambda b: (0, b)), pl.BlockSpec((LANE, hd), lambda b: (b, 3)),
                  pl.BlockSpec((1, LANE), lambda b: (0, 0))],
        out_specs=[col(0), col(0), pl.BlockSpec((1, nh, LANE, LANE), lambda b: (b, 0, 0, 0))],
        out_shape=[jax.ShapeDtypeStruct((n, hd), F32), jax.ShapeDtypeStruct((n, hd), BF16),
                   jax.ShapeDtypeStruct((nb, nh, LANE, LANE), F32)],
        scratch_shapes=[pltpu.VMEM((nh, LANE, LANE), F32)],
        compiler_params=_cparams("arbitrary"),
    )(qkv, qkv, qkv, bg, bgt, proj, ogain)


def gdn_chunk_bwd(qkv, bg, bgt, proj, ogain, o_raw, dog, hist, *, hd, name):
    n = qkv.shape[0]
    nb = n // LANE
    nh = hd // LANE
    c = DN_CHUNK

    def body(q_ref, k_ref, v_ref, bg_ref, bgt_ref, g_ref, gain_ref, o_ref, dog_ref, hist_ref,
             dq_ref, dk_ref, dv_ref, dgate_ref, dbg_ref, dgt_ref, dgain_ref, ds_ref):
        first = pl.program_id(0) == 0

        @pl.when(first)
        def _():
            ds_ref[...] = jnp.zeros_like(ds_ref)

        lane = _iota((LANE, LANE), 1)
        row = _iota((LANE, 1), 0)

        def head(h):
            cols = slice(h * LANE, (h + 1) * LANE)
            q, k, v = q_ref[:, cols], k_ref[:, cols], v_ref[:, cols]
            beta = bg_ref[:, h:h + 1]
            cm = yield from _gdn_common(q, k, v, beta, bg_ref[:, nh + h:nh + h + 1], bgt_ref[nh + h:nh + h + 1, :])
            s0 = hist_ref[0, h]
            u_a, s1, u_b, _ = yield from _gdn_states(cm, s0)
            u_all = jnp.concatenate([u_a, u_b], axis=0)
            o, gate, d_out, gain = o_ref[:, cols], g_ref[:, cols], dog_ref[:, cols], gain_ref[...]
            rn = lax.rsqrt(jnp.mean(o * o, axis=-1, keepdims=True) + EPS)
            xh = o * rn
            d_on = d_out * _silu(gate)
            dgate_ref[:, cols] = d_out * xh * gain * _silu_grad(gate)
            dgain = _colsum8(d_on * xh)
            gdy = d_on * gain
            d_o = rn * (gdy - xh * jnp.mean(xh * gdy, axis=-1, keepdims=True))
            pt_do = _mm(cm["pm"], d_o, 0, 0)
            ds_in = ds_ref[h]
            yield
            du_b = _mm(cm["kl"][c:], ds_in, 1, 1) + pt_do[c:]
            dkl_b = _mm(u_b, ds_in)
            dqg_b = _mm(d_o[c:], s1)
            dg1 = jnp.sum(jnp.sum(ds_in * s1, axis=1, keepdims=True), axis=0, keepdims=True)
            dw_b = -_mm(du_b, s1)
            yield
            ds_mid = cm["g1"] * ds_in + _mm(d_o[c:], cm["qg"][c:], 0, 0) - _mm(du_b, cm["w"][c:], 0, 0)
            yield
            du_a = _mm(cm["kl"][:c], ds_mid, 1, 1) + pt_do[:c]
            dkl_a = _mm(u_a, ds_mid)
            dqg_a = _mm(d_o[:c], s0)
            dg0 = jnp.sum(jnp.sum(ds_mid * s0, axis=1, keepdims=True), axis=0, keepdims=True)
            yield
            dw_a = -_mm(du_a, s0)
            ds_ref[h] = cm["g0"] * ds_mid + _mm(d_o[:c], cm["qg"][:c], 0, 0) - _mm(du_a, cm["w"][:c], 0, 0)
            du = jnp.concatenate([du_a, du_b], axis=0)
            dkl = jnp.concatenate([dkl_a, dkl_b], axis=0)
            dqg = jnp.concatenate([dqg_a, dqg_b], axis=0)
            dw = jnp.concatenate([dw_a, dw_b], axis=0)
            t, d, gam, lam = cm["t"], cm["d"], cm["gam"], cm["lam"]
            dp = jnp.where(cm["incl"], _mm(d_o, u_all, 1, 1), 0.0)
            dt = _mm(dw, cm["kb"], 1, 1) + _mm(du, cm["vb"], 1, 1)
            dkb = _mm(t, dw, 0, 0)
            dvb = _mm(t, du, 0, 0)
            yield
            x_t = _mm(t, dt, 0, 0)
            yield
            da = jnp.where(cm["strict"], -_mm(x_t, t, 1, 1), 0.0)
            yield
            kb_k = jnp.sum(dkb * k, axis=1, keepdims=True)
            dbeta = jnp.sum(da * cm["ahat"], axis=1, keepdims=True) + gam * kb_k + jnp.sum(dvb * v, axis=1, keepdims=True)
            dahat = da * beta
            dkk = dahat * d
            dqk = dp * d
            e = (dahat * cm["kk"] + dp * cm["qk"]) * d
            dk_ref[:, cols] = (_mm(dkk, k) + _mm(dkk, k, 0, 0) + _mm(dqk, q, 0, 0) + dkb * (beta * gam) + dkl * lam)
            dq_ref[:, cols] = _mm(dqk, k) + dqg * gam
            dv_ref[:, cols] = dvb * beta
            dgam = beta * kb_k + jnp.sum(dqg * q, axis=1, keepdims=True)
            dlam_lam = jnp.sum(dkl * k, axis=1, keepdims=True) * lam
            dgl0 = jnp.sum(jnp.where(row < c, dlam_lam, 0.0), axis=0, keepdims=True) + dg0 * cm["g0"]
            dgl1 = jnp.sum(jnp.where(row >= c, dlam_lam, 0.0), axis=0, keepdims=True) + dg1 * cm["g1"]
            dgc = (jnp.sum(e, axis=1, keepdims=True) + dgam * gam - dlam_lam
                   + jnp.where(row == c - 1, dgl0, 0.0) + jnp.where(row == LANE - 1, dgl1, 0.0))
            dgt_ref[h:h + 1, :] = -jnp.sum(e, axis=0, keepdims=True)
            return jnp.where(lane == h, dbeta, 0.0) + jnp.where(lane == nh + h, dgc, 0.0), dgain

        parts = _lockstep(head(h) for h in range(nh))
        dbg_ref[...] = sum(p[0] for p in parts)
        _accum(dgain_ref, sum(p[1] for p in parts), first)

    rev = lambda b: nb - 1 - b
    col = lambda cc: pl.BlockSpec((LANE, hd), lambda b: (rev(b), cc))
    return pl.pallas_call(
        body, name=name, grid=(nb,),
        in_specs=[col(0), col(1), col(2), pl.BlockSpec((LANE, LANE), lambda b: (rev(b), 0)),
                  pl.BlockSpec((2 * nh, LANE), lambda b: (0, rev(b))), pl.BlockSpec((LANE, hd), lambda b: (rev(b), 3)),
                  pl.BlockSpec((1, LANE), lambda b: (0, 0)), col(0), col(0),
                  pl.BlockSpec((1, nh, LANE, LANE), lambda b: (rev(b), 0, 0, 0))],
        out_specs=[col(0), col(0), col(0), col(0), pl.BlockSpec((LANE, LANE), lambda b: (rev(b), 0)),
                   pl.BlockSpec((nh, LANE), lambda b: (0, rev(b))), pl.BlockSpec((SUB, LANE), lambda b: (0, 0))],
        out_shape=[jax.ShapeDtypeStruct((n, hd), F32)] * 4 + [jax.ShapeDtypeStruct((n, LANE), F32),
                   jax.ShapeDtypeStruct((nh, n), F32), jax.ShapeDtypeStruct((SUB, LANE), F32)],
        scratch_shapes=[pltpu.VMEM((nh, LANE, LANE), F32)],
        compiler_params=_cparams("arbitrary"),
    )(qkv, qkv, qkv, bg, bgt, proj, ogain, o_raw, dog, hist)


def gdn_gates_bwd(proj, alog, dtb, dbg, *, hd, pad, name):
    n = proj.shape[0]
    gcol = 4 * (hd // LANE)

    def body(x_ref, a_ref, d_ref, dbg_ref, o_ref, da_ref, dd_ref):
        i = pl.program_id(0)
        x = x_ref[...]
        lane = _iota(x.shape, 1)
        rows = i * LANE + _iota((LANE, 1), 0)
        isg = (lane >= DN_HEADS) & (lane < 2 * DN_HEADS)
        dbgv = jnp.where(rows >= pad, dbg_ref[...], 0.0)
        dg = jnp.dot(_chunk_tri(True), jnp.where(isg, dbgv, 0.0), precision=HI, preferred_element_type=F32)
        ea = jnp.exp(a_ref[...])
        z = x + d_ref[...]
        dg = jnp.where(rows >= pad, dg, 0.0)
        dz = jnp.where(isg, dg * (-ea) * _sigmoid(z), 0.0)
        sb = _sigmoid(x)
        o_ref[...] = jnp.where(lane < DN_HEADS, dbgv * sb * (1.0 - sb), dz).astype(o_ref.dtype)
        _accum(da_ref, _colsum8(jnp.where(isg, dg * (-ea) * _softplus(z), 0.0)), i == 0)
        _accum(dd_ref, _colsum8(dz), i == 0)

    vec = pl.BlockSpec((1, LANE), lambda i: (0, 0))
    blk = pl.BlockSpec((LANE, LANE), lambda i: (i, 0))
    acc = pl.BlockSpec((SUB, LANE), lambda i: (0, 0))
    return pl.pallas_call(
        body, name=name, grid=(n // LANE,),
        in_specs=[pl.BlockSpec((LANE, LANE), lambda i: (i, gcol)), vec, vec, blk],
        out_specs=[blk, acc, acc],
        out_shape=[jax.ShapeDtypeStruct((n, LANE), BF16), jax.ShapeDtypeStruct((SUB, LANE), F32), jax.ShapeDtypeStruct((SUB, LANE), F32)],
        compiler_params=_cparams("arbitrary"),
    )(proj, alog, dtb, dbg)


def gdn_prep_bwd(proj, cw, dqkv, *, hd, name):
    n = proj.shape[0]
    tr = _tile(n, 640, SUB)
    nh = hd // LANE
    r8 = tr // SUB
    qscale = DN_HEAD_DIM ** -0.5

    def body(x_ref, p_ref, w_ref, dq_ref, dk_ref, dv_ref, o_ref, dw_ref):
        c, i = pl.program_id(0), pl.program_id(1)
        first = i == 0
        prev = jnp.where(first, 0.0, p_ref[...])
        cur = x_ref[...]
        cv = _conv_rows(cur, prev, w_ref[...])
        s = _silu(cv)
        r = lax.rsqrt(jnp.sum(s * s, axis=-1, keepdims=True) + EPS)
        y = s * r
        dy = jnp.where(c < nh, dq_ref[...] * qscale, dk_ref[...])
        ds_norm = r * (dy - y * jnp.sum(dy * y, axis=-1, keepdims=True))
        dcv = jnp.where(c < 2 * nh, ds_norm, dv_ref[...]) * _silu_grad(cv)
        o_ref[...] = dcv
        for k in range(DN_CONV):
            part = _colsum8(dcv * _shift_down(cur, prev, DN_CONV - 1 - k))

            @pl.when(first)
            def _():
                dw_ref[k] = part

            @pl.when(jnp.logical_not(first))
            def _():
                dw_ref[k] += part

    blk = lambda f: pl.BlockSpec((tr, LANE), f)
    return pl.pallas_call(
        body, name=name, grid=(3 * nh, n // tr),
        in_specs=[blk(lambda c, i: (i, c)), pl.BlockSpec((SUB, LANE), lambda c, i: (jnp.maximum(i * r8 - 1, 0), c)),
                  pl.BlockSpec((DN_CONV, LANE), lambda c, i: (0, c)),
                  blk(lambda c, i: (i, jnp.minimum(c, nh - 1))), blk(lambda c, i: (i, jnp.clip(c - nh, 0, nh - 1))),
                  blk(lambda c, i: (i, jnp.clip(c - 2 * nh, 0, nh - 1)))],
        out_specs=[blk(lambda c, i: (i, c)), pl.BlockSpec((DN_CONV, SUB, LANE), lambda c, i: (0, 0, c))],
        out_shape=[jax.ShapeDtypeStruct((n, 3 * hd), F32), jax.ShapeDtypeStruct((DN_CONV, SUB, 3 * hd), F32)],
        compiler_params=_cparams("parallel", "arbitrary"),
    )(proj, proj, cw, *dqkv)


def loss_head(h, target, *, x0, name):
    n, d = h.shape
    tr = LANE
    nb0 = x0 // tr

    def body(h_ref, t_ref, dh_ref, sq_ref):
        i = pl.program_id(0)
        live = i >= nb0
        err = jnp.where(live, h_ref[...] - t_ref[...], 0.0)
        dh_ref[...] = err * (1.0 / d)
        _accum(sq_ref, _colsum8(err * err), i == 0)

    row = pl.BlockSpec((tr, d), lambda i: (i, 0))
    return pl.pallas_call(
        body, name=name, grid=(n // tr,),
        in_specs=[row, pl.BlockSpec((tr, d), lambda i: (jnp.maximum(i - nb0, 0), 0))],
        out_specs=[row, pl.BlockSpec((SUB, d), lambda i: (0, 0))],
        out_shape=[jax.ShapeDtypeStruct((n, d), F32), jax.ShapeDtypeStruct((SUB, d), F32)],
        compiler_params=_cparams("arbitrary"),
    )(h, target)


def adamw(w, g, m, v, *, name):
    r, c = w.shape
    tr = _tile(r, 512, SUB) if r % SUB == 0 else r
    c1 = 1.0 / (1.0 - ADAM_B1 ** ADAM_STEP)
    c2 = 1.0 / (1.0 - ADAM_B2 ** ADAM_STEP)

    def body(w_ref, g_ref, m_ref, v_ref, d_ref, mo_ref, vo_ref):
        gv = g_ref[...]
        mn = ADAM_B1 * m_ref[...] + (1.0 - ADAM_B1) * gv
        vn = ADAM_B2 * v_ref[...] + (1.0 - ADAM_B2) * (gv * gv)
        d_ref[...] = -ADAM_LR * ((mn * c1) / (jnp.sqrt(vn * c2) + ADAM_EPS) + ADAM_WD * w_ref[...])
        mo_ref[...] = mn
        vo_ref[...] = vn

    blk = pl.BlockSpec((tr, c), lambda i: (i, 0))
    return pl.pallas_call(
        body, name=name, grid=(r // tr,), in_specs=[blk] * 4, out_specs=[blk] * 3,
        out_shape=[jax.ShapeDtypeStruct((r, c), F32)] * 3, compiler_params=_cparams("parallel"),
    )(w, g, m, v)


_BIG = ("attn_w_in", "attn_w_out", "dn_w_in", "dn_w_out", "ffn_w_up", "ffn_w_down")


def _row(v, width=None):
    v = v.astype(F32).reshape(1, -1)
    if width is not None and v.shape[1] < width:
        v = jnp.pad(v, ((0, 0), (0, width - v.shape[1])))
    return v


def _fold8(p):
    return jnp.sum(p, axis=-2)


def local_step(x, target, w):
    seq, d = x.shape
    pad = (-(N_META + seq)) % LANE
    x0 = pad + N_META
    n = x0 + seq
    depth = w["g_pre"].shape[0]
    hd_a = ATT_HEADS * ATT_HEAD_DIM
    hd_d = DN_HEADS * DN_HEAD_DIM
    t_att = _tile(n, 640)
    h = jnp.concatenate([jnp.zeros((pad, d), F32), w["meta"].astype(F32), x], axis=0)
    saved = []
    for i in range(depth):
        j = i // 2
        s = dict(h=h)
        s["a"] = rms_fwd(h, _row(w["g_pre"][i]), out_dtype=BF16, name="rms_pre")
        if i % 2 == 0:
            s["proj"] = proj = matmul(s["a"], w["attn_w_in"][j], name="mm_attn_in")
            qg, kg = _row(jnp.tile(w["attn_qg"][j], ATT_HEADS)), _row(jnp.tile(w["attn_kg"][j], ATT_HEADS))
            bf = _row(w["attn_b"][j], LANE)
            s["qs"], s["kn"], s["v"], logf = attn_prep(proj, qg, kg, bf, hd=hd_a, name="attn_prep")
            c = cumsum_rows(logf, reverse=False, name="cumsum_fwd")
            s["ct"] = c[:, :ATT_HEADS].T.reshape(ATT_HEADS // 2, 2, n)
            s["o"], s["og"], s["lse"] = flash_fwd(s["qs"], s["kn"], s["v"], proj, s["ct"], pad=pad, t=t_att, name="flash_fwd")
            s["m"] = matmul(s["og"], w["attn_w_out"][j], name="mm_attn_out")
        else:
            s["proj"] = proj = matmul(s["a"], w["dn_w_in"][j], name="mm_dn_in")
            s["qkv"] = gdn_prep(proj, w["dn_conv"][j], hd=hd_d, name="gdn_prep")
            alog = jnp.pad(_row(w["dn_alog"][j]), ((0, 0), (DN_HEADS, LANE - 2 * DN_HEADS)))
            dtb = jnp.pad(_row(w["dn_dtb"][j]), ((0, 0), (DN_HEADS, LANE - 2 * DN_HEADS)))
            s["bg"] = gdn_gates(proj, alog, dtb, hd=hd_d, name="gdn_gates")
            s["bgt"] = s["bg"][:, :2 * DN_HEADS].T
            s["o"], s["og"], s["hist"] = gdn_chunk_fwd(s["qkv"], s["bg"], s["bgt"], proj, _row(w["dn_og"][j]), hd=hd_d, name="gdn_fwd")
            s["m"] = matmul(s["og"], w["dn_w_out"][j], name="mm_dn_out")
        s["h_mid"] = rms_fwd(s["m"], _row(w["g_post"][i]), res=h, out_dtype=F32, name="rms_post")
        s["b"] = rms_fwd(s["h_mid"], _row(w["g_fpre"][i]), out_dtype=BF16, name="rms_fpre")
        s["u"] = matmul(s["b"], w["ffn_w_up"][i], out_dtype=BF16, name="mm_ffn_up")
        s["act"] = conv_glu_fwd(s["u"], w["ffn_conv"][i], name="ffn_glu")
        s["f"] = matmul(s["act"], w["ffn_w_down"][i], name="mm_ffn_down")
        h = rms_fwd(s["f"], _row(w["g_fpost"][i]), res=s["h_mid"], out_dtype=F32, name="rms_fpost")
        saved.append(s)

    dh, sq = loss_head(h, target, x0=x0, name="loss_head")
    loss = 0.5 * jnp.sum(sq) / d

    g = {k: [None] * depth for k in ("g_pre", "g_post", "g_fpre", "g_fpost", "ffn_w_up", "ffn_conv", "ffn_w_down")}
    for k in ("attn_w_in", "attn_b", "attn_qg", "attn_kg", "attn_w_out", "dn_w_in", "dn_conv", "dn_alog", "dn_dtb", "dn_og", "dn_w_out"):
        g[k] = [None] * (depth // 2)
    for i in reversed(range(depth)):
        j = i // 2
        s = saved[i]
        proj = s["proj"]
        df, p8 = rms_bwd(s["f"], _row(w["g_fpost"][i]), dh, pad=pad, name="rms_fpost_bwd")
        g["g_fpost"][i] = _fold8(p8)
        dact = matmul(df, w["ffn_w_down"][i], trans_b=True, out_dtype=BF16, name="mm_ffn_down_dx")
        g["ffn_w_down"][i] = matmul(s["act"], df, trans_a=True, out_dtype=BF16, name="mm_ffn_down_dw")
        d_gate, d_up, pg8, pu8 = conv_glu_bwd(s["u"], w["ffn_conv"][i], dact, name="ffn_glu_bwd")
        g["ffn_conv"][i] = jnp.concatenate([_fold8(pg8), _fold8(pu8)], axis=-1)
        du = conv_transpose(d_gate, w["ffn_conv"][i], dy_hi=d_up, out_dtype=BF16, name="ffn_conv_t")
        db = matmul(du, w["ffn_w_up"][i], trans_b=True, name="mm_ffn_up_dx")
        g["ffn_w_up"][i] = matmul(s["b"], du, trans_a=True, out_dtype=BF16, name="mm_ffn_up_dw")
        dh_mid, p8 = rms_bwd(s["h_mid"], _row(w["g_fpre"][i]), db, res=dh, pad=pad, name="rms_fpre_bwd")
        g["g_fpre"][i] = _fold8(p8)
        dm, p8 = rms_bwd(s["m"], _row(w["g_post"][i]), dh_mid, pad=pad, name="rms_post_bwd")
        g["g_post"][i] = _fold8(p8)
        if i % 2 == 0:
            g["attn_w_out"][j] = matmul(s["og"], dm, trans_a=True, out_dtype=BF16, name="mm_attn_out_dw")
            dgated = matmul(dm, w["attn_w_out"][j], trans_b=True, name="mm_attn_out_dx")
            do, delta, dgate = attn_bwd_prep(dgated, s["o"], proj, hd=hd_a, name="attn_bwd_prep")
            dqs, dkn, dv, dck, dcr = flash_bwd(s["qs"], s["kn"], s["v"], do, s["lse"], delta, s["ct"], pad=pad, t=t_att, name="flash_bwd")
            dc = (dcr - dck)[:, ::ATT_HEAD_DIM].reshape(n, ATT_HEADS // 2, 2)[:, :, ::-1].reshape(n, ATT_HEADS)
            dc = jnp.pad(dc, ((0, 0), (0, LANE - ATT_HEADS)))
            dlogf = cumsum_rows(dc, reverse=True, name="cumsum_bwd")
            qg, kg = _row(jnp.tile(w["attn_qg"][j], ATT_HEADS)), _row(jnp.tile(w["attn_kg"][j], ATT_HEADS))
            bf = _row(w["attn_b"][j], LANE)
            dq_raw, dk_raw, dfl, gq8, gk8, gb8 = attn_in_bwd(dqs, dkn, proj, qg, kg, bf, dlogf, hd=hd_a, pad=pad, name="attn_in_bwd")
            g["attn_qg"][j] = _fold8(gq8).reshape(ATT_HEADS, ATT_HEAD_DIM).sum(axis=0)
            g["attn_kg"][j] = _fold8(gk8).reshape(ATT_HEADS, ATT_HEAD_DIM).sum(axis=0)
            g["attn_b"][j] = _fold8(gb8)[:ATT_HEADS]
            dproj = jnp.concatenate([dq_raw, dk_raw, dv.astype(BF16), dgate.astype(BF16), dfl], axis=1)
            w_in, key = w["attn_w_in"][j], "attn_w_in"
        else:
            g["dn_w_out"][j] = matmul(s["og"], dm, trans_a=True, out_dtype=BF16, name="mm_dn_out_dw")
            dgated = matmul(dm, w["dn_w_out"][j], trans_b=True, name="mm_dn_out_dx")
            alog = jnp.pad(_row(w["dn_alog"][j]), ((0, 0), (DN_HEADS, LANE - 2 * DN_HEADS)))
            dtb = jnp.pad(_row(w["dn_dtb"][j]), ((0, 0), (DN_HEADS, LANE - 2 * DN_HEADS)))
            dq, dk, dv, dgate, dbg, dgt, gain8 = gdn_chunk_bwd(s["qkv"], s["bg"], s["bgt"], proj, _row(w["dn_og"][j]), s["o"], dgated,
                                                              s["hist"], hd=hd_d, name="gdn_bwd")
            g["dn_og"][j] = _fold8(gain8)
            dbg = dbg + jnp.pad(dgt.T, ((0, 0), (DN_HEADS, LANE - 2 * DN_HEADS)))
            dgl, da8, dd8 = gdn_gates_bwd(proj, alog, dtb, dbg, hd=hd_d, pad=pad, name="gdn_gates_bwd")
            g["dn_alog"][j] = _fold8(da8)[DN_HEADS:2 * DN_HEADS]
            g["dn_dtb"][j] = _fold8(dd8)[DN_HEADS:2 * DN_HEADS]
            dcv, p8 = gdn_prep_bwd(proj, w["dn_conv"][j], (dq, dk, dv), hd=hd_d, name="gdn_prep_bwd")
            g["dn_conv"][j] = _fold8(p8)
            dqkv = conv_transpose(dcv, w["dn_conv"][j], out_dtype=BF16, name="gdn_conv_t")
            dproj = jnp.concatenate([dqkv, dgate.astype(BF16), dgl], axis=1)
            w_in, key = w["dn_w_in"][j], "dn_w_in"
        da = matmul(dproj, w_in, trans_b=True, name="mm_in_dx")
        g[key][j] = matmul(s["a"], dproj, trans_a=True, out_dtype=BF16, name="mm_in_dw")
        dh, p8 = rms_bwd(s["h"], _row(w["g_pre"][i]), da, res=dh_mid, pad=pad, name="rms_pre_bwd")
        g["g_pre"][i] = _fold8(p8)

    grads = {k: (v if k in _BIG else jnp.stack(v)) for k, v in g.items()}
    grads["meta"] = dh[pad:x0]
    return loss, dh[x0:], grads


_ANY = pl.BlockSpec(memory_space=pl.ANY)


def _mesh_place():
    x, y, c = lax.axis_index("x"), lax.axis_index("y"), lax.axis_index("c")
    return x, y, c, 4 * x + 2 * y + c


def _peer(x, y, c, k):
    px, py, pc = (1 - x if k & 4 else x), (1 - y if k & 2 else y), (1 - c if k & 1 else c)
    return (px, py, pc), 4 * px + 2 * py + pc


def _window_blocks(shard):
    return max(-(-(shard * (d + 1)) // LANE) - (shard * d) // LANE for d in range(N_DEV))


def _sds(shape, dtype):
    return jax.ShapeDtypeStruct(tuple(shape), dtype)


def _plan_gather(buf):
    return _sds((N_DEV,) + buf.shape, buf.dtype), (lambda r, i: r), (lambda o, i: o.at[i])


def _plan_scatter(buf):
    return _sds(buf.shape, buf.dtype), (lambda r, i: r.at[i]), (lambda o, i: o.at[i])


def _plan_gather_rows(buf):
    l, r, c = buf.shape
    return _sds((l, N_DEV * r, c), buf.dtype), (lambda ref, i: ref), (lambda o, i: o.at[:, pl.ds(pl.multiple_of(i * r, SUB), r), :])


def _plan_scatter_rows(buf):
    r, c = buf.shape[0] // N_DEV, buf.shape[1]
    return _sds((N_DEV, r, c), buf.dtype), (lambda ref, i: ref.at[pl.ds(pl.multiple_of(i * r, SUB), r), :]), (lambda o, i: o.at[i])


def _plan_scatter_cols(buf, shard):
    ww = _window_blocks(shard) * LANE
    src = lambda ref, i: ref.at[:, pl.ds(pl.multiple_of((shard * i) // LANE * LANE, LANE), ww)]
    return _sds((N_DEV, buf.shape[0], ww), buf.dtype), src, (lambda o, i: o.at[i])


def exchange(bufs, plans, *, name, relay=False):
    nbuf = len(bufs)
    far = (2, 4, 6)

    def body(*refs):
        ins, outs = refs[:nbuf], refs[nbuf:2 * nbuf]
        send_sems, recv_sems, loc_sems = refs[2 * nbuf:2 * nbuf + 3]
        x, y, c, me = _mesh_place()
        local = [pltpu.make_async_copy(plans[b][1](ins[b], me), plans[b][2](outs[b], me), loc_sems.at[b]) for b in range(nbuf)]
        for cp in local:
            cp.start()
        sends, recvs = [], {}
        for k in ((1,) + far if relay else range(1, N_DEV)):
            peer, pidx = _peer(x, y, c, k)
            for b in range(nbuf):
                sems = dict(send_sem=send_sems.at[b, k - 1], recv_sem=recv_sems.at[b, k - 1], device_id=peer,
                            device_id_type=pl.DeviceIdType.MESH)
                cp = pltpu.make_async_remote_copy(src_ref=plans[b][1](ins[b], pidx), dst_ref=plans[b][2](outs[b], me), **sems)
                cp.start()
                sends.append(cp)
                recvs[b, k] = pltpu.make_async_remote_copy(src_ref=plans[b][1](ins[b], pidx), dst_ref=plans[b][2](outs[b], pidx), **sems)
        if relay:
            fwd_send, fwd_recv = refs[2 * nbuf + 3:]
            sibling, _ = _peer(x, y, c, 1)
            for j, k in enumerate(far):
                _, pidx = _peer(x, y, c, k)
                _, qidx = _peer(x, y, c, k + 1)
                for b in range(nbuf):
                    recvs.pop((b, k)).wait_recv()
                    sems = dict(send_sem=fwd_send.at[b, j], recv_sem=fwd_recv.at[b, j], device_id=sibling,
                                device_id_type=pl.DeviceIdType.MESH)
                    landed = plans[b][2](outs[b], pidx)
                    cp = pltpu.make_async_remote_copy(src_ref=landed, dst_ref=landed, **sems)
                    cp.start()
                    sends.append(cp)
                    there = plans[b][2](outs[b], qidx)
                    recvs[b, -k] = pltpu.make_async_remote_copy(src_ref=there, dst_ref=there, **sems)
        for cp in recvs.values():
            cp.wait_recv()
        for cp in sends:
            cp.wait_send()
        for cp in local:
            cp.wait()

    sem_shapes = [pltpu.SemaphoreType.DMA((nbuf, N_DEV - 1)), pltpu.SemaphoreType.DMA((nbuf, N_DEV - 1)), pltpu.SemaphoreType.DMA((nbuf,))]
    if relay:
        sem_shapes += [pltpu.SemaphoreType.DMA((nbuf, len(far))), pltpu.SemaphoreType.DMA((nbuf, len(far)))]
    return pl.pallas_call(
        body, name=name, in_specs=[_ANY] * nbuf, out_specs=[_ANY] * nbuf, out_shape=[p[0] for p in plans],
        scratch_shapes=sem_shapes,
        compiler_params=pltpu.CompilerParams(has_side_effects=True),
    )(*bufs)


def slot_sum(x, *, name):
    _, r, c = x.shape
    tr = _tile(r, 512, 16)

    def body(x_ref, o_ref):
        acc = x_ref[0].astype(F32)
        for d in range(1, N_DEV):
            acc = acc + x_ref[d].astype(F32)
        o_ref[...] = acc

    return pl.pallas_call(
        body, name=name, grid=(r // tr,), in_specs=[pl.BlockSpec((N_DEV, tr, c), lambda i: (0, i, 0))],
        out_specs=pl.BlockSpec((tr, c), lambda i: (i, 0)), out_shape=jax.ShapeDtypeStruct((r, c), F32),
        compiler_params=_cparams("parallel"),
    )(x)


def assemble_cols(win, shard, *, name):
    _, r, ww = win.shape
    wb = ww // LANE
    nbo = -(-(N_DEV * shard) // LANE)
    hits = [[(d, b - (shard * d) // LANE) for d in range(N_DEV) if 0 <= b - (shard * d) // LANE < wb] for b in range(nbo)]
    assert all(1 <= len(h) <= 2 for h in hits), hits
    tr = _tile(r, 512, 16)

    def body(w_ref, o_ref):
        for b, hit in enumerate(hits):
            blk = w_ref[hit[0][0], :, hit[0][1] * LANE:(hit[0][1] + 1) * LANE]
            for d, lb in hit[1:]:
                blk = blk + w_ref[d, :, lb * LANE:(lb + 1) * LANE]
            o_ref[:, b * LANE:(b + 1) * LANE] = blk

    return pl.pallas_call(
        body, name=name, grid=(r // tr,), in_specs=[pl.BlockSpec((N_DEV, tr, ww), lambda i: (0, i, 0))],
        out_specs=pl.BlockSpec((tr, nbo * LANE), lambda i: (i, 0)), out_shape=jax.ShapeDtypeStruct((r, nbo * LANE), win.dtype),
        compiler_params=_cparams("parallel"))(win)


def _pack(parts, dtype, lead=()):
    nl = len(lead)
    flat = jnp.concatenate([p.astype(dtype).reshape(lead + (-1,)) for p in parts], axis=nl)
    tot = flat.shape[nl]
    rows = -(-tot // (16 * LANE)) * 16
    flat = jnp.pad(flat, [(0, 0)] * nl + [(0, rows * LANE - tot)])
    return flat.reshape(lead + (rows, LANE))


def _unpack(buf, shapes, lead=()):
    nl = len(lead)
    flat = buf.reshape(lead + (-1,))
    out, off = [], 0
    for shp in shapes:
        size = 1
        for s in shp:
            size *= s
        out.append(lax.slice_in_dim(flat, off, off + size, axis=nl).reshape(lead + tuple(shp)))
        off += size
    return out


def _whole(g8, axis):
    t = jnp.moveaxis(g8, 0, axis)
    shp = t.shape
    return t.reshape(shp[:axis] + (shp[axis] * shp[axis + 1],) + shp[axis + 2:])


def _slots(full, axis):
    shp = full.shape
    t = full.reshape(shp[:axis] + (N_DEV, shp[axis] // N_DEV) + shp[axis + 1:])
    return jnp.moveaxis(t, axis, 0)


_PARAMS = (("meta_tokens", 1), ("norm_mix_pre", None), ("norm_mix_post", None), ("norm_ffn_pre", None), ("norm_ffn_post", None),
           ("attn_w_in", 2), ("attn_b_forget", None), ("attn_q_norm", None), ("attn_k_norm", None), ("attn_w_out", 1), ("dn_w_in", 2),
           ("dn_conv", 2), ("dn_a_log", None), ("dn_dt_bias", None), ("dn_o_norm", None), ("dn_w_out", 1), ("ffn_w_up", 2),
           ("ffn_conv", 2), ("ffn_w_down", 1))
_LOCAL_KEY = dict(meta_tokens="meta", norm_mix_pre="g_pre", norm_mix_post="g_post", norm_ffn_pre="g_fpre", norm_ffn_post="g_fpost",
                  attn_w_in="attn_w_in", attn_b_forget="attn_b", attn_q_norm="attn_qg", attn_k_norm="attn_kg", attn_w_out="attn_w_out",
                  dn_w_in="dn_w_in", dn_conv="dn_conv", dn_a_log="dn_alog", dn_dt_bias="dn_dtb", dn_o_norm="dn_og", dn_w_out="dn_w_out",
                  ffn_w_up="ffn_w_up", ffn_conv="ffn_conv", ffn_w_down="ffn_w_down")
_COL_CUT = ("attn_w_in", "dn_w_in", "ffn_w_up")
_ROW_CUT = ("attn_w_out", "dn_w_out", "ffn_w_down")


def kernel(x, meta_tokens, norm_mix_pre, norm_mix_post, norm_ffn_pre, norm_ffn_post, attn_w_in, attn_b_forget, attn_q_norm, attn_k_norm, attn_w_out, dn_w_in, dn_conv, dn_a_log, dn_dt_bias, dn_o_norm, dn_w_out, ffn_w_up, ffn_conv, ffn_w_down, loss_target, m_meta_tokens, m_norm_mix_pre, m_norm_mix_post, m_norm_ffn_pre, m_norm_ffn_post, m_attn_w_in, m_attn_b_forget, m_attn_q_norm, m_attn_k_norm, m_attn_w_out, m_dn_w_in, m_dn_conv, m_dn_a_log, m_dn_dt_bias, m_dn_o_norm, m_dn_w_out, m_ffn_w_up, m_ffn_conv, m_ffn_w_down, v_meta_tokens, v_norm_mix_pre, v_norm_mix_post, v_norm_ffn_pre, v_norm_ffn_post, v_attn_w_in, v_attn_b_forget, v_attn_q_norm, v_attn_k_norm, v_attn_w_out, v_dn_w_in, v_dn_conv, v_dn_a_log, v_dn_dt_bias, v_dn_o_norm, v_dn_w_out, v_ffn_w_up, v_ffn_conv, v_ffn_w_down):
    given = dict(locals())
    names = [p[0] for p in _PARAMS]
    cut = [p for p in _PARAMS if p[1] is not None and p[0] not in _BIG]
    rep = [p for p in _PARAMS if p[1] is None]
    me = 4 * lax.axis_index("x") + 2 * lax.axis_index("y") + lax.axis_index("c")

    bufs, plans = [], []
    for n in _COL_CUT:
        layers, d, shard = given[n].shape
        win = jnp.zeros((layers * d, _window_blocks(shard) * LANE), BF16)
        win = lax.dynamic_update_slice(win, given[n].astype(BF16).reshape(layers * d, shard), (0, (shard * me) % LANE))
        bufs.append(win)
        plans.append(_plan_gather(win))
    for n in _ROW_CUT:
        bufs.append(given[n].astype(BF16))
        plans.append(_plan_gather_rows(bufs[-1]))
    bufs.append(_pack([given[n] for n, _ in cut], F32))
    plans.append(_plan_gather(bufs[-1]))
    got = exchange(bufs, plans, name="gather_weights", relay=True)
    w = {}
    for n, g8 in zip(_COL_CUT, got[:3]):
        layers, d, shard = given[n].shape
        w[n] = assemble_cols(g8, shard, name="assemble_" + n).reshape(layers, d, -1)
    for n, full in zip(_ROW_CUT, got[3:6]):
        w[n] = full
    for (n, axis), g8 in zip(cut, _unpack(got[6], [given[n].shape for n, _ in cut], lead=(N_DEV,))):
        w[_LOCAL_KEY[n]] = _whole(g8, axis)
    for n, _ in rep:
        w[_LOCAL_KEY[n]] = given[n]

    loss, grad_x, g = local_step(x[0], loss_target[0], w)

    bufs, plans, what = [], [], []
    for n in _COL_CUT:
        for layer, gl in enumerate(g[n]):
            bufs.append(gl)
            plans.append(_plan_scatter_cols(gl, given[n].shape[2]))
            what.append((n, layer))
    for n in _ROW_CUT:
        for layer, gl in enumerate(g[n]):
            bufs.append(gl)
            plans.append(_plan_scatter_rows(gl))
            what.append((n, layer))
    nbig = len(bufs)
    bufs.append(_pack([_slots(g[_LOCAL_KEY[n]], axis) for n, axis in cut], F32, lead=(N_DEV,)))
    plans.append(_plan_scatter(bufs[-1]))
    bufs.append(_pack([g[_LOCAL_KEY[n]] for n, _ in rep] + [loss.reshape(1)], F32))
    plans.append(_plan_gather(bufs[-1]))
    got = exchange(bufs, plans, name="reduce_grads")
    per_layer = {n: [] for n in _BIG}
    for (n, layer), r8 in zip(what, got[:nbig]):
        tot = slot_sum(r8, name="sum_" + n)
        if n in _COL_CUT:
            shard = given[n].shape[2]
            tot = lax.dynamic_slice_in_dim(tot, (shard * me) % LANE, shard, axis=1)
        per_layer[n].append(tot)
    grads = {n: jnp.stack(v) for n, v in per_layer.items()}
    for (n, _), gv in zip(cut, _unpack(slot_sum(got[nbig], name="sum_cut"), [given[n].shape for n, _ in cut])):
        grads[n] = gv
    rep_sum = _unpack(slot_sum(got[nbig + 1], name="sum_rep"), [given[n].shape for n, _ in rep] + [(1,)])
    for (n, _), gv in zip(rep, rep_sum):
        grads[n] = gv
    loss_all = rep_sum[-1].reshape(())

    deltas, new_m, new_v = {}, {}, {}
    for n in names:
        shp = given[n].shape
        two_d = (-1, shp[-1])
        d, mn, vn = adamw(given[n].reshape(two_d), grads[n].reshape(two_d), given["m_" + n].reshape(two_d), given["v_" + n].reshape(two_d),
                          name="adamw_" + n)
        deltas[n], new_m[n], new_v[n] = d.reshape(shp), mn.reshape(shp), vn.reshape(shp)
    return (loss_all, grad_x[None], *[grads[n] for n in names], *[deltas[n] for n in names], *[new_m[n] for n in names],
            *[new_v[n] for n in names])
```

```python
import functools

import jax
import jax.numpy as jnp
from jax import lax
from jax.experimental import pallas as pl
from jax.experimental.pallas import tpu as pltpu

F32 = jnp.float32
BF16 = jnp.bfloat16
LANE = 128
SUB = 8
N_DEV = 8
N_META = 16
ATT_HEADS, ATT_HEAD_DIM = 16, 64
DN_HEADS, DN_HEAD_DIM, DN_CHUNK, DN_CONV = 8, 128, 64, 4
FFN_CONV = 3
EPS = 1e-6
NEG = -1e30
ADAM_LR, ADAM_B1, ADAM_B2, ADAM_EPS, ADAM_WD, ADAM_STEP = 0.001, 0.9, 0.999, 1e-08, 0.01, 10
HI = lax.Precision.HIGHEST
VMEM_LIMIT = 56 * 1024 * 1024


def _tile(n, target, align=LANE):
    if n <= target:
        return n
    best = None
    for t in range(align, target + 1, align):
        if n % t == 0:
            best = t
    assert best is not None, (n, target, align)
    return best


def _iota(shape, dim):
    return lax.broadcasted_iota(jnp.int32, shape, dim)


def _colsum8(x):
    r, c = x.shape
    return x.reshape(r // SUB, SUB, c).sum(axis=0)


def _cparams(*sem):
    return pltpu.CompilerParams(dimension_semantics=sem, vmem_limit_bytes=VMEM_LIMIT)


def _sigmoid(x):
    return 1.0 / (1.0 + jnp.exp(-x))


def _softplus(x):
    return jnp.maximum(x, 0.0) + jnp.log(1.0 + jnp.exp(-jnp.abs(x)))


def _accum(ref, part, first):
    @pl.when(first)
    def _():
        ref[...] = part

    @pl.when(jnp.logical_not(first))
    def _():
        ref[...] += part


def _lockstep(gens):
    gens = list(gens)
    out = [None] * len(gens)
    live = list(range(len(gens)))
    while live:
        nxt = []
        for i in live:
            try:
                next(gens[i])
                nxt.append(i)
            except StopIteration as stop:
                out[i] = stop.value
        live = nxt
    return out


def matmul(a, b, *, trans_a=False, trans_b=False, out_dtype=F32, tm=1664, tn=1408, tk=1664, name="matmul"):
    if trans_a:
        kdim, m = a.shape
    else:
        m, kdim = a.shape
    if trans_b:
        n, kb = b.shape
    else:
        kb, n = b.shape
    assert kb == kdim, (a.shape, b.shape, trans_a, trans_b)
    tm, tn, tk = _tile(m, tm), _tile(n, tn), _tile(kdim, tk)
    nk = kdim // tk
    dims = (((0 if trans_a else 1,), (1 if trans_b else 0,)), ((), ()))
    cdt = BF16

    def body(a_ref, b_ref, o_ref, *acc):
        part = lax.dot_general(a_ref[...].astype(cdt), b_ref[...].astype(cdt), dims, preferred_element_type=F32)
        if nk == 1:
            o_ref[...] = part.astype(o_ref.dtype)
            return
        k = pl.program_id(2)
        _accum(acc[0], part, k == 0)

        @pl.when(k == nk - 1)
        def _():
            o_ref[...] = acc[0][...].astype(o_ref.dtype)

    a_spec = pl.BlockSpec((tk, tm), lambda i, j, k: (k, i)) if trans_a else pl.BlockSpec((tm, tk), lambda i, j, k: (i, k))
    b_spec = pl.BlockSpec((tn, tk), lambda i, j, k: (j, k)) if trans_b else pl.BlockSpec((tk, tn), lambda i, j, k: (k, j))
    return pl.pallas_call(
        body,
        name=name,
        grid=(m // tm, n // tn, nk),
        in_specs=[a_spec, b_spec],
        out_specs=pl.BlockSpec((tm, tn), lambda i, j, k: (i, j)),
        out_shape=jax.ShapeDtypeStruct((m, n), out_dtype),
        scratch_shapes=[] if nk == 1 else [pltpu.VMEM((tm, tn), F32)],
        compiler_params=_cparams("parallel", "parallel", "arbitrary"),
    )(a, b)


def rms_fwd(x, g, *, res=None, out_dtype, name):
    n, d = x.shape
    tr = _tile(n, 640, SUB)

    def body(*refs):
        x_ref, g_ref = refs[0], refs[1]
        o_ref = refs[-1]
        xv = x_ref[...]
        y = xv * lax.rsqrt(jnp.mean(xv * xv, axis=-1, keepdims=True) + EPS) * g_ref[...]
        if res is not None:
            y = y + refs[2][...]
        o_ref[...] = y.astype(o_ref.dtype)

    row = pl.BlockSpec((tr, d), lambda i: (i, 0))
    ins = [x, g] + ([res] if res is not None else [])
    return pl.pallas_call(
        body, name=name, grid=(n // tr,),
        in_specs=[row, pl.BlockSpec((1, d), lambda i: (0, 0))] + ([row] if res is not None else []),
        out_specs=row, out_shape=jax.ShapeDtypeStruct((n, d), out_dtype),
        compiler_params=_cparams("parallel"),
    )(*ins)


def rms_bwd(x, g, dy, *, res=None, pad, name):
    n, d = x.shape
    tr = _tile(n, 640, SUB)

    def body(*refs):
        x_ref, g_ref, dy_ref = refs[:3]
        dx_ref, dg_ref = refs[-2:]
        i = pl.program_id(0)
        xv = x_ref[...]
        r = lax.rsqrt(jnp.mean(xv * xv, axis=-1, keepdims=True) + EPS)
        xh = xv * r
        dyv = dy_ref[...].astype(F32)
        gdy = dyv * g_ref[...]
        dx = r * (gdy - xh * jnp.mean(xh * gdy, axis=-1, keepdims=True))
        if res is not None:
            dx = dx + refs[3][...]
        rows = i * tr + _iota((tr, 1), 0)
        dx_ref[...] = jnp.where(rows >= pad, dx, 0.0)
        _accum(dg_ref, _colsum8(dyv * xh), i == 0)

    row = pl.BlockSpec((tr, d), lambda i: (i, 0))
    ins = [x, g, dy] + ([res] if res is not None else [])
    return pl.pallas_call(
        body, name=name, grid=(n // tr,),
        in_specs=[row, pl.BlockSpec((1, d), lambda i: (0, 0)), row] + ([row] if res is not None else []),
        out_specs=[row, pl.BlockSpec((SUB, d), lambda i: (0, 0))],
        out_shape=[jax.ShapeDtypeStruct((n, d), F32), jax.ShapeDtypeStruct((SUB, d), F32)],
        compiler_params=_cparams("arbitrary"),
    )(*ins)


def _halo_rows(dtype):
    return SUB * 4 // jnp.dtype(dtype).itemsize


def _shift_down(cur, prev, s):
    if s == 0:
        return cur
    hb = prev.shape[0]
    out = pltpu.roll(cur, s, 0)
    row = _iota(cur.shape, 0)
    for r in range(s):
        out = jnp.where(row == r, prev[hb - s + r:hb - s + r + 1, :], out)
    return out


def _shift_up(cur, next8, s):
    if s == 0:
        return cur
    tr = cur.shape[0]
    out = pltpu.roll(cur, tr - s, 0)
    row = _iota(cur.shape, 0)
    for r in range(s):
        out = jnp.where(row == tr - s + r, next8[r:r + 1, :], out)
    return out


def _conv_rows(cur, prev8, w):
    kw = w.shape[0]
    acc = w[kw - 1:kw, :] * cur
    for k in range(kw - 1):
        acc = acc + w[k:k + 1, :] * _shift_down(cur, prev8, kw - 1 - k)
    return acc


def conv_transpose(dy, w, *, dy_hi=None, out_dtype, name):
    n, c_in = dy.shape
    c = c_in if dy_hi is None else 2 * c_in
    kw = w.shape[0]
    tr = _tile(n, 640, SUB)
    tc = _tile(c_in, 512)
    nlo = c_in // tc
    hb = _halo_rows(dy.dtype)
    nbh = n // hb

    def body(*refs):
        w_ref, o_ref = refs[-2:]
        i, j = pl.program_id(0), pl.program_id(1)

        def run(cur_ref, nxt_ref):
            cur = cur_ref[...].astype(F32)
            nxt = jnp.where(i == pl.num_programs(0) - 1, 0.0, nxt_ref[...].astype(F32))
            wv = w_ref[...]
            acc = wv[kw - 1:kw, :] * cur
            for k in range(kw - 1):
                acc = acc + wv[k:k + 1, :] * _shift_up(cur, nxt, kw - 1 - k)
            o_ref[...] = acc.astype(o_ref.dtype)

        if dy_hi is None:
            run(refs[0], refs[1])
        else:
            pl.when(j < nlo)(lambda: run(refs[0], refs[1]))
            pl.when(j >= nlo)(lambda: run(refs[2], refs[3]))

    def pair(col):
        return [pl.BlockSpec((tr, tc), lambda i, j: (i, col(j))),
                pl.BlockSpec((hb, tc), lambda i, j: (jnp.minimum((i + 1) * (tr // hb), nbh - 1), col(j)))]

    if dy_hi is None:
        srcs, specs = [dy, dy], pair(lambda j: j)
    else:
        srcs = [dy, dy, dy_hi, dy_hi]
        specs = pair(lambda j: jnp.minimum(j, nlo - 1)) + pair(lambda j: jnp.maximum(j - nlo, 0))
    return pl.pallas_call(
        body, name=name, grid=(n // tr, c // tc),
        in_specs=specs + [pl.BlockSpec((kw, tc), lambda i, j: (0, j))],
        out_specs=pl.BlockSpec((tr, tc), lambda i, j: (i, j)),
        out_shape=jax.ShapeDtypeStruct((n, c), out_dtype),
        compiler_params=_cparams("parallel", "parallel"),
    )(*srcs, w)


_GELU_C = 0.7978845608028654
_GELU_A = 0.044715


def _gelu(x):
    return 0.5 * x * (1.0 + jnp.tanh(_GELU_C * (x + _GELU_A * x * x * x)))


def _gelu_grad(x):
    th = jnp.tanh(_GELU_C * (x + _GELU_A * x * x * x))
    return 0.5 * (1.0 + th) + 0.5 * x * (1.0 - th * th) * _GELU_C * (1.0 + 3.0 * _GELU_A * x * x)


def conv_glu_fwd(u, cw, *, name):
    n, f2 = u.shape
    f = f2 // 2
    tr = _tile(n, 640, SUB)
    tc = _tile(f, 512)
    nf = f // tc
    hb = _halo_rows(u.dtype)
    r8 = tr // hb

    def body(ug_ref, uu_ref, pg_ref, pu_ref, wg_ref, wu_ref, o_ref):
        first = pl.program_id(0) == 0
        pg = jnp.where(first, 0.0, pg_ref[...].astype(F32))
        pu = jnp.where(first, 0.0, pu_ref[...].astype(F32))
        gate = _conv_rows(ug_ref[...].astype(F32), pg, wg_ref[...])
        up = _conv_rows(uu_ref[...].astype(F32), pu, wu_ref[...])
        o_ref[...] = (_gelu(gate) * up).astype(o_ref.dtype)

    prev = lambda off: pl.BlockSpec((hb, tc), lambda i, j: (jnp.maximum(i * r8 - 1, 0), j + off))
    return pl.pallas_call(
        body, name=name, grid=(n // tr, nf),
        in_specs=[pl.BlockSpec((tr, tc), lambda i, j: (i, j)), pl.BlockSpec((tr, tc), lambda i, j: (i, j + nf)),
                  prev(0), prev(nf),
                  pl.BlockSpec((FFN_CONV, tc), lambda i, j: (0, j)), pl.BlockSpec((FFN_CONV, tc), lambda i, j: (0, j + nf))],
        out_specs=pl.BlockSpec((tr, tc), lambda i, j: (i, j)),
        out_shape=jax.ShapeDtypeStruct((n, f), BF16),
        compiler_params=_cparams("parallel", "parallel"),
    )(u, u, u, u, cw, cw)


def _gelu_both(x):
    th = jnp.tanh(_GELU_C * (x + _GELU_A * x * x * x))
    return 0.5 * x * (1.0 + th), 0.5 * (1.0 + th) + 0.5 * x * (1.0 - th * th) * _GELU_C * (1.0 + 3.0 * _GELU_A * x * x)


def conv_glu_bwd(u, cw, dact, *, name):
    n, f2 = u.shape
    f = f2 // 2
    tr = _tile(n, 640, SUB)
    tc = _tile(f, 512)
    nf = f // tc
    hb = _halo_rows(u.dtype)
    r8 = tr // hb

    def body(ug_ref, uu_ref, pg_ref, pu_ref, wg_ref, wu_ref, da_ref, og_ref, ou_ref, dwg_ref, dwu_ref):
        first = pl.program_id(1) == 0
        pg = jnp.where(first, 0.0, pg_ref[...].astype(F32))
        pu = jnp.where(first, 0.0, pu_ref[...].astype(F32))
        curg, curu = ug_ref[...].astype(F32), uu_ref[...].astype(F32)
        gate = _conv_rows(curg, pg, wg_ref[...])
        up = _conv_rows(curu, pu, wu_ref[...])
        da = da_ref[...].astype(F32)
        gel, gel_grad = _gelu_both(gate)
        d_gate, d_up = da * up * gel_grad, da * gel
        og_ref[...] = d_gate.astype(og_ref.dtype)
        ou_ref[...] = d_up.astype(ou_ref.dtype)
        for k in range(FFN_CONV):
            part_g = _colsum8(d_gate * _shift_down(curg, pg, FFN_CONV - 1 - k))
            part_u = _colsum8(d_up * _shift_down(curu, pu, FFN_CONV - 1 - k))

            @pl.when(first)
            def _():
                dwg_ref[k] = part_g
                dwu_ref[k] = part_u

            @pl.when(jnp.logical_not(first))
            def _():
                dwg_ref[k] += part_g
                dwu_ref[k] += part_u

    tile = lambda off: pl.BlockSpec((tr, tc), lambda j, i: (i, j + off))
    prev = lambda off: pl.BlockSpec((hb, tc), lambda j, i: (jnp.maximum(i * r8 - 1, 0), j + off))
    wsp = lambda off: pl.BlockSpec((FFN_CONV, tc), lambda j, i: (0, j + off))
    acc = pl.BlockSpec((FFN_CONV, SUB, tc), lambda j, i: (0, 0, j))
    return pl.pallas_call(
        body, name=name, grid=(nf, n // tr),
        in_specs=[tile(0), tile(nf), prev(0), prev(nf), wsp(0), wsp(nf), tile(0)],
        out_specs=[tile(0), tile(0), acc, acc],
        out_shape=[jax.ShapeDtypeStruct((n, f), BF16)] * 2 + [jax.ShapeDtypeStruct((FFN_CONV, SUB, f), F32)] * 2,
        compiler_params=_cparams("parallel", "arbitrary"),
    )(u, u, u, u, cw, cw, dact)


def cumsum_rows(x, *, reverse, name):
    n, c = x.shape
    tr = LANE
    nb = n // tr

    def body(x_ref, o_ref, carry_ref):
        i = pl.program_id(0)

        @pl.when(i == 0)
        def _():
            carry_ref[...] = jnp.zeros_like(carry_ref)

        r, cc = _iota((tr, tr), 0), _iota((tr, tr), 1)
        tri = jnp.where((cc >= r) if reverse else (cc <= r), 1.0, 0.0).astype(F32)
        out = jnp.dot(tri, x_ref[...], precision=HI, preferred_element_type=F32) + carry_ref[...]
        o_ref[...] = out
        carry_ref[...] = out[0:1, :] if reverse else out[tr - 1:tr, :]

    idx = (lambda i: (nb - 1 - i, 0)) if reverse else (lambda i: (i, 0))
    return pl.pallas_call(
        body, name=name, grid=(nb,),
        in_specs=[pl.BlockSpec((tr, c), idx)], out_specs=pl.BlockSpec((tr, c), idx),
        out_shape=jax.ShapeDtypeStruct((n, c), F32), scratch_shapes=[pltpu.VMEM((1, c), F32)],
        compiler_params=_cparams("arbitrary"),
    )(x)


def _group_sum64(x):
    r, c = x.shape
    a, b = _iota((LANE, LANE), 0), _iota((LANE, LANE), 1)
    bd = jnp.where((a // 64) == (b // 64), 1.0, 0.0).astype(F32)
    parts = [jnp.dot(x[:, k * LANE:(k + 1) * LANE], bd, precision=HI, preferred_element_type=F32) for k in range(c // LANE)]
    return parts[0] if len(parts) == 1 else jnp.concatenate(parts, axis=1)


def attn_prep(proj, qg, kg, bf, *, hd, name):
    n = proj.shape[0]
    tr = _tile(n, 640, SUB)
    scale = ATT_HEAD_DIM ** -0.5
    nh = hd // LANE

    def body(q_ref, k_ref, v_ref, f_ref, qg_ref, kg_ref, bf_ref, qo_ref, ko_ref, vo_ref, lf_ref):
        def norm(x, g):
            ms = _group_sum64(x * x) * (1.0 / ATT_HEAD_DIM)
            return x * lax.rsqrt(ms + EPS) * g

        qo_ref[...] = (norm(q_ref[...], qg_ref[...]) * scale).astype(qo_ref.dtype)
        ko_ref[...] = norm(k_ref[...], kg_ref[...]).astype(ko_ref.dtype)
        vo_ref[...] = v_ref[...].astype(vo_ref.dtype)
        lf_ref[...] = -_softplus(-(f_ref[...] + bf_ref[...]))

    col = lambda c: pl.BlockSpec((tr, hd), lambda i: (i, c))
    vec = lambda w: pl.BlockSpec((1, w), lambda i: (0, 0))
    return pl.pallas_call(
        body, name=name, grid=(n // tr,),
        in_specs=[col(0), col(1), col(2), pl.BlockSpec((tr, LANE), lambda i: (i, 4 * nh)), vec(hd), vec(hd), vec(LANE)],
        out_specs=[col(0), col(0), col(0), pl.BlockSpec((tr, LANE), lambda i: (i, 0))],
        out_shape=[jax.ShapeDtypeStruct((n, hd), BF16)] * 3 + [jax.ShapeDtypeStruct((n, LANE), F32)],
        compiler_params=_cparams("parallel"),
    )(proj, proj, proj, proj, qg, kg, bf)


def _half_mask(shape):
    return _iota(shape, 1) < ATT_HEAD_DIM


def flash_fwd(qs, kn, v, proj, ct, *, pad, t, name):
    n, hd = qs.shape
    npair = hd // LANE
    nb = n // t
    gate0 = 3 * npair

    def body(q_ref, k_ref, v_ref, g_ref, c_ref, o_ref, og_ref, lse_ref, m_ref, acc_ref):
        i, j = pl.program_id(1), pl.program_id(2)

        @pl.when(j == 0)
        def _():
            m_ref[...] = jnp.full_like(m_ref, NEG)
            acc_ref[...] = jnp.zeros_like(acc_ref)

        def step(masked):
            q, k, vv = q_ref[...], k_ref[...], v_ref[...]
            half0 = _half_mask(q.shape)
            if masked:
                rowpos = i * t + _iota((t, t), 0)
                colpos = j * t + _iota((t, t), 1)
                mask = (colpos <= rowpos) & (colpos >= pad)

            def head(hh):
                sel = half0 if hh == 0 else jnp.logical_not(half0)
                qm = jnp.where(sel, q, jnp.zeros_like(q))
                v1 = jnp.where(sel, vv, jnp.ones_like(vv))
                s = lax.dot_general(qm, k, (((1,), (1,)), ((), ())), preferred_element_type=F32) - c_ref[0, hh:hh + 1, :]
                yield
                if masked:
                    s = jnp.where(mask, s, NEG)
                m_prev = m_ref[hh]
                m_new = jnp.maximum(m_prev, jnp.max(s, axis=1, keepdims=True))
                p = jnp.exp(s - m_new[:, 0:1])
                if masked:
                    p = jnp.where(mask, p, 0.0)
                yield
                acc_ref[hh] = jnp.exp(m_prev - m_new) * acc_ref[hh] + jnp.dot(p.astype(vv.dtype), v1, preferred_element_type=F32)
                m_ref[hh] = m_new

            _lockstep(head(hh) for hh in range(2))

        edge = (j == i) | (j == 0)

        @pl.when(edge & (j <= i))
        def _():
            step(True)

        @pl.when(jnp.logical_not(edge) & (j < i))
        def _():
            step(False)

        @pl.when(j == i)
        def _():
            half0 = _half_mask((t, LANE))
            a0, a1 = acc_ref[0], acc_ref[1]
            l = jnp.where(half0, a0[:, ATT_HEAD_DIM:ATT_HEAD_DIM + 1], a1[:, 0:1])
            acc = jnp.where(half0, a0, a1)
            m = jnp.where(half0, m_ref[0], m_ref[1])
            live = l > 0.0
            o = jnp.where(live, acc / jnp.where(live, l, 1.0), 0.0)
            o_ref[...] = o
            og_ref[...] = (o * _sigmoid(g_ref[...])).astype(og_ref.dtype)
            lse_ref[...] = jnp.where(live, m + jnp.log(jnp.where(live, l, 1.0)), 0.0)

    qspec = pl.BlockSpec((t, LANE), lambda p, i, j: (i, p))
    kspec = pl.BlockSpec((t, LANE), lambda p, i, j: (jnp.minimum(j, i), p))
    return pl.pallas_call(
        body, name=name, grid=(npair, nb, nb),
        in_specs=[qspec, kspec, kspec, pl.BlockSpec((t, LANE), lambda p, i, j: (i, gate0 + p)),
                  pl.BlockSpec((1, 2, t), lambda p, i, j: (p, 0, jnp.minimum(j, i)))],
        out_specs=[qspec, qspec, qspec],
        out_shape=[jax.ShapeDtypeStruct((n, hd), F32), jax.ShapeDtypeStruct((n, hd), BF16), jax.ShapeDtypeStruct((n, hd), F32)],
        scratch_shapes=[pltpu.VMEM((2, t, LANE), F32)] * 2,
        compiler_params=_cparams("parallel", "parallel", "arbitrary"),
    )(qs, kn, v, proj, ct)


def attn_bwd_prep(dgated, o, proj, *, hd, name):
    n = o.shape[0]
    tr = _tile(n, 640, SUB)
    gate0 = 3

    def body(dg_ref, o_ref, g_ref, do_ref, dl_ref, dgate_ref):
        dg, ov = dg_ref[...], o_ref[...]
        sg = _sigmoid(g_ref[...])
        do = dg * sg
        do_ref[...] = do.astype(do_ref.dtype)
        dl_ref[...] = _group_sum64(do * ov)
        dgate_ref[...] = dg * ov * sg * (1.0 - sg)

    row = pl.BlockSpec((tr, hd), lambda i: (i, 0))
    return pl.pallas_call(
        body, name=name, grid=(n // tr,),
        in_specs=[row, row, pl.BlockSpec((tr, hd), lambda i: (i, gate0))],
        out_specs=[row, row, row],
        out_shape=[jax.ShapeDtypeStruct((n, hd), BF16), jax.ShapeDtypeStruct((n, hd), F32), jax.ShapeDtypeStruct((n, hd), F32)],
        compiler_params=_cparams("parallel"),
    )(dgated, o, proj)


def flash_bwd(qs, kn, v, do, lse, delta, ct, *, pad, t, name):
    n, hd = qs.shape
    npair = hd // LANE
    nb = n // t

    def body(q_ref, k_ref, v_ref, do_ref, lse_ref, dl_ref, c_ref, dq_ref, dk_ref, dv_ref, dck_ref, dcr_ref, dk_acc, dv_acc, dck_acc):
        j, i = pl.program_id(1), pl.program_id(2)

        @pl.when((j == 0) & (i == 0))
        def _():
            dq_ref[...] = jnp.zeros_like(dq_ref)
            dcr_ref[...] = jnp.zeros_like(dcr_ref)

        @pl.when(i == j)
        def _():
            dk_acc[...] = jnp.zeros_like(dk_acc)
            dv_acc[...] = jnp.zeros_like(dv_acc)
            dck_acc[...] = jnp.zeros_like(dck_acc)

        def step(masked):
            q, k, vv, dov = q_ref[...], k_ref[...], v_ref[...], do_ref[...]
            half0 = _half_mask(q.shape)
            if masked:
                rowpos = i * t + _iota((t, t), 0)
                colpos = j * t + _iota((t, t), 1)
                mask = (colpos <= rowpos) & (colpos >= pad)
            nt = (((1,), (1,)), ((), ()))
            tn = (((0,), (0,)), ((), ()))
            dq_h, dk_h, dv_h = [], [], []
            for hh in range(2):
                sel = half0 if hh == 0 else jnp.logical_not(half0)
                qm = jnp.where(sel, q, jnp.zeros_like(q))
                dom = jnp.where(sel, dov, jnp.zeros_like(dov))
                q1 = jnp.where(sel, q, jnp.ones_like(q))
                k1 = jnp.where(sel, k, jnp.ones_like(k))
                x = lax.dot_general(qm, k, nt, preferred_element_type=F32) - c_ref[0, hh:hh + 1, :] - lse_ref[:, hh * 64:hh * 64 + 1]
                if masked:
                    p = jnp.where(mask, jnp.exp(jnp.where(mask, x, NEG)), 0.0)
                else:
                    p = jnp.exp(x)
                dp = lax.dot_general(dom, vv, nt, preferred_element_type=F32)
                ds = p * (dp - dl_ref[:, hh * 64:hh * 64 + 1])
                dsb, pb = ds.astype(k.dtype), p.astype(k.dtype)
                dq_h.append(jnp.dot(dsb, k1, preferred_element_type=F32))
                dk_h.append(lax.dot_general(dsb, q1, tn, preferred_element_type=F32))
                dv_h.append(lax.dot_general(pb, dov, tn, preferred_element_type=F32))
            rows = pl.ds(pl.multiple_of(i * t, t), t)
            dq_ref[rows, :] += jnp.where(half0, dq_h[0], dq_h[1])
            dcr_ref[rows, :] += jnp.where(half0, dq_h[1], dq_h[0])
            dk_acc[...] += jnp.where(half0, dk_h[0], dk_h[1])
            dck_acc[...] += jnp.where(half0, dk_h[1], dk_h[0])
            dv_acc[...] += jnp.where(half0, dv_h[0], dv_h[1])

        edge = (i == j) | (j == 0)

        @pl.when(edge & (i >= j))
        def _():
            step(True)

        @pl.when(jnp.logical_not(edge) & (i > j))
        def _():
            step(False)

        @pl.when(i == nb - 1)
        def _():
            dk_ref[...] = dk_acc[...]
            dv_ref[...] = dv_acc[...]
            dck_ref[...] = dck_acc[...]

    qspec = pl.BlockSpec((t, LANE), lambda p, j, i: (jnp.maximum(i, j), p))
    kspec = pl.BlockSpec((t, LANE), lambda p, j, i: (j, p))
    cspec = pl.BlockSpec((1, 2, t), lambda p, j, i: (p, 0, j))
    whole = pl.BlockSpec((n, LANE), lambda p, j, i: (0, p))
    return pl.pallas_call(
        body, name=name, grid=(npair, nb, nb),
        in_specs=[qspec, kspec, kspec, qspec, qspec, qspec, cspec],
        out_specs=[whole, kspec, kspec, kspec, whole],
        out_shape=[jax.ShapeDtypeStruct((n, hd), F32)] * 5,
        scratch_shapes=[pltpu.VMEM((t, LANE), F32)] * 3,
        compiler_params=_cparams("parallel", "arbitrary", "arbitrary"),
    )(qs, kn, v, do, lse, delta, ct)


def attn_in_bwd(dqs, dkn, proj, qg, kg, bf, dlogf, *, hd, pad, name):
    n = proj.shape[0]
    tr = _tile(n, 640, SUB)
    scale = ATT_HEAD_DIM ** -0.5
    nh = hd // LANE

    def body(dq_ref, dk_ref, q_ref, k_ref, f_ref, qg_ref, kg_ref, bf_ref, dl_ref, oq_ref, ok_ref, of_ref, gq_ref, gk_ref, gb_ref):
        i = pl.program_id(0)

        def back(x, g, dy):
            r = lax.rsqrt(_group_sum64(x * x) * (1.0 / ATT_HEAD_DIM) + EPS)
            xh = x * r
            gdy = dy * g
            dx = r * (gdy - xh * _group_sum64(xh * gdy) * (1.0 / ATT_HEAD_DIM))
            return dx, _colsum8(dy * xh)

        dxq, gq = back(q_ref[...], qg_ref[...], dq_ref[...] * scale)
        dxk, gk = back(k_ref[...], kg_ref[...], dk_ref[...])
        oq_ref[...] = dxq.astype(oq_ref.dtype)
        ok_ref[...] = dxk.astype(ok_ref.dtype)
        rows = i * tr + _iota((tr, 1), 0)
        dfl = jnp.where(rows >= pad, dl_ref[...] * _sigmoid(-(f_ref[...] + bf_ref[...])), 0.0)
        of_ref[...] = dfl.astype(of_ref.dtype)
        _accum(gq_ref, gq, i == 0)
        _accum(gk_ref, gk, i == 0)
        _accum(gb_ref, _colsum8(dfl), i == 0)

    row = pl.BlockSpec((tr, hd), lambda i: (i, 0))
    col = lambda c: pl.BlockSpec((tr, hd), lambda i: (i, c))
    nar = pl.BlockSpec((tr, LANE), lambda i: (i, 0))
    vec = lambda w: pl.BlockSpec((1, w), lambda i: (0, 0))
    acc = lambda w: pl.BlockSpec((SUB, w), lambda i: (0, 0))
    return pl.pallas_call(
        body, name=name, grid=(n // tr,),
        in_specs=[row, row, col(0), col(1), pl.BlockSpec((tr, LANE), lambda i: (i, 4 * nh)), vec(hd), vec(hd), vec(LANE), nar],
        out_specs=[row, row, nar, acc(hd), acc(hd), acc(LANE)],
        out_shape=[jax.ShapeDtypeStruct((n, hd), BF16)] * 2 + [jax.ShapeDtypeStruct((n, LANE), BF16),
                   jax.ShapeDtypeStruct((SUB, hd), F32), jax.ShapeDtypeStruct((SUB, hd), F32), jax.ShapeDtypeStruct((SUB, LANE), F32)],
        compiler_params=_cparams("arbitrary"),
    )(dqs, dkn, proj, proj, proj, qg, kg, bf, dlogf)


def _silu(x):
    return x * _sigmoid(x)


def _silu_grad(x):
    s = _sigmoid(x)
    return s * (1.0 + x * (1.0 - s))


def gdn_prep(proj, cw, *, hd, name):
    n = proj.shape[0]
    tr = _tile(n, 640, SUB)
    nh = hd // LANE
    r8 = tr // SUB
    qscale = DN_HEAD_DIM ** -0.5

    def body(x_ref, p_ref, w_ref, o_ref):
        i, c = pl.program_id(0), pl.program_id(1)
        prev = jnp.where(i == 0, 0.0, p_ref[...])
        s = _silu(_conv_rows(x_ref[...], prev, w_ref[...]))
        r = lax.rsqrt(jnp.sum(s * s, axis=-1, keepdims=True) + EPS)
        mult = jnp.where(c < nh, r * qscale, jnp.where(c < 2 * nh, r, 1.0))
        o_ref[...] = s * mult

    return pl.pallas_call(
        body, name=name, grid=(n // tr, 3 * nh),
        in_specs=[pl.BlockSpec((tr, LANE), lambda i, c: (i, c)),
                  pl.BlockSpec((SUB, LANE), lambda i, c: (jnp.maximum(i * r8 - 1, 0), c)),
                  pl.BlockSpec((DN_CONV, LANE), lambda i, c: (0, c))],
        out_specs=pl.BlockSpec((tr, LANE), lambda i, c: (i, c)),
        out_shape=jax.ShapeDtypeStruct((n, 3 * hd), F32),
        compiler_params=_cparams("parallel", "parallel"),
    )(proj, proj, cw)


def _chunk_tri(reverse):
    r, c = _iota((LANE, LANE), 0), _iota((LANE, LANE), 1)
    same = (r // DN_CHUNK) == (c // DN_CHUNK)
    return jnp.where(same & ((c >= r) if reverse else (c <= r)), 1.0, 0.0).astype(F32)


def gdn_gates(proj, alog, dtb, *, hd, name):
    n = proj.shape[0]
    gcol = 4 * (hd // LANE)

    def body(x_ref, a_ref, d_ref, o_ref):
        x = x_ref[...]
        lane = _iota(x.shape, 1)
        g = -jnp.exp(a_ref[...]) * _softplus(x + d_ref[...])
        gc = jnp.dot(_chunk_tri(False), jnp.where((lane >= DN_HEADS) & (lane < 2 * DN_HEADS), g, 0.0), precision=HI,
                     preferred_element_type=F32)
        o_ref[...] = jnp.where(lane < DN_HEADS, _sigmoid(x), gc)

    vec = pl.BlockSpec((1, LANE), lambda i: (0, 0))
    return pl.pallas_call(
        body, name=name, grid=(n // LANE,),
        in_specs=[pl.BlockSpec((LANE, LANE), lambda i: (i, gcol)), vec, vec],
        out_specs=pl.BlockSpec((LANE, LANE), lambda i: (i, 0)),
        out_shape=jax.ShapeDtypeStruct((n, LANE), F32),
        compiler_params=_cparams("parallel"),
    )(proj, alog, dtb)


def _mm(a, b, ca=1, cb=0):
    return lax.dot_general(a.astype(BF16), b.astype(BF16), (((ca,), (cb,)), ((), ())), preferred_element_type=F32)


def _mmh(a, b):
    return jnp.dot(a, b, precision=HI, preferred_element_type=F32)


def _gdn_common(q, k, v, beta, gc_c, gc_r):
    r, c = _iota((LANE, LANE), 0), _iota((LANE, LANE), 1)
    same = (r // DN_CHUNK) == (c // DN_CHUNK)
    incl, strict = same & (r >= c), same & (r > c)
    d = jnp.exp(jnp.where(incl, gc_c - gc_r, NEG))
    kk = _mm(k, k, 1, 1)
    qk = _mm(q, k, 1, 1)
    yield
    ahat = jnp.where(strict, kk * d, 0.0)
    a = ahat * beta
    eye = jnp.where(r == c, 1.0, 0.0).astype(F32)
    t = eye - a
    pw = _mmh(a, a)
    yield
    for step in range(5):
        t = t + _mmh(t, pw)
        if step < 4:
            pw = _mmh(pw, pw)
        yield
    row = _iota((LANE, 1), 0)
    gl0 = jnp.sum(jnp.where(row == DN_CHUNK - 1, gc_c, 0.0), axis=0, keepdims=True)
    gl1 = jnp.sum(jnp.where(row == LANE - 1, gc_c, 0.0), axis=0, keepdims=True)
    gam = jnp.exp(gc_c)
    lam = jnp.exp(jnp.where(row < DN_CHUNK, gl0, gl1) - gc_c)
    kb, vb = k * (beta * gam), v * beta
    cm = dict(incl=incl, strict=strict, d=d, kk=kk, ahat=ahat, t=t, gam=gam, lam=lam, kb=kb, vb=vb, w=_mm(t, kb), u0=_mm(t, vb),
              qk=qk, pm=jnp.where(incl, qk * d, 0.0), qg=q * gam, kl=k * lam, g0=jnp.exp(gl0), g1=jnp.exp(gl1))
    yield
    return cm


def _gdn_states(cm, s0):
    c = DN_CHUNK
    u_a = cm["u0"][:c] - _mm(cm["w"][:c], s0, 1, 1)
    yield
    s1 = cm["g0"] * s0 + _mm(u_a, cm["kl"][:c], 0, 0)
    yield
    u_b = cm["u0"][c:] - _mm(cm["w"][c:], s1, 1, 1)
    yield
    s2 = cm["g1"] * s1 + _mm(u_b, cm["kl"][c:], 0, 0)
    yield
    return u_a, s1, u_b, s2


def gdn_chunk_fwd(qkv, bg, bgt, proj, ogain, *, hd, name):
    n = qkv.shape[0]
    nb = n // LANE
    nh = hd // LANE
    c = DN_CHUNK

    def body(q_ref, k_ref, v_ref, bg_ref, bgt_ref, g_ref, gain_ref, o_ref, og_ref, hist_ref, s_ref):
        @pl.when(pl.program_id(0) == 0)
        def _():
            s_ref[...] = jnp.zeros_like(s_ref)

        hist_ref[0] = s_ref[...]

        def head(h):
            cols = slice(h * LANE, (h + 1) * LANE)
            cm = yield from _gdn_common(q_ref[:, cols], k_ref[:, cols], v_ref[:, cols], bg_ref[:, h:h + 1],
                                        bg_ref[:, nh + h:nh + h + 1], bgt_ref[nh + h:nh + h + 1, :])
            s0 = s_ref[h]
            u_a, s1, u_b, s2 = yield from _gdn_states(cm, s0)
            u_all = jnp.concatenate([u_a, u_b], axis=0)
            o = jnp.concatenate([_mm(cm["qg"][:c], s0, 1, 1), _mm(cm["qg"][c:], s1, 1, 1)], axis=0) + _mm(cm["pm"], u_all)
            s_ref[h] = s2
            o_ref[:, cols] = o
            rn = lax.rsqrt(jnp.mean(o * o, axis=-1, keepdims=True) + EPS)
            og_ref[:, cols] = (o * rn * gain_ref[...] * _silu(g_ref[:, cols])).astype(og_ref.dtype)

        _lockstep(head(h) for h in range(nh))

    col = lambda cc: pl.BlockSpec((LANE, hd), lambda b: (b, cc))
    return pl.pallas_call(
        body, name=name, grid=(nb,),
        in_specs=[col(0), col(1), col(2), pl.BlockSpec((LANE, LANE), lambda b: (b, 0)),
                  pl.BlockSpec((2 * nh, LANE), lambda b: (0, b)), pl.BlockSpec((LANE, hd), lambda b: (b, 3)),
                  pl.BlockSpec((1, LANE), lambda b: (0, 0))],
        out_specs=[col(0), col(0), pl.BlockSpec((1, nh, LANE, LANE), lambda b: (b, 0, 0, 0))],
        out_shape=[jax.ShapeDtypeStruct((n, hd), F32), jax.ShapeDtypeStruct((n, hd), BF16),
                   jax.ShapeDtypeStruct((nb, nh, LANE, LANE), F32)],
        scratch_shapes=[pltpu.VMEM((nh, LANE, LANE), F32)],
        compiler_params=_cparams("arbitrary"),
    )(qkv, qkv, qkv, bg, bgt, proj, ogain)


def gdn_chunk_bwd(qkv, bg, bgt, proj, ogain, o_raw, dog, hist, *, hd, name):
    n = qkv.shape[0]
    nb = n // LANE
    nh = hd // LANE
    c = DN_CHUNK

    def body(q_ref, k_ref, v_ref, bg_ref, bgt_ref, g_ref, gain_ref, o_ref, dog_ref, hist_ref,
             dq_ref, dk_ref, dv_ref, dgate_ref, dbg_ref, dgt_ref, dgain_ref, ds_ref):
        first = pl.program_id(0) == 0

        @pl.when(first)
        def _():
            ds_ref[...] = jnp.zeros_like(ds_ref)

        lane = _iota((LANE, LANE), 1)
        row = _iota((LANE, 1), 0)

        def head(h):
            cols = slice(h * LANE, (h + 1) * LANE)
            q, k, v = q_ref[:, cols], k_ref[:, cols], v_ref[:, cols]
            beta = bg_ref[:, h:h + 1]
            cm = yield from _gdn_common(q, k, v, beta, bg_ref[:, nh + h:nh + h + 1], bgt_ref[nh + h:nh + h + 1, :])
            s0 = hist_ref[0, h]
            u_a, s1, u_b, _ = yield from _gdn_states(cm, s0)
            u_all = jnp.concatenate([u_a, u_b], axis=0)
            o, gate, d_out, gain = o_ref[:, cols], g_ref[:, cols], dog_ref[:, cols], gain_ref[...]
            rn = lax.rsqrt(jnp.mean(o * o, axis=-1, keepdims=True) + EPS)
            xh = o * rn
            d_on = d_out * _silu(gate)
            dgate_ref[:, cols] = d_out * xh * gain * _silu_grad(gate)
            dgain = _colsum8(d_on * xh)
            gdy = d_on * gain
            d_o = rn * (gdy - xh * jnp.mean(xh * gdy, axis=-1, keepdims=True))
            pt_do = _mm(cm["pm"], d_o, 0, 0)
            ds_in = ds_ref[h]
            yield
            du_b = _mm(cm["kl"][c:], ds_in, 1, 1) + pt_do[c:]
            dkl_b = _mm(u_b, ds_in)
            dqg_b = _mm(d_o[c:], s1)
            dg1 = jnp.sum(jnp.sum(ds_in * s1, axis=1, keepdims=True), axis=0, keepdims=True)
            dw_b = -_mm(du_b, s1)
            yield
            ds_mid = cm["g1"] * ds_in + _mm(d_o[c:], cm["qg"][c:], 0, 0) - _mm(du_b, cm["w"][c:], 0, 0)
            yield
            du_a = _mm(cm["kl"][:c], ds_mid, 1, 1) + pt_do[:c]
            dkl_a = _mm(u_a, ds_mid)
            dqg_a = _mm(d_o[:c], s0)
            dg0 = jnp.sum(jnp.sum(ds_mid * s0, axis=1, keepdims=True), axis=0, keepdims=True)
            yield
            dw_a = -_mm(du_a, s0)
            ds_ref[h] = cm["g0"] * ds_mid + _mm(d_o[:c], cm["qg"][:c], 0, 0) - _mm(du_a, cm["w"][:c], 0, 0)
            du = jnp.concatenate([du_a, du_b], axis=0)
            dkl = jnp.concatenate([dkl_a, dkl_b], axis=0)
            dqg = jnp.concatenate([dqg_a, dqg_b], axis=0)
            dw = jnp.concatenate([dw_a, dw_b], axis=0)
            t, d, gam, lam = cm["t"], cm["d"], cm["gam"], cm["lam"]
            dp = jnp.where(cm["incl"], _mm(d_o, u_all, 1, 1), 0.0)
            dt = _mm(dw, cm["kb"], 1, 1) + _mm(du, cm["vb"], 1, 1)
            dkb = _mm(t, dw, 0, 0)
            dvb = _mm(t, du, 0, 0)
            yield
            x_t = _mm(t, dt, 0, 0)
            yield
            da = jnp.where(cm["strict"], -_mm(x_t, t, 1, 1), 0.0)
            yield
            kb_k = jnp.sum(dkb * k, axis=1, keepdims=True)
            dbeta = jnp.sum(da * cm["ahat"], axis=1, keepdims=True) + gam * kb_k + jnp.sum(dvb * v, axis=1, keepdims=True)
            dahat = da * beta
            dkk = dahat * d
            dqk = dp * d
            e = (dahat * cm["kk"] + dp * cm["qk"]) * d
            dk_ref[:, cols] = (_mm(dkk, k) + _mm(dkk, k, 0, 0) + _mm(dqk, q, 0, 0) + dkb * (beta * gam) + dkl * lam)
            dq_ref[:, cols] = _mm(dqk, k) + dqg * gam
            dv_ref[:, cols] = dvb * beta
            dgam = beta * kb_k + jnp.sum(dqg * q, axis=1, keepdims=True)
            dlam_lam = jnp.sum(dkl * k, axis=1, keepdims=True) * lam
            dgl0 = jnp.sum(jnp.where(row < c, dlam_lam, 0.0), axis=0, keepdims=True) + dg0 * cm["g0"]
            dgl1 = jnp.sum(jnp.where(row >= c, dlam_lam, 0.0), axis=0, keepdims=True) + dg1 * cm["g1"]
            dgc = (jnp.sum(e, axis=1, keepdims=True) + dgam * gam - dlam_lam
                   + jnp.where(row == c - 1, dgl0, 0.0) + jnp.where(row == LANE - 1, dgl1, 0.0))
            dgt_ref[h:h + 1, :] = -jnp.sum(e, axis=0, keepdims=True)
            return jnp.where(lane == h, dbeta, 0.0) + jnp.where(lane == nh + h, dgc, 0.0), dgain

        parts = _lockstep(head(h) for h in range(nh))
        dbg_ref[...] = sum(p[0] for p in parts)
        _accum(dgain_ref, sum(p[1] for p in parts), first)

    rev = lambda b: nb - 1 - b
    col = lambda cc: pl.BlockSpec((LANE, hd), lambda b: (rev(b), cc))
    return pl.pallas_call(
        body, name=name, grid=(nb,),
        in_specs=[col(0), col(1), col(2), pl.BlockSpec((LANE, LANE), lambda b: (rev(b), 0)),
                  pl.BlockSpec((2 * nh, LANE), lambda b: (0, rev(b))), pl.BlockSpec((LANE, hd), lambda b: (rev(b), 3)),
                  pl.BlockSpec((1, LANE), lambda b: (0, 0)), col(0), col(0),
                  pl.BlockSpec((1, nh, LANE, LANE), lambda b: (rev(b), 0, 0, 0))],
        out_specs=[col(0), col(0), col(0), col(0), pl.BlockSpec((LANE, LANE), lambda b: (rev(b), 0)),
                   pl.BlockSpec((nh, LANE), lambda b: (0, rev(b))), pl.BlockSpec((SUB, LANE), lambda b: (0, 0))],
        out_shape=[jax.ShapeDtypeStruct((n, hd), F32)] * 4 + [jax.ShapeDtypeStruct((n, LANE), F32),
                   jax.ShapeDtypeStruct((nh, n), F32), jax.ShapeDtypeStruct((SUB, LANE), F32)],
        scratch_shapes=[pltpu.VMEM((nh, LANE, LANE), F32)],
        compiler_params=_cparams("arbitrary"),
    )(qkv, qkv, qkv, bg, bgt, proj, ogain, o_raw, dog, hist)


def gdn_gates_bwd(proj, alog, dtb, dbg, *, hd, pad, name):
    n = proj.shape[0]
    gcol = 4 * (hd // LANE)

    def body(x_ref, a_ref, d_ref, dbg_ref, o_ref, da_ref, dd_ref):
        i = pl.program_id(0)
        x = x_ref[...]
        lane = _iota(x.shape, 1)
        rows = i * LANE + _iota((LANE, 1), 0)
        isg = (lane >= DN_HEADS) & (lane < 2 * DN_HEADS)
        dbgv = jnp.where(rows >= pad, dbg_ref[...], 0.0)
        dg = jnp.dot(_chunk_tri(True), jnp.where(isg, dbgv, 0.0), precision=HI, preferred_element_type=F32)
        ea = jnp.exp(a_ref[...])
        z = x + d_ref[...]
        dg = jnp.where(rows >= pad, dg, 0.0)
        dz = jnp.where(isg, dg * (-ea) * _sigmoid(z), 0.0)
        sb = _sigmoid(x)
        o_ref[...] = jnp.where(lane < DN_HEADS, dbgv * sb * (1.0 - sb), dz).astype(o_ref.dtype)
        _accum(da_ref, _colsum8(jnp.where(isg, dg * (-ea) * _softplus(z), 0.0)), i == 0)
        _accum(dd_ref, _colsum8(dz), i == 0)

    vec = pl.BlockSpec((1, LANE), lambda i: (0, 0))
    blk = pl.BlockSpec((LANE, LANE), lambda i: (i, 0))
    acc = pl.BlockSpec((SUB, LANE), lambda i: (0, 0))
    return pl.pallas_call(
        body, name=name, grid=(n // LANE,),
        in_specs=[pl.BlockSpec((LANE, LANE), lambda i: (i, gcol)), vec, vec, blk],
        out_specs=[blk, acc, acc],
        out_shape=[jax.ShapeDtypeStruct((n, LANE), BF16), jax.ShapeDtypeStruct((SUB, LANE), F32), jax.ShapeDtypeStruct((SUB, LANE), F32)],
        compiler_params=_cparams("arbitrary"),
    )(proj, alog, dtb, dbg)


def gdn_prep_bwd(proj, cw, dqkv, *, hd, name):
    n = proj.shape[0]
    tr = _tile(n, 640, SUB)
    nh = hd // LANE
    r8 = tr // SUB
    qscale = DN_HEAD_DIM ** -0.5

    def body(x_ref, p_ref, w_ref, dq_ref, dk_ref, dv_ref, o_ref, dw_ref):
        c, i = pl.program_id(0), pl.program_id(1)
        first = i == 0
        prev = jnp.where(first, 0.0, p_ref[...])
        cur = x_ref[...]
        cv = _conv_rows(cur, prev, w_ref[...])
        s = _silu(cv)
        r = lax.rsqrt(jnp.sum(s * s, axis=-1, keepdims=True) + EPS)
        y = s * r
        dy = jnp.where(c < nh, dq_ref[...] * qscale, dk_ref[...])
        ds_norm = r * (dy - y * jnp.sum(dy * y, axis=-1, keepdims=True))
        dcv = jnp.where(c < 2 * nh, ds_norm, dv_ref[...]) * _silu_grad(cv)
        o_ref[...] = dcv
        for k in range(DN_CONV):
            part = _colsum8(dcv * _shift_down(cur, prev, DN_CONV - 1 - k))

            @pl.when(first)
            def _():
                dw_ref[k] = part

            @pl.when(jnp.logical_not(first))
            def _():
                dw_ref[k] += part

    blk = lambda f: pl.BlockSpec((tr, LANE), f)
    return pl.pallas_call(
        body, name=name, grid=(3 * nh, n // tr),
        in_specs=[blk(lambda c, i: (i, c)), pl.BlockSpec((SUB, LANE), lambda c, i: (jnp.maximum(i * r8 - 1, 0), c)),
                  pl.BlockSpec((DN_CONV, LANE), lambda c, i: (0, c)),
                  blk(lambda c, i: (i, jnp.minimum(c, nh - 1))), blk(lambda c, i: (i, jnp.clip(c - nh, 0, nh - 1))),
                  blk(lambda c, i: (i, jnp.clip(c - 2 * nh, 0, nh - 1)))],
        out_specs=[blk(lambda c, i: (i, c)), pl.BlockSpec((DN_CONV, SUB, LANE), lambda c, i: (0, 0, c))],
        out_shape=[jax.ShapeDtypeStruct((n, 3 * hd), F32), jax.ShapeDtypeStruct((DN_CONV, SUB, 3 * hd), F32)],
        compiler_params=_cparams("parallel", "arbitrary"),
    )(proj, proj, cw, *dqkv)


def loss_head(h, target, *, x0, name):
    n, d = h.shape
    tr = LANE
    nb0 = x0 // tr

    def body(h_ref, t_ref, dh_ref, sq_ref):
        i = pl.program_id(0)
        live = i >= nb0
        err = jnp.where(live, h_ref[...] - t_ref[...], 0.0)
        dh_ref[...] = err * (1.0 / d)
        _accum(sq_ref, _colsum8(err * err), i == 0)

    row = pl.BlockSpec((tr, d), lambda i: (i, 0))
    return pl.pallas_call(
        body, name=name, grid=(n // tr,),
        in_specs=[row, pl.BlockSpec((tr, d), lambda i: (jnp.maximum(i - nb0, 0), 0))],
        out_specs=[row, pl.BlockSpec((SUB, d), lambda i: (0, 0))],
        out_shape=[jax.ShapeDtypeStruct((n, d), F32), jax.ShapeDtypeStruct((SUB, d), F32)],
        compiler_params=_cparams("arbitrary"),
    )(h, target)


def adamw(w, g, m, v, *, name):
    r, c = w.shape
    tr = _tile(r, 512, SUB) if r % SUB == 0 else r
    c1 = 1.0 / (1.0 - ADAM_B1 ** ADAM_STEP)
    c2 = 1.0 / (1.0 - ADAM_B2 ** ADAM_STEP)

    def body(w_ref, g_ref, m_ref, v_ref, d_ref, mo_ref, vo_ref):
        gv = g_ref[...]
        mn = ADAM_B1 * m_ref[...] + (1.0 - ADAM_B1) * gv
        vn = ADAM_B2 * v_ref[...] + (1.0 - ADAM_B2) * (gv * gv)
        d_ref[...] = -ADAM_LR * ((mn * c1) / (jnp.sqrt(vn * c2) + ADAM_EPS) + ADAM_WD * w_ref[...])
        mo_ref[...] = mn
        vo_ref[...] = vn

    blk = pl.BlockSpec((tr, c), lambda i: (i, 0))
    return pl.pallas_call(
        body, name=name, grid=(r // tr,), in_specs=[blk] * 4, out_specs=[blk] * 3,
        out_shape=[jax.ShapeDtypeStruct((r, c), F32)] * 3, compiler_params=_cparams("parallel"),
    )(w, g, m, v)


_BIG = ("attn_w_in", "attn_w_out", "dn_w_in", "dn_w_out", "ffn_w_up", "ffn_w_down")


def _row(v, width=None):
    v = v.astype(F32).reshape(1, -1)
    if width is not None and v.shape[1] < width:
        v = jnp.pad(v, ((0, 0), (0, width - v.shape[1])))
    return v


def _fold8(p):
    return jnp.sum(p, axis=-2)


def local_step(x, target, w):
    seq, d = x.shape
    pad = (-(N_META + seq)) % LANE
    x0 = pad + N_META
    n = x0 + seq
    depth = w["g_pre"].shape[0]
    hd_a = ATT_HEADS * ATT_HEAD_DIM
    hd_d = DN_HEADS * DN_HEAD_DIM
    t_att = _tile(n, 640)
    h = jnp.concatenate([jnp.zeros((pad, d), F32), w["meta"].astype(F32), x], axis=0)
    saved = []
    for i in range(depth):
        j = i // 2
        s = dict(h=h)
        s["a"] = rms_fwd(h, _row(w["g_pre"][i]), out_dtype=BF16, name="rms_pre")
        if i % 2 == 0:
            s["proj"] = proj = matmul(s["a"], w["attn_w_in"][j], name="mm_attn_in")
            qg, kg = _row(jnp.tile(w["attn_qg"][j], ATT_HEADS)), _row(jnp.tile(w["attn_kg"][j], ATT_HEADS))
            bf = _row(w["attn_b"][j], LANE)
            s["qs"], s["kn"], s["v"], logf = attn_prep(proj, qg, kg, bf, hd=hd_a, name="attn_prep")
            c = cumsum_rows(logf, reverse=False, name="cumsum_fwd")
            s["ct"] = c[:, :ATT_HEADS].T.reshape(ATT_HEADS // 2, 2, n)
            s["o"], s["og"], s["lse"] = flash_fwd(s["qs"], s["kn"], s["v"], proj, s["ct"], pad=pad, t=t_att, name="flash_fwd")
            s["m"] = matmul(s["og"], w["attn_w_out"][j], name="mm_attn_out")
        else:
            s["proj"] = proj = matmul(s["a"], w["dn_w_in"][j], name="mm_dn_in")
            s["qkv"] = gdn_prep(proj, w["dn_conv"][j], hd=hd_d, name="gdn_prep")
            alog = jnp.pad(_row(w["dn_alog"][j]), ((0, 0), (DN_HEADS, LANE - 2 * DN_HEADS)))
            dtb = jnp.pad(_row(w["dn_dtb"][j]), ((0, 0), (DN_HEADS, LANE - 2 * DN_HEADS)))
            s["bg"] = gdn_gates(proj, alog, dtb, hd=hd_d, name="gdn_gates")
            s["bgt"] = s["bg"][:, :2 * DN_HEADS].T
            s["o"], s["og"], s["hist"] = gdn_chunk_fwd(s["qkv"], s["bg"], s["bgt"], proj, _row(w["dn_og"][j]), hd=hd_d, name="gdn_fwd")
            s["m"] = matmul(s["og"], w["dn_w_out"][j], name="mm_dn_out")
        s["h_mid"] = rms_fwd(s["m"], _row(w["g_post"][i]), res=h, out_dtype=F32, name="rms_post")
        s["b"] = rms_fwd(s["h_mid"], _row(w["g_fpre"][i]), out_dtype=BF16, name="rms_fpre")
        s["u"] = matmul(s["b"], w["ffn_w_up"][i], out_dtype=BF16, name="mm_ffn_up")
        s["act"] = conv_glu_fwd(s["u"], w["ffn_conv"][i], name="ffn_glu")
        s["f"] = matmul(s["act"], w["ffn_w_down"][i], name="mm_ffn_down")
        h = rms_fwd(s["f"], _row(w["g_fpost"][i]), res=s["h_mid"], out_dtype=F32, name="rms_fpost")
        saved.append(s)

    dh, sq = loss_head(h, target, x0=x0, name="loss_head")
    loss = 0.5 * jnp.sum(sq) / d

    g = {k: [None] * depth for k in ("g_pre", "g_post", "g_fpre", "g_fpost", "ffn_w_up", "ffn_conv", "ffn_w_down")}
    for k in ("attn_w_in", "attn_b", "attn_qg", "attn_kg", "attn_w_out", "dn_w_in", "dn_conv", "dn_alog", "dn_dtb", "dn_og", "dn_w_out"):
        g[k] = [None] * (depth // 2)
    for i in reversed(range(depth)):
        j = i // 2
        s = saved[i]
        proj = s["proj"]
        df, p8 = rms_bwd(s["f"], _row(w["g_fpost"][i]), dh, pad=pad, name="rms_fpost_bwd")
        g["g_fpost"][i] = _fold8(p8)
        dact = matmul(df, w["ffn_w_down"][i], trans_b=True, out_dtype=BF16, name="mm_ffn_down_dx")
        g["ffn_w_down"][i] = matmul(s["act"], df, trans_a=True, out_dtype=BF16, name="mm_ffn_down_dw")
        d_gate, d_up, pg8, pu8 = conv_glu_bwd(s["u"], w["ffn_conv"][i], dact, name="ffn_glu_bwd")
        g["ffn_conv"][i] = jnp.concatenate([_fold8(pg8), _fold8(pu8)], axis=-1)
        du = conv_transpose(d_gate, w["ffn_conv"][i], dy_hi=d_up, out_dtype=BF16, name="ffn_conv_t")
        db = matmul(du, w["ffn_w_up"][i], trans_b=True, name="mm_ffn_up_dx")
        g["ffn_w_up"][i] = matmul(s["b"], du, trans_a=True, out_dtype=BF16, name="mm_ffn_up_dw")
        dh_mid, p8 = rms_bwd(s["h_mid"], _row(w["g_fpre"][i]), db, res=dh, pad=pad, name="rms_fpre_bwd")
        g["g_fpre"][i] = _fold8(p8)
        dm, p8 = rms_bwd(s["m"], _row(w["g_post"][i]), dh_mid, pad=pad, name="rms_post_bwd")
        g["g_post"][i] = _fold8(p8)
        if i % 2 == 0:
            g["attn_w_out"][j] = matmul(s["og"], dm, trans_a=True, out_dtype=BF16, name="mm_attn_out_dw")
            dgated = matmul(dm, w["attn_w_out"][j], trans_b=True, name="mm_attn_out_dx")
            do, delta, dgate = attn_bwd_prep(dgated, s["o"], proj, hd=hd_a, name="attn_bwd_prep")
            dqs, dkn, dv, dck, dcr = flash_bwd(s["qs"], s["kn"], s["v"], do, s["lse"], delta, s["ct"], pad=pad, t=t_att, name="flash_bwd")
            dc = (dcr - dck)[:, ::ATT_HEAD_DIM].reshape(n, ATT_HEADS // 2, 2)[:, :, ::-1].reshape(n, ATT_HEADS)
            dc = jnp.pad(dc, ((0, 0), (0, LANE - ATT_HEADS)))
            dlogf = cumsum_rows(dc, reverse=True, name="cumsum_bwd")
            qg, kg = _row(jnp.tile(w["attn_qg"][j], ATT_HEADS)), _row(jnp.tile(w["attn_kg"][j], ATT_HEADS))
            bf = _row(w["attn_b"][j], LANE)
            dq_raw, dk_raw, dfl, gq8, gk8, gb8 = attn_in_bwd(dqs, dkn, proj, qg, kg, bf, dlogf, hd=hd_a, pad=pad, name="attn_in_bwd")
            g["attn_qg"][j] = _fold8(gq8).reshape(ATT_HEADS, ATT_HEAD_DIM).sum(axis=0)
            g["attn_kg"][j] = _fold8(gk8).reshape(ATT_HEADS, ATT_HEAD_DIM).sum(axis=0)
            g["attn_b"][j] = _fold8(gb8)[:ATT_HEADS]
            dproj = jnp.concatenate([dq_raw, dk_raw, dv.astype(BF16), dgate.astype(BF16), dfl], axis=1)
            w_in, key = w["attn_w_in"][j], "attn_w_in"
        else:
            g["dn_w_out"][j] = matmul(s["og"], dm, trans_a=True, out_dtype=BF16, name="mm_dn_out_dw")
            dgated = matmul(dm, w["dn_w_out"][j], trans_b=True, name="mm_dn_out_dx")
            alog = jnp.pad(_row(w["dn_alog"][j]), ((0, 0), (DN_HEADS, LANE - 2 * DN_HEADS)))
            dtb = jnp.pad(_row(w["dn_dtb"][j]), ((0, 0), (DN_HEADS, LANE - 2 * DN_HEADS)))
            dq, dk, dv, dgate, dbg, dgt, gain8 = gdn_chunk_bwd(s["qkv"], s["bg"], s["bgt"], proj, _row(w["dn_og"][j]), s["o"], dgated,
                                                              s["hist"], hd=hd_d, name="gdn_bwd")
            g["dn_og"][j] = _fold8(gain8)
            dbg = dbg + jnp.pad(dgt.T, ((0, 0), (DN_HEADS, LANE - 2 * DN_HEADS)))
            dgl, da8, dd8 = gdn_gates_bwd(proj, alog, dtb, dbg, hd=hd_d, pad=pad, name="gdn_gates_bwd")
            g["dn_alog"][j] = _fold8(da8)[DN_HEADS:2 * DN_HEADS]
            g["dn_dtb"][j] = _fold8(dd8)[DN_HEADS:2 * DN_HEADS]
            dcv, p8 = gdn_prep_bwd(proj, w["dn_conv"][j], (dq, dk, dv), hd=hd_d, name="gdn_prep_bwd")
            g["dn_conv"][j] = _fold8(p8)
            dqkv = conv_transpose(dcv, w["dn_conv"][j], out_dtype=BF16, name="gdn_conv_t")
            dproj = jnp.concatenate([dqkv, dgate.astype(BF16), dgl], axis=1)
            w_in, key = w["dn_w_in"][j], "dn_w_in"
        da = matmul(dproj, w_in, trans_b=True, name="mm_in_dx")
        g[key][j] = matmul(s["a"], dproj, trans_a=True, out_dtype=BF16, name="mm_in_dw")
        dh, p8 = rms_bwd(s["h"], _row(w["g_pre"][i]), da, res=dh_mid, pad=pad, name="rms_pre_bwd")
        g["g_pre"][i] = _fold8(p8)

    grads = {k: (v if k in _BIG else jnp.stack(v)) for k, v in g.items()}
    grads["meta"] = dh[pad:x0]
    return loss, dh[x0:], grads


_ANY = pl.BlockSpec(memory_space=pl.ANY)


def _mesh_place():
    x, y, c = lax.axis_index("x"), lax.axis_index("y"), lax.axis_index("c")
    return x, y, c, 4 * x + 2 * y + c


def _peer(x, y, c, k):
    px, py, pc = (1 - x if k & 4 else x), (1 - y if k & 2 else y), (1 - c if k & 1 else c)
    return (px, py, pc), 4 * px + 2 * py + pc


def _window_blocks(shard):
    return max(-(-(shard * (d + 1)) // LANE) - (shard * d) // LANE for d in range(N_DEV))


def _sds(shape, dtype):
    return jax.ShapeDtypeStruct(tuple(shape), dtype)


def _plan_gather(buf):
    return _sds((N_DEV,) + buf.shape, buf.dtype), (lambda r, i: r), (lambda o, i: o.at[i])


def _plan_scatter(buf):
    return _sds(buf.shape, buf.dtype), (lambda r, i: r.at[i]), (lambda o, i: o.at[i])


def _plan_gather_rows(buf):
    l, r, c = buf.shape
    return _sds((l, N_DEV * r, c), buf.dtype), (lambda ref, i: ref), (lambda o, i: o.at[:, pl.ds(pl.multiple_of(i * r, SUB), r), :])


def _plan_scatter_rows(buf):
    r, c = buf.shape[0] // N_DEV, buf.shape[1]
    return _sds((N_DEV, r, c), buf.dtype), (lambda ref, i: ref.at[pl.ds(pl.multiple_of(i * r, SUB), r), :]), (lambda o, i: o.at[i])


def _plan_scatter_cols(buf, shard):
    ww = _window_blocks(shard) * LANE
    src = lambda ref, i: ref.at[:, pl.ds(pl.multiple_of((shard * i) // LANE * LANE, LANE), ww)]
    return _sds((N_DEV, buf.shape[0], ww), buf.dtype), src, (lambda o, i: o.at[i])


def exchange(bufs, plans, *, name, relay=False):
    nbuf = len(bufs)
    far = (2, 4, 6)

    def body(*refs):
        ins, outs = refs[:nbuf], refs[nbuf:2 * nbuf]
        send_sems, recv_sems, loc_sems = refs[2 * nbuf:2 * nbuf + 3]
        x, y, c, me = _mesh_place()
        local = [pltpu.make_async_copy(plans[b][1](ins[b], me), plans[b][2](outs[b], me), loc_sems.at[b]) for b in range(nbuf)]
        for cp in local:
            cp.start()
        sends, recvs = [], {}
        for k in ((1,) + far if relay else range(1, N_DEV)):
            peer, pidx = _peer(x, y, c, k)
            for b in range(nbuf):
                sems = dict(send_sem=send_sems.at[b, k - 1], recv_sem=recv_sems.at[b, k - 1], device_id=peer,
                            device_id_type=pl.DeviceIdType.MESH)
                cp = pltpu.make_async_remote_copy(src_ref=plans[b][1](ins[b], pidx), dst_ref=plans[b][2](outs[b], me), **sems)
                cp.start()
                sends.append(cp)
                recvs[b, k] = pltpu.make_async_remote_copy(src_ref=plans[b][1](ins[b], pidx), dst_ref=plans[b][2](outs[b], pidx), **sems)
        if relay:
            fwd_send, fwd_recv = refs[2 * nbuf + 3:]
            sibling, _ = _peer(x, y, c, 1)
            for j, k in enumerate(far):
                _, pidx = _peer(x, y, c, k)
                _, qidx = _peer(x, y, c, k + 1)
                for b in range(nbuf):
                    recvs.pop((b, k)).wait_recv()
                    sems = dict(send_sem=fwd_send.at[b, j], recv_sem=fwd_recv.at[b, j], device_id=sibling,
                                device_id_type=pl.DeviceIdType.MESH)
                    landed = plans[b][2](outs[b], pidx)
                    cp = pltpu.make_async_remote_copy(src_ref=landed, dst_ref=landed, **sems)
                    cp.start()
                    sends.append(cp)
                    there = plans[b][2](outs[b], qidx)
                    recvs[b, -k] = pltpu.make_async_remote_copy(src_ref=there, dst_ref=there, **sems)
        for cp in recvs.values():
            cp.wait_recv()
        for cp in sends:
            cp.wait_send()
        for cp in local:
            cp.wait()

    sem_shapes = [pltpu.SemaphoreType.DMA((nbuf, N_DEV - 1)), pltpu.SemaphoreType.DMA((nbuf, N_DEV - 1)), pltpu.SemaphoreType.DMA((nbuf,))]
    if relay:
        sem_shapes += [pltpu.SemaphoreType.DMA((nbuf, len(far))), pltpu.SemaphoreType.DMA((nbuf, len(far)))]
    return pl.pallas_call(
        body, name=name, in_specs=[_ANY] * nbuf, out_specs=[_ANY] * nbuf, out_shape=[p[0] for p in plans],
        scratch_shapes=sem_shapes,
        compiler_params=pltpu.CompilerParams(has_side_effects=True),
    )(*bufs)


def slot_sum(x, *, name):
    _, r, c = x.shape
    tr = _tile(r, 512, 16)

    def body(x_ref, o_ref):
        acc = x_ref[0].astype(F32)
        for d in range(1, N_DEV):
            acc = acc + x_ref[d].astype(F32)
        o_ref[...] = acc

    return pl.pallas_call(
        body, name=name, grid=(r // tr,), in_specs=[pl.BlockSpec((N_DEV, tr, c), lambda i: (0, i, 0))],
        out_specs=pl.BlockSpec((tr, c), lambda i: (i, 0)), out_shape=jax.ShapeDtypeStruct((r, c), F32),
        compiler_params=_cparams("parallel"),
    )(x)


def assemble_cols(win, shard, *, name):
    _, r, ww = win.shape
    wb = ww // LANE
    nbo = -(-(N_DEV * shard) // LANE)
    hits = [[(d, b - (shard * d) // LANE) for d in range(N_DEV) if 0 <= b - (shard * d) // LANE < wb] for b in range(nbo)]
    assert all(1 <= len(h) <= 2 for h in hits), hits
    tr = _tile(r, 512, 16)

    def body(w_ref, o_ref):
        for b, hit in enumerate(hits):
            blk = w_ref[hit[0][0], :, hit[0][1] * LANE:(hit[0][1] + 1) * LANE]
            for d, lb in hit[1:]:
                blk = blk + w_ref[d, :, lb * LANE:(lb + 1) * LANE]
            o_ref[:, b * LANE:(b + 1) * LANE] = blk

    return pl.pallas_call(
        body, name=name, grid=(r // tr,), in_specs=[pl.BlockSpec((N_DEV, tr, ww), lambda i: (0, i, 0))],
        out_specs=pl.BlockSpec((tr, nbo * LANE), lambda i: (i, 0)), out_shape=jax.ShapeDtypeStruct((r, nbo * LANE), win.dtype),
        compiler_params=_cparams("parallel"))(win)


def _pack(parts, dtype, lead=()):
    nl = len(lead)
    flat = jnp.concatenate([p.astype(dtype).reshape(lead + (-1,)) for p in parts], axis=nl)
    tot = flat.shape[nl]
    rows = -(-tot // (16 * LANE)) * 16
    flat = jnp.pad(flat, [(0, 0)] * nl + [(0, rows * LANE - tot)])
    return flat.reshape(lead + (rows, LANE))


def _unpack(buf, shapes, lead=()):
    nl = len(lead)
    flat = buf.reshape(lead + (-1,))
    out, off = [], 0
    for shp in shapes:
        size = 1
        for s in shp:
            size *= s
        out.append(lax.slice_in_dim(flat, off, off + size, axis=nl).reshape(lead + tuple(shp)))
        off += size
    return out


def _whole(g8, axis):
    t = jnp.moveaxis(g8, 0, axis)
    shp = t.shape
    return t.reshape(shp[:axis] + (shp[axis] * shp[axis + 1],) + shp[axis + 2:])


def _slots(full, axis):
    shp = full.shape
    t = full.reshape(shp[:axis] + (N_DEV, shp[axis] // N_DEV) + shp[axis + 1:])
    return jnp.moveaxis(t, axis, 0)


_PARAMS = (("meta_tokens", 1), ("norm_mix_pre", None), ("norm_mix_post", None), ("norm_ffn_pre", None), ("norm_ffn_post", None),
           ("attn_w_in", 2), ("attn_b_forget", None), ("attn_q_norm", None), ("attn_k_norm", None), ("attn_w_out", 1), ("dn_w_in", 2),
           ("dn_conv", 2), ("dn_a_log", None), ("dn_dt_bias", None), ("dn_o_norm", None), ("dn_w_out", 1), ("ffn_w_up", 2),
           ("ffn_conv", 2), ("ffn_w_down", 1))
_LOCAL_KEY = dict(meta_tokens="meta", norm_mix_pre="g_pre", norm_mix_post="g_post", norm_ffn_pre="g_fpre", norm_ffn_post="g_fpost",
                  attn_w_in="attn_w_in", attn_b_forget="attn_b", attn_q_norm="attn_qg", attn_k_norm="attn_kg", attn_w_out="attn_w_out",
                  dn_w_in="dn_w_in", dn_conv="dn_conv", dn_a_log="dn_alog", dn_dt_bias="dn_dtb", dn_o_norm="dn_og", dn_w_out="dn_w_out",
                  ffn_w_up="ffn_w_up", ffn_conv="ffn_conv", ffn_w_down="ffn_w_down")
_COL_CUT = ("attn_w_in", "dn_w_in", "ffn_w_up")
_ROW_CUT = ("attn_w_out", "dn_w_out", "ffn_w_down")


def kernel(x, meta_tokens, norm_mix_pre, norm_mix_post, norm_ffn_pre, norm_ffn_post, attn_w_in, attn_b_forget, attn_q_norm, attn_k_norm, attn_w_out, dn_w_in, dn_conv, dn_a_log, dn_dt_bias, dn_o_norm, dn_w_out, ffn_w_up, ffn_conv, ffn_w_down, loss_target, m_meta_tokens, m_norm_mix_pre, m_norm_mix_post, m_norm_ffn_pre, m_norm_ffn_post, m_attn_w_in, m_attn_b_forget, m_attn_q_norm, m_attn_k_norm, m_attn_w_out, m_dn_w_in, m_dn_conv, m_dn_a_log, m_dn_dt_bias, m_dn_o_norm, m_dn_w_out, m_ffn_w_up, m_ffn_conv, m_ffn_w_down, v_meta_tokens, v_norm_mix_pre, v_norm_mix_post, v_norm_ffn_pre, v_norm_ffn_post, v_attn_w_in, v_attn_b_forget, v_attn_q_norm, v_attn_k_norm, v_attn_w_out, v_dn_w_in, v_dn_conv, v_dn_a_log, v_dn_dt_bias, v_dn_o_norm, v_dn_w_out, v_ffn_w_up, v_ffn_conv, v_ffn_w_down):
    given = dict(locals())
    names = [p[0] for p in _PARAMS]
    cut = [p for p in _PARAMS if p[1] is not None and p[0] not in _BIG]
    rep = [p for p in _PARAMS if p[1] is None]
    me = 4 * lax.axis_index("x") + 2 * lax.axis_index("y") + lax.axis_index("c")

    bufs, plans = [], []
    for n in _COL_CUT:
        layers, d, shard = given[n].shape
        win = jnp.zeros((layers * d, _window_blocks(shard) * LANE), BF16)
        win = lax.dynamic_update_slice(win, given[n].astype(BF16).reshape(layers * d, shard), (0, (shard * me) % LANE))
        bufs.append(win)
        plans.append(_plan_gather(win))
    for n in _ROW_CUT:
        bufs.append(given[n].astype(BF16))
        plans.append(_plan_gather_rows(bufs[-1]))
    bufs.append(_pack([given[n] for n, _ in cut], F32))
    plans.append(_plan_gather(bufs[-1]))
    got = exchange(bufs, plans, name="gather_weights", relay=True)
    w = {}
    for n, g8 in zip(_COL_CUT, got[:3]):
        layers, d, shard = given[n].shape
        w[n] = assemble_cols(g8, shard, name="assemble_" + n).reshape(layers, d, -1)
    for n, full in zip(_ROW_CUT, got[3:6]):
        w[n] = full
    for (n, axis), g8 in zip(cut, _unpack(got[6], [given[n].shape for n, _ in cut], lead=(N_DEV,))):
        w[_LOCAL_KEY[n]] = _whole(g8, axis)
    for n, _ in rep:
        w[_LOCAL_KEY[n]] = given[n]

    loss, grad_x, g = local_step(x[0], loss_target[0], w)

    bufs, plans, what = [], [], []
    for n in _COL_CUT:
        for layer, gl in enumerate(g[n]):
            bufs.append(gl)
            plans.append(_plan_scatter_cols(gl, given[n].shape[2]))
            what.append((n, layer))
    for n in _ROW_CUT:
        for layer, gl in enumerate(g[n]):
            bufs.append(gl)
            plans.append(_plan_scatter_rows(gl))
            what.append((n, layer))
    nbig = len(bufs)
    bufs.append(_pack([_slots(g[_LOCAL_KEY[n]], axis) for n, axis in cut], F32, lead=(N_DEV,)))
    plans.append(_plan_scatter(bufs[-1]))
    bufs.append(_pack([g[_LOCAL_KEY[n]] for n, _ in rep] + [loss.reshape(1)], F32))
    plans.append(_plan_gather(bufs[-1]))
    got = exchange(bufs, plans, name="reduce_grads")
    per_layer = {n: [] for n in _BIG}
    for (n, layer), r8 in zip(what, got[:nbig]):
        tot = slot_sum(r8, name="sum_" + n)
        if n in _COL_CUT:
            shard = given[n].shape[2]
            tot = lax.dynamic_slice_in_dim(tot, (shard * me) % LANE, shard, axis=1)
        per_layer[n].append(tot)
    grads = {n: jnp.stack(v) for n, v in per_layer.items()}
    for (n, _), gv in zip(cut, _unpack(slot_sum(got[nbig], name="sum_cut"), [given[n].shape for n, _ in cut])):
        grads[n] = gv
    rep_sum = _unpack(slot_sum(got[nbig + 1], name="sum_rep"), [given[n].shape for n, _ in rep] + [(1,)])
    for (n, _), gv in zip(rep, rep_sum):
        grads[n] = gv
    loss_all = rep_sum[-1].reshape(())

    deltas, new_m, new_v = {}, {}, {}
    for n in names:
        shp = given[n].shape
        two_d = (-1, shp[-1])
        d, mn, vn = adamw(given[n].reshape(two_d), grads[n].reshape(two_d), given["m_" + n].reshape(two_d), given["v_" + n].reshape(two_d),
                          name="adamw_" + n)
        deltas[n], new_m[n], new_v[n] = d.reshape(shp), mn.reshape(shp), vn.reshape(shp)
    return (loss_all, grad_x[None], *[grads[n] for n in names], *[deltas[n] for n in names], *[new_m[n] for n in names],
            *[new_v[n] for n in names])
```
